```python
import jax, jax.numpy as jnp
from jax import lax
import numpy as np

D_MODEL = 2048
BATCH = 4
SEQ = 2048
DEPTH = 4
DEC_BATCH = 8
DEC_SEQ = 1
PAST_LEN = 16384
PAGE_SIZE = 128

N_MIXERS = 3
N_CONV_LAYERS = (DEPTH + 2) // N_MIXERS
N_HGRN_LAYERS = (DEPTH + 1) // N_MIXERS
N_ATTN_LAYERS = DEPTH // N_MIXERS
CONV_W = 3
CONV_DIM = D_MODEL
HG_DK = 128
HG_HEADS = D_MODEL // HG_DK
HG_DV = D_MODEL // HG_HEADS
GLA_CHUNK = 64
ATT_DH = 128
ATT_HEADS = D_MODEL // ATT_DH
DILATED_GROUPS = ((128, 1), (512, 4), (2048, 16))
N_GROUPS = 3
ROPE_DIM = ATT_DH // 4
ROPE_THETA = 500000.0
MEM_LEN = 256
XA_HEADS = 4
XA_DH = 128
D_FF = 5632
EPS = 1e-6

kernel_name = 'hybrid_conv_hgrn2_dilated_decoder_step'


def rms_norm(x, g):
    xf = x.astype(jnp.float32)
    y = xf * lax.rsqrt(jnp.mean(xf * xf, axis=-1, keepdims=True) + EPS)
    return (y * g.astype(jnp.float32)).astype(x.dtype)


def swiglu(x, w_gu, w_down):
    gate, up = jnp.split(x @ w_gu, 2, axis=-1)
    return (jax.nn.silu(gate) * up) @ w_down


def partial_rotary(x, pos):
    half = ROPE_DIM // 2
    inv_freq = ROPE_THETA ** (-jnp.arange(half, dtype=jnp.float32) * 2.0 / ROPE_DIM)
    ang = pos.astype(jnp.float32)[:, None] * inv_freq[None, :]
    cos = jnp.cos(ang)[None, :, None, :]
    sin = jnp.sin(ang)[None, :, None, :]
    x1 = x[..., :half].astype(jnp.float32)
    x2 = x[..., half:ROPE_DIM].astype(jnp.float32)
    rot = jnp.concatenate([x1 * cos - x2 * sin, x2 * cos + x1 * sin], axis=-1).astype(x.dtype)
    return jnp.concatenate([rot, x[..., ROPE_DIM:]], axis=-1)


def short_conv_mixer(h, w_in, w_conv, w_out, buf):
    t = h.shape[1]
    b_gate, c_gate, z = jnp.split(h @ w_in, 3, axis=-1)
    u = c_gate * z
    ue = jnp.concatenate([buf.astype(u.dtype), u], axis=1)
    conv = w_conv[0] * ue[:, 0:t]
    for j in range(1, CONV_W):
        conv = conv + w_conv[j] * ue[:, j:j + t]
    y = (b_gate * conv) @ w_out
    return y, ue[:, -(CONV_W - 1):]


def gated_linear_recurrence(q, k, v, logf, s0):
    bsz, t, nh, _ = q.shape
    dv = v.shape[-1]
    c = GLA_CHUNK if t % GLA_CHUNK == 0 else t
    n = t // c

    def chunks(a):
        return a.reshape(bsz, n, c, nh, a.shape[-1]).transpose(1, 0, 3, 2, 4).astype(jnp.float32)

    causal = jnp.tril(jnp.ones((c, c), dtype=bool))[:, :, None]

    def step(s, inp):
        qc, kc, vc, gc = inp
        cum = jnp.cumsum(gc, axis=2)
        o = jnp.einsum('bhtd,bhde->bhte', qc * jnp.exp(cum), s)
        rel = jnp.where(causal, cum[:, :, :, None, :] - cum[:, :, None, :, :], -jnp.inf)
        att = jnp.einsum('bhtd,bhtsd,bhsd->bhts', qc, jnp.exp(rel), kc)
        o = o + jnp.einsum('bhts,bhse->bhte', att, vc)
        last = cum[:, :, -1:, :]
        s = jnp.exp(last[:, :, 0, :, None]) * s + jnp.einsum('bhsd,bhse->bhde', kc * jnp.exp(last - cum), vc)
        return s, o

    s, o = lax.scan(step, s0.astype(jnp.float32), (chunks(q), chunks(k), chunks(v), chunks(logf)))
    o = o.transpose(1, 0, 3, 2, 4).reshape(bsz, t, nh, dv)
    return o, s


def hgrn2_mixer(h, w_in, lb, norm_g, w_out, s0):
    bsz, t, _ = h.shape
    q, f, i_in, g = jnp.split(h @ w_in, 4, axis=-1)
    z = f.astype(jnp.float32)
    lbf = lb.astype(jnp.float32)
    logf = jnp.logaddexp(jnp.log(lbf), jnp.log1p(-lbf) + jax.nn.log_sigmoid(z))
    k = (1.0 - lbf) * jax.nn.sigmoid(-z)
    shp = (bsz, t, HG_HEADS, HG_DK)
    o, s = gated_linear_recurrence(jax.nn.silu(q).reshape(shp), k.reshape(shp),
                                   i_in.reshape(bsz, t, HG_HEADS, HG_DV), logf.reshape(shp), s0)
    o = rms_norm(o.astype(h.dtype), norm_g)
    y = (o.reshape(bsz, t, HG_HEADS * HG_DV) * jax.nn.silu(g)) @ w_out
    return y, s.astype(s0.dtype)


def dilated_group_prompt(q, k, v, dil, nk):
    bsz, t, nh, e = q.shape
    lsub = t // dil
    blk = nk
    nb = -(-lsub // blk)
    pad_end = nb * blk - lsub

    def sub(a, front):
        a = a.reshape(bsz, lsub, dil, nh, e).transpose(0, 2, 1, 3, 4)
        return jnp.pad(a, ((0, 0), (0, 0), (front, pad_end), (0, 0), (0, 0)))

    qs = sub(q, 0).reshape(bsz, dil, nb, blk, nh, e)
    ks = sub(k, blk).reshape(bsz, dil, nb + 1, blk, nh, e)
    vs = sub(v, blk).reshape(bsz, dil, nb + 1, blk, nh, e)
    kw = jnp.concatenate([ks[:, :, :-1], ks[:, :, 1:]], axis=3)
    vw = jnp.concatenate([vs[:, :, :-1], vs[:, :, 1:]], axis=3)
    s = jnp.einsum('brnqhe,brnkhe->brnhqk', qs, kw).astype(jnp.float32) * (e ** -0.5)
    qi = jnp.arange(blk)[:, None]
    ki = jnp.arange(2 * blk)[None, :]
    dist = qi + blk - ki
    band = (dist >= 0) & (dist <= nk)
    real = (jnp.arange(nb)[:, None, None] > 0) | (ki >= blk)[None]
    mask = band[None] & real
    s = jnp.where(mask[:, None], s, -jnp.inf)
    lse = jax.nn.logsumexp(s, axis=-1)
    p = jnp.exp(s - lse[..., None])
    o = jnp.einsum('brnhqk,brnkhe->brnqhe', p, vw.astype(jnp.float32))
    o = o.reshape(bsz, dil, nb * blk, nh, e)[:, :, :lsub].transpose(0, 2, 1, 3, 4).reshape(bsz, t, nh, e)
    lse = lse.transpose(0, 1, 2, 4, 3).reshape(bsz, dil, nb * blk, nh)[:, :, :lsub]
    lse = lse.transpose(0, 2, 1, 3).reshape(bsz, t, nh)
    return o, lse


def dilated_group_step(q, k, v, k_buf, v_buf, dil, nk):
    t = q.shape[1]
    e = q.shape[-1]
    rows = k_buf.shape[1]
    k_all = jnp.concatenate([k_buf.astype(k.dtype), k], axis=1)
    v_all = jnp.concatenate([v_buf.astype(v.dtype), v], axis=1)
    idx = rows + jnp.arange(t)[:, None] - dil * jnp.arange(nk + 1)[None, :]
    valid = idx >= 0
    idx = jnp.maximum(idx, 0)
    kg = k_all[:, idx]
    vg = v_all[:, idx]
    s = jnp.einsum('bthe,btkhe->bthk', q, kg).astype(jnp.float32) * (e ** -0.5)
    s = jnp.where(valid[:, None, :], s, -jnp.inf)
    lse = jax.nn.logsumexp(s, axis=-1)
    p = jnp.exp(s - lse[..., None])
    o = jnp.einsum('bthk,btkhe->bthe', p, vg.astype(jnp.float32))
    return o, lse


def dilated_attention(h, w_qkv, w_out, pos0, bufs):
    bsz, t, _ = h.shape
    pos = pos0 + jnp.arange(t)
    qkv = (h @ w_qkv).reshape(bsz, t, N_GROUPS, 3, ATT_HEADS, ATT_DH)
    outs, lses, rows = [], [], []
    for g, (win, dil) in enumerate(DILATED_GROUPS):
        nk = win // dil
        q = partial_rotary(qkv[:, :, g, 0], pos)
        k = partial_rotary(qkv[:, :, g, 1], pos)
        v = qkv[:, :, g, 2]
        if bufs is None:
            o, lse = dilated_group_prompt(q, k, v, dil, nk)
            keep = min(win, t)
            rows += [k[:, t - keep:], v[:, t - keep:]]
        else:
            o, lse = dilated_group_step(q, k, v, bufs[g][0], bufs[g][1], dil, nk)
            rows += [k, v]
        outs.append(o)
        lses.append(lse)
    wgt = jax.nn.softmax(jnp.stack(lses), axis=0)
    o = jnp.einsum('gbth,gbthe->bthe', wgt, jnp.stack(outs))
    y = o.reshape(bsz, t, ATT_HEADS * ATT_DH).astype(h.dtype) @ w_out
    return y, tuple(rows)


def memory_kv(mem, g_mem, w_kv):
    bsz, m, _ = mem.shape
    k, v = jnp.split(rms_norm(mem, g_mem) @ w_kv, 2, axis=-1)
    return k.reshape(bsz, m, XA_HEADS, XA_DH), v.reshape(bsz, m, XA_HEADS, XA_DH)


def cross_attention(h, mem_k, mem_v, w_q, w_o):
    bsz, t, _ = h.shape
    q = (h @ w_q).reshape(bsz, t, XA_HEADS, XA_DH)
    s = jnp.einsum('bthe,bmhe->bhtm', q, mem_k.astype(q.dtype)).astype(jnp.float32) * (XA_DH ** -0.5)
    p = jax.nn.softmax(s, axis=-1)
    o = jnp.einsum('bhtm,bmhe->bthe', p, mem_v.astype(jnp.float32))
    return o.reshape(bsz, t, XA_HEADS * XA_DH).astype(h.dtype) @ w_o


def setup_inputs(seed: int = 0) -> dict:
    keys = iter(jax.random.split(jax.random.key(seed), 64))

    def normal(shape, scale=1.0):
        return jax.random.normal(next(keys), shape, jnp.float32) * scale

    def gain(shape):
        return 1.0 + 0.02 * normal(shape)

    d = D_MODEL
    inp = {}
    inp['x_prompt'] = normal((BATCH, SEQ, d))
    inp['x_sample'] = normal((DEC_BATCH, DEC_SEQ, d))
    inp['state_conv'] = normal((N_CONV_LAYERS, DEC_BATCH, CONV_W - 1, CONV_DIM))
    inp['state_hgrn'] = normal((N_HGRN_LAYERS, DEC_BATCH, HG_HEADS, HG_DK, HG_DV), 0.5)
    for g, (win, _) in enumerate(DILATED_GROUPS):
        rows = min(win, PAST_LEN)
        inp['cache_win_k%d' % g] = normal((N_ATTN_LAYERS, DEC_BATCH, rows, ATT_HEADS, ATT_DH))
        inp['cache_win_v%d' % g] = normal((N_ATTN_LAYERS, DEC_BATCH, rows, ATT_HEADS, ATT_DH))
    inp['cache_mem_k'] = normal((DEPTH, DEC_BATCH, MEM_LEN, XA_HEADS, XA_DH))
    inp['cache_mem_v'] = normal((DEPTH, DEC_BATCH, MEM_LEN, XA_HEADS, XA_DH))
    inp['mem_prompt'] = normal((BATCH, MEM_LEN, d))
    inp['norm_ffn1'] = gain((DEPTH, d))
    inp['ffn1_w_gu'] = normal((DEPTH, d, 2 * D_FF), d ** -0.5)
    inp['ffn1_w_down'] = normal((DEPTH, D_FF, d), D_FF ** -0.5)
    inp['norm_mix'] = gain((DEPTH, d))
    inp['conv_w_in'] = normal((N_CONV_LAYERS, d, 3 * CONV_DIM), d ** -0.5)
    inp['conv_w'] = normal((N_CONV_LAYERS, CONV_W, CONV_DIM), CONV_W ** -0.5)
    inp['conv_w_out'] = normal((N_CONV_LAYERS, CONV_DIM, d), CONV_DIM ** -0.5)
    inp['hgrn_w_in'] = normal((N_HGRN_LAYERS, d, 2 * HG_HEADS * HG_DK + 2 * HG_HEADS * HG_DV), d ** -0.5)
    inp['hgrn_lb_logits'] = normal((DEPTH, HG_HEADS * HG_DK), 0.5)
    inp['hgrn_norm'] = gain((N_HGRN_LAYERS, HG_HEADS, HG_DV))
    inp['hgrn_w_out'] = normal((N_HGRN_LAYERS, HG_HEADS * HG_DV, d), (HG_HEADS * HG_DV) ** -0.5)
    inp['attn_w_qkv'] = normal((N_ATTN_LAYERS, d, N_GROUPS * 3 * ATT_HEADS * ATT_DH), d ** -0.5)
    inp['attn_w_out'] = normal((N_ATTN_LAYERS, ATT_HEADS * ATT_DH, d), (ATT_HEADS * ATT_DH) ** -0.5)
    inp['norm_mem'] = gain((DEPTH, d))
    inp['xattn_w_kv'] = normal((DEPTH, d, 2 * XA_HEADS * XA_DH), d ** -0.5)
    inp['norm_xattn'] = gain((DEPTH, d))
    inp['xattn_w_q'] = normal((DEPTH, d, XA_HEADS * XA_DH), d ** -0.5)
    inp['xattn_w_o'] = normal((DEPTH, XA_HEADS * XA_DH, d), (XA_HEADS * XA_DH) ** -0.5)
    inp['norm_ffn2'] = gain((DEPTH, d))
    inp['ffn2_w_gu'] = normal((DEPTH, d, 2 * D_FF), d ** -0.5)
    inp['ffn2_w_down'] = normal((DEPTH, D_FF, d), D_FF ** -0.5)
    inp['norm_final'] = gain((d,))
    return inp


def reference(x_prompt, x_sample, state_conv, state_hgrn,
              cache_win_k0, cache_win_v0, cache_win_k1, cache_win_v1, cache_win_k2, cache_win_v2,
              cache_mem_k, cache_mem_v, mem_prompt,
              norm_ffn1, ffn1_w_gu, ffn1_w_down, norm_mix,
              conv_w_in, conv_w, conv_w_out,
              hgrn_w_in, hgrn_lb_logits, hgrn_norm, hgrn_w_out,
              attn_w_qkv, attn_w_out,
              norm_mem, xattn_w_kv, norm_xattn, xattn_w_q, xattn_w_o,
              norm_ffn2, ffn2_w_gu, ffn2_w_down, norm_final):
    lb_p = jax.nn.softmax(hgrn_lb_logits.astype(jnp.float32), axis=0)
    lower_bounds = jnp.cumsum(lb_p, axis=0) - lb_p[0]
    win_cache = ((cache_win_k0, cache_win_v0), (cache_win_k1, cache_win_v1), (cache_win_k2, cache_win_v2))

    def run(x, pos0, conv_st, hgrn_st, win_st, mem, mem_k_cache, mem_v_cache):
        nbat = x.shape[0]
        new_conv, new_hgrn, new_win, new_mk, new_mv = [], [], [], [], []
        for i in range(DEPTH):
            j = i // N_MIXERS
            kind = i % N_MIXERS
            x = x + 0.5 * swiglu(rms_norm(x, norm_ffn1[i]), ffn1_w_gu[i], ffn1_w_down[i])
            h = rms_norm(x, norm_mix[i])
            if kind == 0:
                buf = jnp.zeros((nbat, CONV_W - 1, CONV_DIM), x.dtype) if conv_st is None else conv_st[j]
                y, st = short_conv_mixer(h, conv_w_in[j], conv_w[j], conv_w_out[j], buf)
                new_conv.append(st)
            elif kind == 1:
                s0 = jnp.zeros((nbat, HG_HEADS, HG_DK, HG_DV), x.dtype) if hgrn_st is None else hgrn_st[j]
                y, st = hgrn2_mixer(h, hgrn_w_in[j], lower_bounds[i], hgrn_norm[j], hgrn_w_out[j], s0)
                new_hgrn.append(st)
            else:
                bufs = None if win_st is None else tuple((kc[j], vc[j]) for kc, vc in win_st)
                y, rows = dilated_attention(h, attn_w_qkv[j], attn_w_out[j], pos0, bufs)
                new_win.append(rows)
            x = x + y
            if mem is not None:
                mk, mv = memory_kv(mem, norm_mem[i], xattn_w_kv[i])
                new_mk.append(mk)
                new_mv.append(mv)
            else:
                mk, mv = mem_k_cache[i], mem_v_cache[i]
            x = x + cross_attention(rms_norm(x, norm_xattn[i]), mk, mv, xattn_w_q[i], xattn_w_o[i])
            x = x + 0.5 * swiglu(rms_norm(x, norm_ffn2[i]), ffn2_w_gu[i], ffn2_w_down[i])
        return rms_norm(x, norm_final), new_conv, new_hgrn, new_win, new_mk, new_mv

    y_prompt, pc, ph, pw, pmk, pmv = run(x_prompt, 0, None, None, None, mem_prompt, None, None)
    y_sample, sc, sh, sw, _, _ = run(x_sample, PAST_LEN, state_conv, state_hgrn, win_cache,
                                     None, cache_mem_k, cache_mem_v)

    p_state_conv = jnp.stack(pc)
    p_state_hgrn = jnp.stack(ph)
    p_win = [jnp.stack([r[g] for r in pw]) for g in range(2 * N_GROUPS)]
    p_win_k0, p_win_v0, p_win_k1, p_win_v1, p_win_k2, p_win_v2 = p_win
    p_mem_k = jnp.stack(pmk)
    p_mem_v = jnp.stack(pmv)
    s_state_conv = jnp.stack(sc)
    s_state_hgrn = jnp.stack(sh)
    s_win = [jnp.stack([r[g] for r in sw]) for g in range(2 * N_GROUPS)]
    s_win_k0, s_win_v0, s_win_k1, s_win_v1, s_win_k2, s_win_v2 = s_win
    return (y_prompt, y_sample,
            p_state_conv, p_state_hgrn, p_win_k0, p_win_v0, p_win_k1, p_win_v1, p_win_k2, p_win_v2,
            p_mem_k, p_mem_v,
            s_state_conv, s_state_hgrn, s_win_k0, s_win_v0, s_win_k1, s_win_v1, s_win_k2, s_win_v2)
```

```python
import functools

import jax
import jax.numpy as jnp
from jax import lax
from jax.experimental import pallas as pl
from jax.experimental.pallas import tpu as pltpu

F32 = jnp.float32
BF16 = jnp.bfloat16
EPS = 1e-6
LANES = 128
SUBLANES = 8
VMEM_LIMIT_BYTES = 56 << 20
NEG_BIG = -1e30

N_MIXERS = 3
CONV_W = 3
HG_DK = 128
ATT_DH = 128
DILATED_GROUPS = ((128, 1), (512, 4), (2048, 16))
N_GROUPS = 3
ROPE_DIM = ATT_DH // 4
ROPE_THETA = 500000.0
XA_DH = 128
GLA_BLOCK = 16
PAST_LEN = 16384

NT_DIMS = (((1,), (1,)), ((), ()))
TN_DIMS = (((0,), (0,)), ((), ()))


def _cparams(*semantics):
    return pltpu.CompilerParams(dimension_semantics=semantics, vmem_limit_bytes=VMEM_LIMIT_BYTES)


def _sigmoid(x):
    return 1.0 / (1.0 + jnp.exp(-x))


def _norm_mm_body(x_ref, g_ref, *refs, n_w, n_extra, epilogue, row_chunk):
    w_refs = refs[:n_w]
    extra = refs[n_w:n_w + n_extra]
    outs = refs[n_w + n_extra:-1]
    xn_ref = refs[-1]

    @pl.when(pl.program_id(1) == 0)
    def _():
        def body(c, carry):
            rows = pl.ds(pl.multiple_of(c * row_chunk, row_chunk), row_chunk)
            x = x_ref[rows, :]
            ms = jnp.mean(x * x, axis=-1, keepdims=True)
            xn_ref[rows, :] = (x * lax.rsqrt(ms + EPS) * g_ref[...]).astype(BF16)
            return carry
        lax.fori_loop(0, x_ref.shape[0] // row_chunk, body, 0)

    xn = xn_ref[...]
    accs = [jnp.dot(xn, w[...], preferred_element_type=F32) for w in w_refs]
    epilogue(accs, extra, outs)


def _ep_plain(accs, extra, outs):
    outs[0][...] = accs[0]


def _ep_swiglu(accs, extra, outs):
    gate, up = accs
    outs[0][...] = (gate * _sigmoid(gate) * up).astype(BF16)


def _ep_conv_in(accs, extra, outs):
    b_gate, c_gate, z = accs
    outs[0][...] = b_gate
    outs[1][...] = c_gate * z


def _ep_rotary(accs, extra, outs):
    cos, sin_lo, sin_hi = (r[0] for r in extra)
    acc = accs[0]
    half = ROPE_DIM // 2
    for c in range(acc.shape[1] // LANES):
        xc = acc[:, c * LANES:(c + 1) * LANES]
        outs[0][:, c * LANES:(c + 1) * LANES] = (
            xc * cos + pltpu.roll(xc, LANES - half, 1) * sin_lo + pltpu.roll(xc, half, 1) * sin_hi)


def _norm_mm(x, gain, w, *, w_col_blocks, n_col_blocks, epilogue, out_dtypes, tm, tn,
             extra=(), extra_specs=()):
    m, k = x.shape
    assert m % tm == 0 and w.shape[0] == k
    n_out = n_col_blocks * tn
    in_specs = [pl.BlockSpec((tm, k), lambda i, j: (i, 0)),
                pl.BlockSpec((1, k), lambda i, j: (0, 0))]
    for off in w_col_blocks:
        in_specs.append(pl.BlockSpec((k, tn), lambda i, j, off=off: (0, j + off)))
    in_specs += list(extra_specs)
    out_shape = [jax.ShapeDtypeStruct((m, n_out), dt) for dt in out_dtypes]
    out_specs = [pl.BlockSpec((tm, tn), lambda i, j: (i, j)) for _ in out_dtypes]
    body = functools.partial(_norm_mm_body, n_w=len(w_col_blocks), n_extra=len(extra),
                             epilogue=epilogue, row_chunk=min(tm, 128))
    res = pl.pallas_call(
        body,
        grid=(m // tm, n_col_blocks),
        in_specs=in_specs,
        out_specs=out_specs,
        out_shape=out_shape,
        scratch_shapes=[pltpu.VMEM((tm, k), BF16)],
        compiler_params=_cparams("parallel", "arbitrary"),
    )(x, gain.reshape(1, k), *([w] * len(w_col_blocks)), *extra)
    return res


def _res_mm_body(a_ref, w_ref, r_ref, o_ref, *, scale):
    acc = jnp.dot(a_ref[...], w_ref[...], preferred_element_type=F32)
    o_ref[...] = r_ref[...] + scale * acc


def _res_mm(a, w, res, scale, *, tm, tn):
    m, k = a.shape
    n = w.shape[1]
    assert m % tm == 0 and n % tn == 0 and res.shape == (m, n)
    return pl.pallas_call(
        functools.partial(_res_mm_body, scale=scale),
        grid=(m // tm, n // tn),
        in_specs=[pl.BlockSpec((tm, k), lambda i, j: (i, 0)),
                  pl.BlockSpec((k, tn), lambda i, j: (0, j)),
                  pl.BlockSpec((tm, tn), lambda i, j: (i, j))],
        out_specs=pl.BlockSpec((tm, tn), lambda i, j: (i, j)),
        out_shape=jax.ShapeDtypeStruct((m, n), F32),
        compiler_params=_cparams("parallel", "arbitrary"),
    )(a, w, res)


def _rms_body(x_ref, g_ref, o_ref):
    x = x_ref[...]
    ms = jnp.mean(x * x, axis=-1, keepdims=True)
    o_ref[...] = x * lax.rsqrt(ms + EPS) * g_ref[...]


def _rms_norm(x, gain, *, tm):
    m, k = x.shape
    return pl.pallas_call(
        _rms_body,
        grid=(m // tm,),
        in_specs=[pl.BlockSpec((tm, k), lambda i: (i, 0)), pl.BlockSpec((1, k), lambda i: (0, 0))],
        out_specs=pl.BlockSpec((tm, k), lambda i: (i, 0)),
        out_shape=jax.ShapeDtypeStruct((m, k), F32),
        compiler_params=_cparams("parallel"),
    )(x, gain.reshape(1, k))


def _conv_seq_body(b_ref, u_ref, up_ref, w_ref, o_ref, ue_ref, *, tiles_per_seq):
    tt = u_ref.shape[0]
    first = (pl.program_id(0) % tiles_per_seq) == 0
    ue_ref[0:SUBLANES, :] = jnp.where(first, 0.0, up_ref[...])
    ue_ref[SUBLANES:SUBLANES + tt, :] = u_ref[...]
    w = w_ref[...]
    conv = (w[0:1] * ue_ref[pl.ds(SUBLANES - 2, tt), :] + w[1:2] * ue_ref[pl.ds(SUBLANES - 1, tt), :]
            + w[2:3] * u_ref[...])
    o_ref[...] = (b_ref[...] * conv).astype(BF16)


def _conv_seq(b_gate, u, w_conv, *, seq_len, tt, tc):
    m, c = u.shape
    rb = tt // SUBLANES
    return pl.pallas_call(
        functools.partial(_conv_seq_body, tiles_per_seq=seq_len // tt),
        grid=(m // tt, c // tc),
        in_specs=[pl.BlockSpec((tt, tc), lambda i, j: (i, j)),
                  pl.BlockSpec((tt, tc), lambda i, j: (i, j)),
                  pl.BlockSpec((SUBLANES, tc), lambda i, j: (jnp.maximum(i * rb - 1, 0), j)),
                  pl.BlockSpec((CONV_W, tc), lambda i, j: (0, j))],
        out_specs=pl.BlockSpec((tt, tc), lambda i, j: (i, j)),
        out_shape=jax.ShapeDtypeStruct((m, c), BF16),
        scratch_shapes=[pltpu.VMEM((tt + SUBLANES, tc), F32)],
        compiler_params=_cparams("parallel", "arbitrary"),
    )(b_gate, u, u, w_conv)


def _conv_step_body(b_ref, u_ref, s0_ref, s1_ref, w_ref, o_ref):
    w = w_ref[...]
    conv = w[0:1] * s0_ref[...] + w[1:2] * s1_ref[...] + w[2:3] * u_ref[...]
    o_ref[...] = (b_ref[...] * conv).astype(BF16)


def _conv_step(b_gate, u, s0, s1, w_conv, *, tc):
    m, c = u.shape
    spec = pl.BlockSpec((m, tc), lambda j: (0, j))
    return pl.pallas_call(
        _conv_step_body,
        grid=(c // tc,),
        in_specs=[spec, spec, spec, spec, pl.BlockSpec((CONV_W, tc), lambda j: (0, j))],
        out_specs=spec,
        out_shape=jax.ShapeDtypeStruct((m, c), BF16),
        compiler_params=_cparams("parallel"),
    )(b_gate, u, s0, s1, w_conv)


def _hgrn_seq_body(q_ref, z_ref, v_ref, g_ref, lb_ref, gn_ref, tri_ref, y_ref, st_out_ref,
                   st_ref, qs_ref, ks_ref, cum_ref, o_ref):
    @pl.when(pl.program_id(2) == 0)
    def _():
        st_ref[...] = jnp.zeros_like(st_ref)

    tt = q_ref.shape[0]
    lb = lb_ref[...]
    z = z_ref[...]
    forget = lb + (1.0 - lb) * _sigmoid(z)
    ks_ref[...] = (1.0 - lb) * _sigmoid(-z)
    q = q_ref[...]
    qs_ref[...] = q * _sigmoid(q)
    cum_ref[...] = jnp.dot(tri_ref[...], jnp.log(forget), preferred_element_type=F32,
                           precision=lax.Precision.HIGHEST)

    t_idx = lax.broadcasted_iota(jnp.int32, (GLA_BLOCK, 1), 0)
    half = GLA_BLOCK // 2

    def block(j, carry):
        rows = pl.ds(pl.multiple_of(j * GLA_BLOCK, GLA_BLOCK), GLA_BLOCK)
        cb = cum_ref[rows, :]
        qb = qs_ref[rows, :]
        kb = ks_ref[rows, :]
        vb = v_ref[rows, :]
        last = cb[GLA_BLOCK - 1:GLA_BLOCK, :]
        st = st_ref[...]
        o = lax.dot_general((qb * jnp.exp(cb)).astype(BF16), st.astype(BF16), NT_DIMS,
                            preferred_element_type=F32)
        o_lo = jnp.zeros((half, LANES), F32)
        o_hi = jnp.zeros((half, LANES), F32)
        for s in range(GLA_BLOCK):
            cs = cb[s:s + 1, :]
            qk_hi = qb[half:, :] * kb[s:s + 1, :]
            g_hi = jnp.where(t_idx[half:] >= s, jnp.exp(cb[half:, :] - cs), 0.0) * qk_hi
            o_hi = o_hi + jnp.sum(g_hi, axis=-1, keepdims=True) * vb[s:s + 1, :]
            if s < half:
                qk_lo = qb[:half, :] * kb[s:s + 1, :]
                g_lo = jnp.where(t_idx[:half] >= s, jnp.exp(cb[:half, :] - cs), 0.0) * qk_lo
                o_lo = o_lo + jnp.sum(g_lo, axis=-1, keepdims=True) * vb[s:s + 1, :]
        o_ref[rows, :] = o + jnp.concatenate([o_lo, o_hi], axis=0)
        kd = kb * jnp.exp(last - cb)
        upd = lax.dot_general(vb.astype(BF16), kd.astype(BF16), TN_DIMS, preferred_element_type=F32)
        st_ref[...] = st * jnp.exp(last) + upd
        return carry

    lax.fori_loop(0, tt // GLA_BLOCK, block, 0)

    o = o_ref[...]
    ms = jnp.mean(o * o, axis=-1, keepdims=True)
    g = g_ref[...]
    y_ref[...] = (o * lax.rsqrt(ms + EPS) * gn_ref[0] * (g * _sigmoid(g))).astype(BF16)

    @pl.when(pl.program_id(2) == pl.num_programs(2) - 1)
    def _():
        st_out_ref[0, 0] = st_ref[...]


def _hgrn_seq(proj, lower_bound, norm_gain, *, batch, seq_len, tt):
    m, n4 = proj.shape
    d = n4 // 4
    heads = d // HG_DK
    nt = seq_len // tt
    tri = (jnp.arange(tt)[:, None] >= jnp.arange(tt)[None, :]) & (
        jnp.arange(tt)[:, None] // GLA_BLOCK == jnp.arange(tt)[None, :] // GLA_BLOCK)
    tri = tri.astype(F32)

    def col(sec):
        return pl.BlockSpec((tt, HG_DK), lambda b, h, t, sec=sec: (b * nt + t, sec * heads + h))

    y, st = pl.pallas_call(
        _hgrn_seq_body,
        grid=(batch, heads, nt),
        in_specs=[col(0), col(1), col(2), col(3),
                  pl.BlockSpec((1, HG_DK), lambda b, h, t: (0, h)),
                  pl.BlockSpec((1, 1, HG_DK), lambda b, h, t: (h, 0, 0)),
                  pl.BlockSpec((tt, tt), lambda b, h, t: (0, 0))],
        out_specs=[pl.BlockSpec((tt, HG_DK), lambda b, h, t: (b * nt + t, h)),
                   pl.BlockSpec((1, 1, HG_DK, HG_DK), lambda b, h, t: (b, h, 0, 0))],
        out_shape=[jax.ShapeDtypeStruct((m, d), BF16),
                   jax.ShapeDtypeStruct((batch, heads, HG_DK, HG_DK), F32)],
        scratch_shapes=[pltpu.VMEM((HG_DK, HG_DK), F32), pltpu.VMEM((tt, HG_DK), F32),
                        pltpu.VMEM((tt, HG_DK), F32), pltpu.VMEM((tt, HG_DK), F32),
                        pltpu.VMEM((tt, HG_DK), F32)],
        compiler_params=_cparams("parallel", "parallel", "arbitrary"),
    )(proj, proj, proj, proj, lower_bound.reshape(1, d), norm_gain.reshape(heads, 1, HG_DK), tri)
    return y, st


def _hgrn_step_body(q_ref, z_ref, v_ref, g_ref, lb_ref, gn_ref, s_ref, y_ref, s_out_ref):
    heads = s_ref.shape[1]
    for h in range(heads):
        lb = lb_ref[h]
        z = z_ref[0, h]
        forget = lb + (1.0 - lb) * _sigmoid(z)
        k = (1.0 - lb) * _sigmoid(-z)
        q = q_ref[0, h]
        q = q * _sigmoid(q)
        s_new = forget * s_ref[0, h] + k * v_ref[0, h]
        s_out_ref[0, h] = s_new
        o = jnp.sum(q * s_new, axis=0, keepdims=True)
        ms = jnp.mean(o * o, axis=-1, keepdims=True)
        g = g_ref[0, h]
        y_ref[0, h] = o * lax.rsqrt(ms + EPS) * gn_ref[h] * (g * _sigmoid(g))


def _hgrn_step(proj, lower_bound, norm_gain, state):
    bsz, heads, dk, dv = state.shape
    d = heads * dk
    q, z, v, g = (proj[:, s * d:(s + 1) * d] for s in range(4))
    col = lambda a: a.reshape(bsz, heads, dk, 1)
    row = lambda a: a.reshape(bsz, heads, 1, dv)
    col_spec = pl.BlockSpec((1, heads, dk, 1), lambda b: (b, 0, 0, 0))
    row_spec = pl.BlockSpec((1, heads, 1, dv), lambda b: (b, 0, 0, 0))
    st_spec = pl.BlockSpec((1, heads, dk, dv), lambda b: (b, 0, 0, 0))
    y, s_new = pl.pallas_call(
        _hgrn_step_body,
        grid=(bsz,),
        in_specs=[col_spec, col_spec, row_spec, row_spec,
                  pl.BlockSpec((heads, dk, 1), lambda b: (0, 0, 0)),
                  pl.BlockSpec((heads, 1, dv), lambda b: (0, 0, 0)),
                  st_spec],
        out_specs=[row_spec, st_spec],
        out_shape=[jax.ShapeDtypeStruct((bsz, heads, 1, dv), F32),
                   jax.ShapeDtypeStruct(state.shape, F32)],
        compiler_params=_cparams("parallel"),
    )(col(q), col(z), row(v), row(g), lower_bound.reshape(heads, dk, 1),
      norm_gain.reshape(heads, 1, dv), state)
    return y.reshape(bsz, d), s_new


def _band_block(q, k_prev, v_prev, k_cur, v_cur, scale):
    n = q.shape[0]
    qi = lax.broadcasted_iota(jnp.int32, (n, n), 0)
    ki = lax.broadcasted_iota(jnp.int32, (n, n), 1)
    qb = q.astype(BF16)
    s_cur = lax.dot_general(qb, k_cur.astype(BF16), NT_DIMS, preferred_element_type=F32) * scale
    s_cur = jnp.where(ki <= qi, s_cur, NEG_BIG)
    m = jnp.max(s_cur, axis=-1, keepdims=True)
    if k_prev is not None:
        s_prev = lax.dot_general(qb, k_prev.astype(BF16), NT_DIMS, preferred_element_type=F32) * scale
        s_prev = jnp.where(ki >= qi, s_prev, NEG_BIG)
        m = jnp.maximum(m, jnp.max(s_prev, axis=-1, keepdims=True))
    p_cur = jnp.exp(s_cur - m)
    l = jnp.sum(p_cur, axis=-1, keepdims=True)
    acc = jnp.dot(p_cur.astype(BF16), v_cur.astype(BF16), preferred_element_type=F32)
    if k_prev is not None:
        p_prev = jnp.exp(s_prev - m)
        l = l + jnp.sum(p_prev, axis=-1, keepdims=True)
        acc = acc + jnp.dot(p_prev.astype(BF16), v_prev.astype(BF16), preferred_element_type=F32)
    return acc / l, m + jnp.log(l)


def _dil_attn_seq_body(*refs, seq_len):
    qkv = refs[:3 * N_GROUPS]
    y_ref = refs[3 * N_GROUPS]
    og_ref, lse_ref = refs[3 * N_GROUPS + 1:]
    scale = ATT_DH ** -0.5
    for g, (win, dil) in enumerate(DILATED_GROUPS):
        nk = win // dil
        q_ref, k_ref, v_ref = qkv[3 * g:3 * g + 3]
        lsub = seq_len // dil
        for r in range(dil):
            for n in range(lsub // nk):
                def rows(blk):
                    start = r + dil * nk * blk
                    return pl.ds(start, nk, stride=dil) if dil > 1 else pl.ds(start, nk)
                cur = rows(n)
                q = q_ref[0, cur, :]
                if n == 0:
                    o, lse = _band_block(q, None, None, k_ref[0, cur, :], v_ref[0, cur, :], scale)
                else:
                    prev = rows(n - 1)
                    o, lse = _band_block(q, k_ref[0, prev, :], v_ref[0, prev, :],
                                         k_ref[0, cur, :], v_ref[0, cur, :], scale)
                og_ref[g, cur, :] = o
                lse_ref[g, cur, :] = jnp.broadcast_to(lse, (nk, LANES))
    lse = [lse_ref[g] for g in range(N_GROUPS)]
    m = jnp.maximum(jnp.maximum(lse[0], lse[1]), lse[2])
    w = [jnp.exp(x - m) for x in lse]
    tot = w[0] + w[1] + w[2]
    y_ref[0] = ((w[0] * og_ref[0] + w[1] * og_ref[1] + w[2] * og_ref[2]) / tot).astype(BF16)


def _dil_attn_seq(qkv, *, batch, seq_len, heads):
    in_specs = []
    for g in range(N_GROUPS):
        for c in range(3):
            in_specs.append(pl.BlockSpec((1, seq_len, ATT_DH),
                                         lambda b, h, s=(g * 3 + c): (b, 0, s * heads + h)))
    return pl.pallas_call(
        functools.partial(_dil_attn_seq_body, seq_len=seq_len),
        grid=(batch, heads),
        in_specs=in_specs,
        out_specs=pl.BlockSpec((1, seq_len, ATT_DH), lambda b, h: (b, 0, h)),
        out_shape=jax.ShapeDtypeStruct((batch, seq_len, heads * ATT_DH), BF16),
        scratch_shapes=[pltpu.VMEM((N_GROUPS, seq_len, ATT_DH), F32),
                        pltpu.VMEM((N_GROUPS, seq_len, LANES), F32)],
        compiler_params=_cparams("parallel", "parallel"),
    )(*([qkv] * (3 * N_GROUPS)))


def _dil_attn_step_body(*refs, heads):
    new = refs[:3 * N_GROUPS]
    caches = refs[3 * N_GROUPS:5 * N_GROUPS]
    y_ref = refs[5 * N_GROUPS]
    scale = ATT_DH ** -0.5
    for h in range(heads):
        cols = slice(h * ATT_DH, (h + 1) * ATT_DH)
        outs, lses = [], []
        for g in range(N_GROUPS):
            q = new[3 * g][0, 0][:, cols]
            k_new = new[3 * g + 1][0, 0][:, cols]
            v_new = new[3 * g + 2][0, 0][:, cols]
            k_c = caches[2 * g][0][:, cols]
            v_c = caches[2 * g + 1][0][:, cols]
            q8 = jnp.broadcast_to(q, (SUBLANES, ATT_DH)).astype(BF16)
            s_c = lax.dot_general(q8, k_c.astype(BF16), NT_DIMS, preferred_element_type=F32)[0:1] * scale
            s_n = jnp.sum(q.astype(BF16).astype(F32) * k_new.astype(BF16).astype(F32),
                          axis=-1, keepdims=True) * scale
            m = jnp.maximum(jnp.max(s_c, axis=-1, keepdims=True), s_n)
            p_c = jnp.exp(s_c - m)
            p_n = jnp.exp(s_n - m)
            l = jnp.sum(p_c, axis=-1, keepdims=True) + p_n
            p8 = jnp.broadcast_to(p_c, (SUBLANES, p_c.shape[1])).astype(BF16)
            acc = jnp.dot(p8, v_c.astype(BF16), preferred_element_type=F32)[0:1]
            acc = acc + p_n.astype(BF16).astype(F32) * v_new.astype(BF16).astype(F32)
            outs.append(acc / l)
            lses.append(m + jnp.log(l))
        m = jnp.maximum(jnp.maximum(lses[0], lses[1]), lses[2])
        w = [jnp.exp(x - m) for x in lses]
        y_ref[0, :, cols] = (w[0] * outs[0] + w[1] * outs[1] + w[2] * outs[2]) / (w[0] + w[1] + w[2])


def _dil_attn_step(qkv, caches, *, heads):
    bsz = qkv.shape[0]
    hd = heads * ATT_DH
    new = qkv.reshape(bsz, 3 * N_GROUPS, 1, hd)
    args, in_specs = [], []
    for s in range(3 * N_GROUPS):
        args.append(new)
        in_specs.append(pl.BlockSpec((1, 1, 1, hd), lambda b, s=s: (b, s, 0, 0)))
    for g, (win, dil) in enumerate(DILATED_GROUPS):
        nk = win // dil
        for buf in caches[g]:
            assert buf.shape[1] == win
            args.append(buf.reshape(bsz, nk, dil * hd))
            in_specs.append(pl.BlockSpec((1, nk, hd), lambda b: (b, 0, 0)))
    y = pl.pallas_call(
        functools.partial(_dil_attn_step_body, heads=heads),
        grid=(bsz,),
        in_specs=in_specs,
        out_specs=pl.BlockSpec((1, 1, hd), lambda b: (b, 0, 0)),
        out_shape=jax.ShapeDtypeStruct((bsz, 1, hd), F32),
        compiler_params=_cparams("parallel"),
    )(*args)
    return y.reshape(bsz, hd)


def _xattn_body(q_ref, k_ref, v_ref, o_ref, *, heads, pad_rows):
    scale = XA_DH ** -0.5
    for h in range(heads):
        cols = slice(h * XA_DH, (h + 1) * XA_DH)
        q = q_ref[0][:, cols]
        rows = q.shape[0]
        if pad_rows:
            q = jnp.broadcast_to(q, (pad_rows, XA_DH))
        s = lax.dot_general(q.astype(BF16), k_ref[0][:, cols].astype(BF16), NT_DIMS,
                            preferred_element_type=F32) * scale
        m = jnp.max(s, axis=-1, keepdims=True)
        p = jnp.exp(s - m)
        l = jnp.sum(p, axis=-1, keepdims=True)
        o = jnp.dot(p.astype(BF16), v_ref[0][:, cols].astype(BF16), preferred_element_type=F32) / l
        o_ref[0, :, cols] = o[0:rows].astype(o_ref.dtype)


def _xattn(q, mem_k, mem_v, k_block, v_block, *, tq, out_dtype):
    bsz, t, hd = q.shape
    mem = mem_k.shape[1]
    heads = hd // XA_DH
    return pl.pallas_call(
        functools.partial(_xattn_body, heads=heads, pad_rows=SUBLANES if tq < SUBLANES else 0),
        grid=(bsz, t // tq),
        in_specs=[pl.BlockSpec((1, tq, hd), lambda b, i: (b, i, 0)),
                  pl.BlockSpec((1, mem, hd), lambda b, i: (b, 0, k_block)),
                  pl.BlockSpec((1, mem, hd), lambda b, i: (b, 0, v_block))],
        out_specs=pl.BlockSpec((1, tq, hd), lambda b, i: (b, i, 0)),
        out_shape=jax.ShapeDtypeStruct((bsz, t, hd), out_dtype),
        compiler_params=_cparams("parallel", "parallel"),
    )(q, mem_k, mem_v)


TN_COLS = 512


def _rope_tables(pos):
    rows = pos.shape[0]
    half = ROPE_DIM // 2
    inv_freq = ROPE_THETA ** (-jnp.arange(half, dtype=F32) * 2.0 / ROPE_DIM)
    ang = pos.astype(F32)[:, None] * inv_freq[None, :]
    cos, sin = jnp.cos(ang), jnp.sin(ang)
    rest = jnp.zeros((rows, LANES - ROPE_DIM), F32)
    zh = jnp.zeros((rows, half), F32)
    c = jnp.concatenate([cos, cos, rest + 1.0], axis=1)
    s_lo = jnp.concatenate([-sin, zh, rest], axis=1)
    s_hi = jnp.concatenate([zh, sin, rest], axis=1)
    one, zero = jnp.ones((rows, LANES), F32), jnp.zeros((rows, LANES), F32)
    return jnp.stack([c, one]), jnp.stack([s_lo, zero]), jnp.stack([s_hi, zero])


def _pad_rows(a, rows):
    return jnp.pad(a, ((0, rows - a.shape[0]), (0, 0)))


def _ffn(x, gain, w_gu, w_down, tm):
    nf = w_down.shape[0] // TN_COLS
    (hid,) = _norm_mm(x, gain, w_gu, w_col_blocks=(0, nf), n_col_blocks=nf, epilogue=_ep_swiglu,
                      out_dtypes=(BF16,), tm=tm, tn=TN_COLS)
    return _res_mm(hid, w_down, x, 0.5, tm=tm, tn=TN_COLS)


def _conv_mixer(x, gain, w_in, w_conv, w_out, state, *, nbat, t, tm):
    rows, d = x.shape
    nb = d // TN_COLS
    b_gate, u = _norm_mm(x, gain, w_in, w_col_blocks=(0, nb, 2 * nb), n_col_blocks=nb,
                         epilogue=_ep_conv_in, out_dtypes=(F32, F32), tm=tm, tn=TN_COLS)
    if state is None:
        gated = _conv_seq(b_gate, u, w_conv, seq_len=t, tt=tm, tc=TN_COLS)
        new_state = u.reshape(nbat, t, d)[:, t - (CONV_W - 1):]
    else:
        s0, s1 = state[:, 0], state[:, 1]
        gated = _conv_step(b_gate, u, _pad_rows(s0, rows), _pad_rows(s1, rows), w_conv, tc=TN_COLS)
        new_state = jnp.stack([s1, u[:nbat]], axis=1)
    return _res_mm(gated, w_out, x, 1.0, tm=tm, tn=TN_COLS), new_state


def _hgrn_mixer(x, gain, w_in, lower_bound, norm_gain, w_out, state, *, nbat, t, tm):
    rows, d = x.shape
    (proj,) = _norm_mm(x, gain, w_in, w_col_blocks=(0,), n_col_blocks=w_in.shape[1] // TN_COLS,
                       epilogue=_ep_plain, out_dtypes=(F32,), tm=tm, tn=TN_COLS)
    if state is None:
        y, st = _hgrn_seq(proj, lower_bound, norm_gain, batch=nbat, seq_len=t, tt=min(t, 256))
        new_state = jnp.swapaxes(st, -1, -2)
    else:
        y, new_state = _hgrn_step(proj[:nbat], lower_bound, norm_gain, state)
        y = _pad_rows(y, rows).astype(BF16)
    return _res_mm(y, w_out, x, 1.0, tm=tm, tn=TN_COLS), new_state


def _attn_mixer(x, gain, w_qkv, w_out, pos, caches, *, nbat, t, tm):
    rows, d = x.shape
    heads = d // ATT_DH
    n_sec = 3 * N_GROUPS
    sec_blocks = d // TN_COLS
    tiles = pos.shape[0] // tm
    tab_spec = pl.BlockSpec((1, tm, LANES), lambda r, c: (((c // sec_blocks) % 3) // 2, r % tiles, 0))
    (qkv,) = _norm_mm(x, gain, w_qkv, w_col_blocks=(0,), n_col_blocks=n_sec * sec_blocks,
                      epilogue=_ep_rotary, out_dtypes=(F32,), tm=tm, tn=TN_COLS,
                      extra=_rope_tables(pos), extra_specs=(tab_spec,) * 3)
    q5 = qkv[:nbat * t].reshape(nbat, t, N_GROUPS, 3, heads, ATT_DH)
    win_rows = []
    if caches is None:
        y = _dil_attn_seq(qkv.reshape(nbat, t, n_sec * d), batch=nbat, seq_len=t, heads=heads)
        y = y.reshape(rows, d)
        for g, (win, _) in enumerate(DILATED_GROUPS):
            keep = min(win, t)
            win_rows += [q5[:, t - keep:, g, 1], q5[:, t - keep:, g, 2]]
    else:
        y = _dil_attn_step(qkv[:nbat], caches, heads=heads)
        y = _pad_rows(y, rows).astype(BF16)
        for g in range(N_GROUPS):
            win_rows += [q5[:, :, g, 1], q5[:, :, g, 2]]
    return _res_mm(y, w_out, x, 1.0, tm=tm, tn=TN_COLS), win_rows


def _memory_kv(mem, gain, w_kv, *, tm):
    (kv,) = _norm_mm(mem, gain, w_kv, w_col_blocks=(0,), n_col_blocks=w_kv.shape[1] // TN_COLS,
                     epilogue=_ep_plain, out_dtypes=(F32,), tm=min(tm, mem.shape[0]), tn=TN_COLS)
    return kv


def _cross_attention(x, gain, w_q, w_o, mem_k, mem_v, k_block, v_block, *, nbat, t, tm):
    rows = x.shape[0]
    hd = w_q.shape[1]
    (q,) = _norm_mm(x, gain, w_q, w_col_blocks=(0,), n_col_blocks=hd // TN_COLS,
                    epilogue=_ep_plain, out_dtypes=(F32,), tm=tm, tn=TN_COLS)
    if t > 1:
        o = _xattn(q.reshape(nbat, t, hd), mem_k, mem_v, k_block, v_block, tq=tm, out_dtype=BF16)
        o = o.reshape(rows, hd)
    else:
        o = _xattn(q[:nbat].reshape(nbat, 1, hd), mem_k, mem_v, k_block, v_block, tq=1, out_dtype=F32)
        o = _pad_rows(o.reshape(nbat, hd), rows).astype(BF16)
    return _res_mm(o, w_o, x, 1.0, tm=tm, tn=TN_COLS)


def kernel(x_prompt, x_sample, state_conv, state_hgrn,
           cache_win_k0, cache_win_v0, cache_win_k1, cache_win_v1, cache_win_k2, cache_win_v2,
           cache_mem_k, cache_mem_v, mem_prompt,
           norm_ffn1, ffn1_w_gu, ffn1_w_down, norm_mix,
           conv_w_in, conv_w, conv_w_out,
           hgrn_w_in, hgrn_lb_logits, hgrn_norm, hgrn_w_out,
           attn_w_qkv, attn_w_out,
           norm_mem, xattn_w_kv, norm_xattn, xattn_w_q, xattn_w_o,
           norm_ffn2, ffn2_w_gu, ffn2_w_down, norm_final):
    batch, seq, d = x_prompt.shape
    dec_batch, dec_seq, _ = x_sample.shape
    assert dec_seq == 1
    depth = norm_ffn1.shape[0]
    mem_len = mem_prompt.shape[1]
    xa_hd = xattn_w_q.shape[2]
    xa_heads = xa_hd // XA_DH
    win_cache = ((cache_win_k0, cache_win_v0), (cache_win_k1, cache_win_v1), (cache_win_k2, cache_win_v2))

    lb_p = jax.nn.softmax(hgrn_lb_logits.astype(F32), axis=0)
    lower_bounds = jnp.cumsum(lb_p, axis=0) - lb_p[0]

    bf = lambda w: w.astype(BF16)
    w_gu1, w_dn1, w_gu2, w_dn2 = bf(ffn1_w_gu), bf(ffn1_w_down), bf(ffn2_w_gu), bf(ffn2_w_down)
    w_cin, w_cout = bf(conv_w_in), bf(conv_w_out)
    w_hin, w_hout = bf(hgrn_w_in), bf(hgrn_w_out)
    w_qkv, w_ao = bf(attn_w_qkv), bf(attn_w_out)
    w_xkv, w_xq, w_xo = bf(xattn_w_kv), bf(xattn_w_q), bf(xattn_w_o)

    def run(x, pos, nbat, t, tm, conv_st, hgrn_st, win_st, mem, mem_k_cache, mem_v_cache):
        new_conv, new_hgrn, new_win, new_mk, new_mv = [], [], [], [], []
        for i in range(depth):
            j, kind = divmod(i, N_MIXERS)
            x = _ffn(x, norm_ffn1[i], w_gu1[i], w_dn1[i], tm)
            if kind == 0:
                x, st = _conv_mixer(x, norm_mix[i], w_cin[j], conv_w[j], w_cout[j],
                                    None if conv_st is None else conv_st[j], nbat=nbat, t=t, tm=tm)
                new_conv.append(st)
            elif kind == 1:
                x, st = _hgrn_mixer(x, norm_mix[i], w_hin[j], lower_bounds[i], hgrn_norm[j], w_hout[j],
                                    None if hgrn_st is None else hgrn_st[j], nbat=nbat, t=t, tm=tm)
                new_hgrn.append(st)
            else:
                caches = None if win_st is None else [(kc[j], vc[j]) for kc, vc in win_st]
                x, win_rows = _attn_mixer(x, norm_mix[i], w_qkv[j], w_ao[j], pos, caches, nbat=nbat, t=t, tm=tm)
                new_win.append(win_rows)
            if mem is not None:
                kv = _memory_kv(mem, norm_mem[i], w_xkv[i], tm=tm).reshape(nbat, mem_len, 2 * xa_hd)
                new_mk.append(kv[:, :, :xa_hd].reshape(nbat, mem_len, xa_heads, XA_DH))
                new_mv.append(kv[:, :, xa_hd:].reshape(nbat, mem_len, xa_heads, XA_DH))
                mk, mv, vb = kv, kv, 1
            else:
                mk = mem_k_cache[i].reshape(nbat, mem_len, xa_hd)
                mv = mem_v_cache[i].reshape(nbat, mem_len, xa_hd)
                vb = 0
            x = _cross_attention(x, norm_xattn[i], w_xq[i], w_xo[i], mk, mv, 0, vb, nbat=nbat, t=t, tm=tm)
            x = _ffn(x, norm_ffn2[i], w_gu2[i], w_dn2[i], tm)
        return _rms_norm(x, norm_final, tm=tm), new_conv, new_hgrn, new_win, new_mk, new_mv

    y_p, pc, ph, pw, pmk, pmv = run(x_prompt.reshape(batch * seq, d), jnp.arange(seq), batch, seq, 512,
                                    None, None, None, mem_prompt.reshape(batch * mem_len, d), None, None)
    y_prompt = y_p.reshape(batch, seq, d)

    rows_s = 16
    xs = _pad_rows(x_sample.reshape(dec_batch, d), rows_s)
    y_s, sc, sh, sw, _, _ = run(xs, jnp.full((rows_s,), PAST_LEN, jnp.int32), dec_batch, 1, rows_s,
                                state_conv, state_hgrn, win_cache, None, cache_mem_k, cache_mem_v)
    y_sample = y_s[:dec_batch].reshape(dec_batch, 1, d)

    p_win = [jnp.stack([r[g] for r in pw]) for g in range(2 * N_GROUPS)]
    s_win = [jnp.stack([r[g] for r in sw]) for g in range(2 * N_GROUPS)]
    return (y_prompt, y_sample,
            jnp.stack(pc), jnp.stack(ph), *p_win, jnp.stack(pmk), jnp.stack(pmv),
            jnp.stack(sc), jnp.stack(sh), *s_win)
```

```python
import functools

import jax
import jax.numpy as jnp
from jax import lax
from jax.experimental import pallas as pl
from jax.experimental.pallas import tpu as pltpu

F32 = jnp.float32
BF16 = jnp.bfloat16
EPS = 1e-6
LANES = 128
SUBLANES = 8
VMEM_LIMIT_BYTES = 56 << 20
NEG_BIG = -1e30

N_MIXERS = 3
CONV_W = 3
HG_DK = 128
ATT_DH = 128
DILATED_GROUPS = ((128, 1), (512, 4), (2048, 16))
N_GROUPS = 3
ROPE_DIM = ATT_DH // 4
ROPE_THETA = 500000.0
XA_DH = 128
GLA_BLOCK = 16
PAST_LEN = 16384

NT_DIMS = (((1,), (1,)), ((), ()))
TN_DIMS = (((0,), (0,)), ((), ()))


def _cparams(*semantics):
    return pltpu.CompilerParams(dimension_semantics=semantics, vmem_limit_bytes=VMEM_LIMIT_BYTES)


def _sigmoid(x):
    return 1.0 / (1.0 + jnp.exp(-x))


def _rms_body(x_ref, g_ref, o_ref):
    x = x_ref[...]
    ms = jnp.mean(x * x, axis=-1, keepdims=True)
    o_ref[...] = (x * lax.rsqrt(ms + EPS) * g_ref[...]).astype(o_ref.dtype)


def _rms_norm(x, gain, *, tm, out_dtype, name):
    m, k = x.shape
    return pl.pallas_call(
        _rms_body,
        grid=(m // tm,),
        in_specs=[pl.BlockSpec((tm, k), lambda i: (i, 0)), pl.BlockSpec((1, k), lambda i: (0, 0))],
        out_specs=pl.BlockSpec((tm, k), lambda i: (i, 0)),
        out_shape=jax.ShapeDtypeStruct((m, k), out_dtype),
        compiler_params=_cparams("parallel"),
        name=name,
    )(x, gain.reshape(1, k))


CAST_ROWS = 256


def _cast_weight(w_ref, wb_ref, slot):
    def body(c, carry):
        rows = pl.ds(pl.multiple_of(c * CAST_ROWS, CAST_ROWS), CAST_ROWS)
        wb_ref[slot, rows, :] = w_ref[0, rows, :].astype(BF16)
        return carry
    lax.fori_loop(0, w_ref.shape[1] // CAST_ROWS, body, 0)


def _ep_plain(accs, extra, outs):
    outs[0][...] = accs[0].astype(outs[0].dtype)


def _ep_swiglu(accs, extra, outs):
    gate, up = accs
    outs[0][...] = (gate * _sigmoid(gate) * up).astype(BF16)


def _ep_conv_in(accs, extra, outs):
    b_gate, c_gate, z = accs
    outs[0][...] = b_gate
    outs[1][...] = c_gate * z


def _ep_rotary(accs, extra, outs):
    cos, sin_lo, sin_hi = (r[0] for r in extra)
    acc = accs[0]
    half = ROPE_DIM // 2
    for c in range(acc.shape[1] // LANES):
        xc = acc[:, c * LANES:(c + 1) * LANES]
        outs[0][:, c * LANES:(c + 1) * LANES] = (
            xc * cos + pltpu.roll(xc, LANES - half, 1) * sin_lo + pltpu.roll(xc, half, 1) * sin_hi)


def _proj_body(*refs, n_w, n_extra, n_out, has_sample, epilogue):
    refs = list(refs)
    take = lambda n: [refs.pop(0) for _ in range(n)]
    (x_ref,) = take(1)
    xs_ref, gs_ref = take(2) if has_sample else (None, None)
    w_refs = take(n_w)
    extra = take(n_extra)
    extra_s = take(n_extra) if has_sample else []
    outs = take(n_out)
    outs_s = take(n_out) if has_sample else []
    (wb_ref,) = refs

    @pl.when(pl.program_id(1) == 0)
    def _():
        for k, w_ref in enumerate(w_refs):
            _cast_weight(w_ref, wb_ref, k)
        if has_sample:
            xs = xs_ref[...]
            ms = jnp.mean(xs * xs, axis=-1, keepdims=True)
            xsn = (xs * lax.rsqrt(ms + EPS) * gs_ref[...]).astype(BF16)
            epilogue([jnp.dot(xsn, wb_ref[k], preferred_element_type=F32) for k in range(n_w)],
                     extra_s, outs_s)

    x = x_ref[...]
    epilogue([jnp.dot(x, wb_ref[k], preferred_element_type=F32) for k in range(n_w)], extra, outs)


def _proj(xn, xs, gain, w_stack, layer, *, w_col_blocks, n_col_blocks, epilogue, out_dtypes, name,
          tm=512, tn=512, extra=(), extra_specs=(), extra_s=(), extra_s_specs=()):
    m, k = xn.shape
    has_sample = xs is not None
    assert m % tm == 0 and w_stack.shape[1] == k and k % CAST_ROWS == 0
    n_w, n_out = len(w_col_blocks), len(out_dtypes)
    n_total = n_col_blocks * tn
    args = [xn]
    in_specs = [pl.BlockSpec((tm, k), lambda j, i: (i, 0))]
    if has_sample:
        rs = xs.shape[0]
        args += [xs, gain.reshape(1, k)]
        in_specs += [pl.BlockSpec((rs, k), lambda j, i: (0, 0)), pl.BlockSpec((1, k), lambda j, i: (0, 0))]
    for off in w_col_blocks:
        args.append(w_stack)
        in_specs.append(pl.BlockSpec((1, k, tn), lambda j, i, off=off: (layer, 0, j + off)))
    args += list(extra)
    in_specs += list(extra_specs)
    out_shape = [jax.ShapeDtypeStruct((m, n_total), dt) for dt in out_dtypes]
    out_specs = [pl.BlockSpec((tm, tn), lambda j, i: (i, j)) for _ in out_dtypes]
    if has_sample:
        args += list(extra_s)
        in_specs += list(extra_s_specs)
        out_shape += [jax.ShapeDtypeStruct((rs, n_total), dt) for dt in out_dtypes]
        out_specs += [pl.BlockSpec((rs, tn), lambda j, i: (0, j)) for _ in out_dtypes]
    res = pl.pallas_call(
        functools.partial(_proj_body, n_w=n_w, n_extra=len(extra), n_out=n_out,
                          has_sample=has_sample, epilogue=epilogue),
        grid=(n_col_blocks, m // tm),
        in_specs=in_specs,
        out_specs=out_specs,
        out_shape=out_shape,
        scratch_shapes=[pltpu.VMEM((n_w, k, tn), BF16)],
        compiler_params=_cparams("arbitrary", "arbitrary"),
        name=name,
    )(*args)
    return (res[:n_out], res[n_out:]) if has_sample else (res, None)


def _res_proj_body(a_ref, as_ref, w_ref, r_ref, rs_ref, o_ref, os_ref, wb_ref, *, scale):
    @pl.when(pl.program_id(1) == 0)
    def _():
        _cast_weight(w_ref, wb_ref, 0)
        os_ref[...] = rs_ref[...] + scale * jnp.dot(as_ref[...], wb_ref[0], preferred_element_type=F32)

    o_ref[...] = r_ref[...] + scale * jnp.dot(a_ref[...], wb_ref[0], preferred_element_type=F32)


def _res_proj(a, a_s, w_stack, layer, res, res_s, scale, *, name, tm=512, tn=512):
    m, k = a.shape
    rs = a_s.shape[0]
    n = w_stack.shape[2]
    assert m % tm == 0 and n % tn == 0 and k % CAST_ROWS == 0 and res.shape == (m, n)
    return pl.pallas_call(
        functools.partial(_res_proj_body, scale=scale),
        grid=(n // tn, m // tm),
        in_specs=[pl.BlockSpec((tm, k), lambda j, i: (i, 0)),
                  pl.BlockSpec((rs, k), lambda j, i: (0, 0)),
                  pl.BlockSpec((1, k, tn), lambda j, i: (layer, 0, j)),
                  pl.BlockSpec((tm, tn), lambda j, i: (i, j)),
                  pl.BlockSpec((rs, tn), lambda j, i: (0, j))],
        out_specs=[pl.BlockSpec((tm, tn), lambda j, i: (i, j)),
                   pl.BlockSpec((rs, tn), lambda j, i: (0, j))],
        out_shape=[jax.ShapeDtypeStruct((m, n), F32), jax.ShapeDtypeStruct((rs, n), F32)],
        scratch_shapes=[pltpu.VMEM((1, k, tn), BF16)],
        compiler_params=_cparams("arbitrary", "arbitrary"),
        name=name,
    )(a, a_s, w_stack, res, res_s)


def _conv_seq_body(b_ref, u_ref, up_ref, w_ref, o_ref, ue_ref, *, tiles_per_seq):
    tt = u_ref.shape[0]
    first = (pl.program_id(0) % tiles_per_seq) == 0
    ue_ref[0:SUBLANES, :] = jnp.where(first, 0.0, up_ref[...])
    ue_ref[SUBLANES:SUBLANES + tt, :] = u_ref[...]
    w = w_ref[...]
    conv = (w[0:1] * ue_ref[pl.ds(SUBLANES - 2, tt), :] + w[1:2] * ue_ref[pl.ds(SUBLANES - 1, tt), :]
            + w[2:3] * u_ref[...])
    o_ref[...] = (b_ref[...] * conv).astype(BF16)


def _conv_seq(b_gate, u, w_conv, *, seq_len, tt, tc):
    m, c = u.shape
    rb = tt // SUBLANES
    return pl.pallas_call(
        functools.partial(_conv_seq_body, tiles_per_seq=seq_len // tt),
        grid=(m // tt, c // tc),
        in_specs=[pl.BlockSpec((tt, tc), lambda i, j: (i, j)),
                  pl.BlockSpec((tt, tc), lambda i, j: (i, j)),
                  pl.BlockSpec((SUBLANES, tc), lambda i, j: (jnp.maximum(i * rb - 1, 0), j)),
                  pl.BlockSpec((CONV_W, tc), lambda i, j: (0, j))],
        out_specs=pl.BlockSpec((tt, tc), lambda i, j: (i, j)),
        out_shape=jax.ShapeDtypeStruct((m, c), BF16),
        scratch_shapes=[pltpu.VMEM((tt + SUBLANES, tc), F32)],
        compiler_params=_cparams("parallel", "arbitrary"),
        name="conv_seq",
    )(b_gate, u, u, w_conv)


def _conv_step_body(b_ref, u_ref, s0_ref, s1_ref, w_ref, o_ref):
    w = w_ref[...]
    conv = w[0:1] * s0_ref[...] + w[1:2] * s1_ref[...] + w[2:3] * u_ref[...]
    o_ref[...] = (b_ref[...] * conv).astype(BF16)


def _conv_step(b_gate, u, s0, s1, w_conv, *, tc):
    m, c = u.shape
    spec = pl.BlockSpec((m, tc), lambda j: (0, j))
    return pl.pallas_call(
        _conv_step_body,
        grid=(c // tc,),
        in_specs=[spec, spec, spec, spec, pl.BlockSpec((CONV_W, tc), lambda j: (0, j))],
        out_specs=spec,
        out_shape=jax.ShapeDtypeStruct((m, c), BF16),
        compiler_params=_cparams("parallel"),
        name="conv_step",
    )(b_gate, u, s0, s1, w_conv)


def _hgrn_seq_body(q_ref, z_ref, v_ref, g_ref, lb_ref, gn_ref, tri_ref, y_ref, st_out_ref,
                   st_ref, qs_ref, ks_ref, cum_ref, o_ref, *, hb):
    @pl.when(pl.program_id(2) == 0)
    def _():
        st_ref[...] = jnp.zeros_like(st_ref)

    tt = q_ref.shape[0]
    lb = lb_ref[...]
    z = z_ref[...]
    forget = lb + (1.0 - lb) * _sigmoid(z)
    ks_ref[...] = (1.0 - lb) * _sigmoid(-z)
    q = q_ref[...]
    qs_ref[...] = q * _sigmoid(q)
    cum_ref[...] = jnp.dot(tri_ref[...], jnp.log(forget), preferred_element_type=F32,
                           precision=lax.Precision.HIGHEST)

    t_idx = lax.broadcasted_iota(jnp.int32, (GLA_BLOCK, 1), 0)
    half = GLA_BLOCK // 2

    def block(j, carry):
        rows = pl.ds(pl.multiple_of(j * GLA_BLOCK, GLA_BLOCK), GLA_BLOCK)
        for h in range(hb):
            cols = slice(h * HG_DK, (h + 1) * HG_DK)
            cb = cum_ref[rows, cols]
            qb = qs_ref[rows, cols]
            kb = ks_ref[rows, cols]
            vb = v_ref[rows, cols]
            last = cb[GLA_BLOCK - 1:GLA_BLOCK, :]
            st = st_ref[h]
            o = lax.dot_general((qb * jnp.exp(cb)).astype(BF16), st.astype(BF16), NT_DIMS,
                                preferred_element_type=F32)
            o_lo = jnp.zeros((half, LANES), F32)
            o_hi = jnp.zeros((half, LANES), F32)
            for s in range(GLA_BLOCK):
                cs = cb[s:s + 1, :]
                qk_hi = qb[half:, :] * kb[s:s + 1, :]
                g_hi = jnp.where(t_idx[half:] >= s, jnp.exp(cb[half:, :] - cs), 0.0) * qk_hi
                o_hi = o_hi + jnp.sum(g_hi, axis=-1, keepdims=True) * vb[s:s + 1, :]
                if s < half:
                    qk_lo = qb[:half, :] * kb[s:s + 1, :]
                    g_lo = jnp.where(t_idx[:half] >= s, jnp.exp(cb[:half, :] - cs), 0.0) * qk_lo
                    o_lo = o_lo + jnp.sum(g_lo, axis=-1, keepdims=True) * vb[s:s + 1, :]
            o_ref[rows, cols] = o + jnp.concatenate([o_lo, o_hi], axis=0)
            kd = kb * jnp.exp(last - cb)
            upd = lax.dot_general(vb.astype(BF16), kd.astype(BF16), TN_DIMS, preferred_element_type=F32)
            st_ref[h] = st * jnp.exp(last) + upd
        return carry

    lax.fori_loop(0, tt // GLA_BLOCK, block, 0)

    for h in range(hb):
        cols = slice(h * HG_DK, (h + 1) * HG_DK)
        o = o_ref[:, cols]
        ms = jnp.mean(o * o, axis=-1, keepdims=True)
        g = g_ref[:, cols]
        y_ref[:, cols] = (o * lax.rsqrt(ms + EPS) * gn_ref[0, h:h + 1, :] * (g * _sigmoid(g))).astype(BF16)

    @pl.when(pl.program_id(2) == pl.num_programs(2) - 1)
    def _():
        st_out_ref[0] = st_ref[...]


def _hgrn_seq(proj, lower_bound, norm_gain, *, batch, seq_len, tt=256, hb=4):
    m, n4 = proj.shape
    d = n4 // 4
    heads = d // HG_DK
    hg = heads // hb
    nt = seq_len // tt
    wcols = hb * HG_DK
    tri = (jnp.arange(tt)[:, None] >= jnp.arange(tt)[None, :]) & (
        jnp.arange(tt)[:, None] // GLA_BLOCK == jnp.arange(tt)[None, :] // GLA_BLOCK)
    tri = tri.astype(F32)

    def col(sec):
        return pl.BlockSpec((tt, wcols), lambda b, h, t, sec=sec: (b * nt + t, sec * hg + h))

    y, st = pl.pallas_call(
        functools.partial(_hgrn_seq_body, hb=hb),
        grid=(batch, hg, nt),
        in_specs=[col(0), col(1), col(2), col(3),
                  pl.BlockSpec((1, wcols), lambda b, h, t: (0, h)),
                  pl.BlockSpec((1, hb, HG_DK), lambda b, h, t: (h, 0, 0)),
                  pl.BlockSpec((tt, tt), lambda b, h, t: (0, 0))],
        out_specs=[pl.BlockSpec((tt, wcols), lambda b, h, t: (b * nt + t, h)),
                   pl.BlockSpec((1, hb, HG_DK, HG_DK), lambda b, h, t: (b, h, 0, 0))],
        out_shape=[jax.ShapeDtypeStruct((m, d), BF16),
                   jax.ShapeDtypeStruct((batch, heads, HG_DK, HG_DK), F32)],
        scratch_shapes=[pltpu.VMEM((hb, HG_DK, HG_DK), F32), pltpu.VMEM((tt, wcols), F32),
                        pltpu.VMEM((tt, wcols), F32), pltpu.VMEM((tt, wcols), F32),
                        pltpu.VMEM((tt, wcols), F32)],
        compiler_params=_cparams("parallel", "parallel", "arbitrary"),
        name="hgrn_seq",
    )(proj, proj, proj, proj, lower_bound.reshape(1, d), norm_gain.reshape(hg, hb, HG_DK), tri)
    return y, st


def _hgrn_step_body(q_ref, z_ref, v_ref, g_ref, lb_ref, gn_ref, s_ref, y_ref, s_out_ref):
    heads = s_ref.shape[1]
    for h in range(heads):
        lb = lb_ref[h]
        z = z_ref[0, h]
        forget = lb + (1.0 - lb) * _sigmoid(z)
        k = (1.0 - lb) * _sigmoid(-z)
        q = q_ref[0, h]
        q = q * _sigmoid(q)
        s_new = forget * s_ref[0, h] + k * v_ref[0, h]
        s_out_ref[0, h] = s_new
        o = jnp.sum(q * s_new, axis=0, keepdims=True)
        ms = jnp.mean(o * o, axis=-1, keepdims=True)
        g = g_ref[0, h]
        y_ref[0, h] = o * lax.rsqrt(ms + EPS) * gn_ref[h] * (g * _sigmoid(g))


def _hgrn_step(proj, lower_bound, norm_gain, state):
    bsz, heads, dk, dv = state.shape
    d = heads * dk
    q, z, v, g = (proj[:, s * d:(s + 1) * d] for s in range(4))
    col = lambda a: a.reshape(bsz, heads, dk, 1)
    row = lambda a: a.reshape(bsz, heads, 1, dv)
    col_spec = pl.BlockSpec((1, heads, dk, 1), lambda b: (b, 0, 0, 0))
    row_spec = pl.BlockSpec((1, heads, 1, dv), lambda b: (b, 0, 0, 0))
    st_spec = pl.BlockSpec((1, heads, dk, dv), lambda b: (b, 0, 0, 0))
    y, s_new = pl.pallas_call(
        _hgrn_step_body,
        grid=(bsz,),
        in_specs=[col_spec, col_spec, row_spec, row_spec,
                  pl.BlockSpec((heads, dk, 1), lambda b: (0, 0, 0)),
                  pl.BlockSpec((heads, 1, dv), lambda b: (0, 0, 0)),
                  st_spec],
        out_specs=[row_spec, st_spec],
        out_shape=[jax.ShapeDtypeStruct((bsz, heads, 1, dv), F32),
                   jax.ShapeDtypeStruct(state.shape, F32)],
        compiler_params=_cparams("parallel"),
        name="hgrn_step",
    )(col(q), col(z), row(v), row(g), lower_bound.reshape(heads, dk, 1),
      norm_gain.reshape(heads, 1, dv), state)
    return y.reshape(bsz, d), s_new


def _band_block(q, k_prev, v_prev, k_cur, v_cur, scale):
    n = q.shape[0]
    qi = lax.broadcasted_iota(jnp.int32, (n, n), 0)
    ki = lax.broadcasted_iota(jnp.int32, (n, n), 1)
    qb = q.astype(BF16)
    s_cur = lax.dot_general(qb, k_cur.astype(BF16), NT_DIMS, preferred_element_type=F32) * scale
    s_cur = jnp.where(ki <= qi, s_cur, NEG_BIG)
    m = jnp.max(s_cur, axis=-1, keepdims=True)
    if k_prev is not None:
        s_prev = lax.dot_general(qb, k_prev.astype(BF16), NT_DIMS, preferred_element_type=F32) * scale
        s_prev = jnp.where(ki >= qi, s_prev, NEG_BIG)
        m = jnp.maximum(m, jnp.max(s_prev, axis=-1, keepdims=True))
    p_cur = jnp.exp(s_cur - m)
    l = jnp.sum(p_cur, axis=-1, keepdims=True)
    acc = jnp.dot(p_cur.astype(BF16), v_cur.astype(BF16), preferred_element_type=F32)
    if k_prev is not None:
        p_prev = jnp.exp(s_prev - m)
        l = l + jnp.sum(p_prev, axis=-1, keepdims=True)
        acc = acc + jnp.dot(p_prev.astype(BF16), v_prev.astype(BF16), preferred_element_type=F32)
    return acc / l, m + jnp.log(l)


def _dil_attn_seq_body(*refs, seq_len):
    qkv = refs[:3 * N_GROUPS]
    y_ref = refs[3 * N_GROUPS]
    og_ref, lse_ref = refs[3 * N_GROUPS + 1:]
    scale = ATT_DH ** -0.5
    for g, (win, dil) in enumerate(DILATED_GROUPS):
        nk = win // dil
        q_ref, k_ref, v_ref = qkv[3 * g:3 * g + 3]
        lsub = seq_len // dil
        for r in range(dil):
            for n in range(lsub // nk):
                def rows(blk):
                    start = r + dil * nk * blk
                    return pl.ds(start, nk, stride=dil) if dil > 1 else pl.ds(start, nk)
                cur = rows(n)
                q = q_ref[0, cur, :]
                if n == 0:
                    o, lse = _band_block(q, None, None, k_ref[0, cur, :], v_ref[0, cur, :], scale)
                else:
                    prev = rows(n - 1)
                    o, lse = _band_block(q, k_ref[0, prev, :], v_ref[0, prev, :],
                                         k_ref[0, cur, :], v_ref[0, cur, :], scale)
                og_ref[g, cur, :] = o
                lse_ref[g, cur, :] = jnp.broadcast_to(lse, (nk, LANES))
    lse = [lse_ref[g] for g in range(N_GROUPS)]
    m = jnp.maximum(jnp.maximum(lse[0], lse[1]), lse[2])
    w = [jnp.exp(x - m) for x in lse]
    tot = w[0] + w[1] + w[2]
    y_ref[0] = ((w[0] * og_ref[0] + w[1] * og_ref[1] + w[2] * og_ref[2]) / tot).astype(BF16)


def _dil_attn_seq(qkv, *, batch, seq_len, heads):
    in_specs = []
    for g in range(N_GROUPS):
        for c in range(3):
            in_specs.append(pl.BlockSpec((1, seq_len, ATT_DH),
                                         lambda b, h, s=(g * 3 + c): (b, 0, s * heads + h)))
    return pl.pallas_call(
        functools.partial(_dil_attn_seq_body, seq_len=seq_len),
        grid=(batch, heads),
        in_specs=in_specs,
        out_specs=pl.BlockSpec((1, seq_len, ATT_DH), lambda b, h: (b, 0, h)),
        out_shape=jax.ShapeDtypeStruct((batch, seq_len, heads * ATT_DH), BF16),
        scratch_shapes=[pltpu.VMEM((N_GROUPS, seq_len, ATT_DH), F32),
                        pltpu.VMEM((N_GROUPS, seq_len, LANES), F32)],
        compiler_params=_cparams("parallel", "parallel"),
        name="dil_attn_seq",
    )(*([qkv] * (3 * N_GROUPS)))


def _dil_attn_step_body(*refs, heads):
    new = refs[:3 * N_GROUPS]
    caches = refs[3 * N_GROUPS:5 * N_GROUPS]
    y_ref = refs[5 * N_GROUPS]
    scale = ATT_DH ** -0.5
    for h in range(heads):
        cols = slice(h * ATT_DH, (h + 1) * ATT_DH)
        outs, lses = [], []
        for g in range(N_GROUPS):
            q = new[3 * g][0, 0][:, cols]
            k_new = new[3 * g + 1][0, 0][:, cols]
            v_new = new[3 * g + 2][0, 0][:, cols]
            k_c = caches[2 * g][0][:, cols]
            v_c = caches[2 * g + 1][0][:, cols]
            q8 = jnp.broadcast_to(q, (SUBLANES, ATT_DH)).astype(BF16)
            s_c = lax.dot_general(q8, k_c.astype(BF16), NT_DIMS, preferred_element_type=F32)[0:1] * scale
            s_n = jnp.sum(q.astype(BF16).astype(F32) * k_new.astype(BF16).astype(F32),
                          axis=-1, keepdims=True) * scale
            m = jnp.maximum(jnp.max(s_c, axis=-1, keepdims=True), s_n)
            p_c = jnp.exp(s_c - m)
            p_n = jnp.exp(s_n - m)
            l = jnp.sum(p_c, axis=-1, keepdims=True) + p_n
            p8 = jnp.broadcast_to(p_c, (SUBLANES, p_c.shape[1])).astype(BF16)
            acc = jnp.dot(p8, v_c.astype(BF16), preferred_element_type=F32)[0:1]
            acc = acc + p_n.astype(BF16).astype(F32) * v_new.astype(BF16).astype(F32)
            outs.append(acc / l)
            lses.append(m + jnp.log(l))
        m = jnp.maximum(jnp.maximum(lses[0], lses[1]), lses[2])
        w = [jnp.exp(x - m) for x in lses]
        y_ref[0, :, cols] = (w[0] * outs[0] + w[1] * outs[1] + w[2] * outs[2]) / (w[0] + w[1] + w[2])


def _dil_attn_step(qkv, caches, *, heads):
    bsz = qkv.shape[0]
    hd = heads * ATT_DH
    new = qkv.reshape(bsz, 3 * N_GROUPS, 1, hd)
    args, in_specs = [], []
    for s in range(3 * N_GROUPS):
        args.append(new)
        in_specs.append(pl.BlockSpec((1, 1, 1, hd), lambda b, s=s: (b, s, 0, 0)))
    for g, (win, dil) in enumerate(DILATED_GROUPS):
        nk = win // dil
        for buf in caches[g]:
            assert buf.shape[1] == win
            args.append(buf.reshape(bsz, nk, dil * hd))
            in_specs.append(pl.BlockSpec((1, nk, hd), lambda b: (b, 0, 0)))
    y = pl.pallas_call(
        functools.partial(_dil_attn_step_body, heads=heads),
        grid=(bsz,),
        in_specs=in_specs,
        out_specs=pl.BlockSpec((1, 1, hd), lambda b: (b, 0, 0)),
        out_shape=jax.ShapeDtypeStruct((bsz, 1, hd), F32),
        compiler_params=_cparams("parallel"),
        name="dil_attn_step",
    )(*args)
    return y.reshape(bsz, hd)


def _xattn_body(q_ref, k_ref, v_ref, o_ref, *, heads, pad_rows):
    scale = XA_DH ** -0.5
    for h in range(heads):
        cols = slice(h * XA_DH, (h + 1) * XA_DH)
        q = q_ref[0][:, cols]
        rows = q.shape[0]
        if pad_rows:
            q = jnp.broadcast_to(q, (pad_rows, XA_DH))
        s = lax.dot_general(q.astype(BF16), k_ref[0][:, cols].astype(BF16), NT_DIMS,
                            preferred_element_type=F32) * scale
        m = jnp.max(s, axis=-1, keepdims=True)
        p = jnp.exp(s - m)
        l = jnp.sum(p, axis=-1, keepdims=True)
        o = jnp.dot(p.astype(BF16), v_ref[0][:, cols].astype(BF16), preferred_element_type=F32) / l
        o_ref[0, :, cols] = o[0:rows].astype(o_ref.dtype)


def _xattn(q, mem_k, mem_v, k_block, v_block, *, tq, out_dtype):
    bsz, t, hd = q.shape
    mem = mem_k.shape[1]
    heads = hd // XA_DH
    return pl.pallas_call(
        functools.partial(_xattn_body, heads=heads, pad_rows=SUBLANES if tq < SUBLANES else 0),
        grid=(bsz, t // tq),
        in_specs=[pl.BlockSpec((1, tq, hd), lambda b, i: (b, i, 0)),
                  pl.BlockSpec((1, mem, hd), lambda b, i: (b, 0, k_block)),
                  pl.BlockSpec((1, mem, hd), lambda b, i: (b, 0, v_block))],
        out_specs=pl.BlockSpec((1, tq, hd), lambda b, i: (b, i, 0)),
        out_shape=jax.ShapeDtypeStruct((bsz, t, hd), out_dtype),
        compiler_params=_cparams("parallel", "parallel"),
        name="xattn_seq" if tq >= SUBLANES else "xattn_step",
    )(q, mem_k, mem_v)


TN_COLS = 512


def _rope_tables(pos):
    rows = pos.shape[0]
    half = ROPE_DIM // 2
    inv_freq = ROPE_THETA ** (-jnp.arange(half, dtype=F32) * 2.0 / ROPE_DIM)
    ang = pos.astype(F32)[:, None] * inv_freq[None, :]
    cos, sin = jnp.cos(ang), jnp.sin(ang)
    rest = jnp.zeros((rows, LANES - ROPE_DIM), F32)
    zh = jnp.zeros((rows, half), F32)
    c = jnp.concatenate([cos, cos, rest + 1.0], axis=1)
    s_lo = jnp.concatenate([-sin, zh, rest], axis=1)
    s_hi = jnp.concatenate([zh, sin, rest], axis=1)
    one, zero = jnp.ones((rows, LANES), F32), jnp.zeros((rows, LANES), F32)
    return jnp.stack([c, one]), jnp.stack([s_lo, zero]), jnp.stack([s_hi, zero])


def _pad_rows(a, rows):
    return jnp.pad(a, ((0, rows - a.shape[0]), (0, 0)))


ROWS_S = 16
TM_ROWS = 512


def _ffn(xp, xs, gain, w_gu, w_down, layer, tag):
    nf = w_down.shape[1] // TN_COLS
    xn = _rms_norm(xp, gain, tm=TM_ROWS, out_dtype=BF16, name=tag + "_norm")
    (hid,), (hid_s,) = _proj(xn, xs, gain, w_gu, layer, w_col_blocks=(0, nf), n_col_blocks=nf,
                             epilogue=_ep_swiglu, out_dtypes=(BF16,), name=tag + "_gate_up")
    return _res_proj(hid, hid_s, w_down, layer, xp, xs, 0.5, name=tag + "_down")


def _conv_mixer(xp, xs, gain, w_in, w_conv, w_out, layer, state, *, nbat, t, nbat_s):
    d = xp.shape[1]
    nb = d // TN_COLS
    xn = _rms_norm(xp, gain, tm=TM_ROWS, out_dtype=BF16, name="conv_norm")
    (b_gate, u), (b_s, u_s) = _proj(xn, xs, gain, w_in, layer, w_col_blocks=(0, nb, 2 * nb), n_col_blocks=nb,
                                    epilogue=_ep_conv_in, out_dtypes=(F32, F32), name="conv_in")
    gated = _conv_seq(b_gate, u, w_conv, seq_len=t, tt=TM_ROWS, tc=TN_COLS)
    state_p = u.reshape(nbat, t, d)[:, t - (CONV_W - 1):]
    s0, s1 = state[:, 0], state[:, 1]
    gated_s = _conv_step(b_s, u_s, _pad_rows(s0, ROWS_S), _pad_rows(s1, ROWS_S), w_conv, tc=TN_COLS)
    state_s = jnp.stack([s1, u_s[:nbat_s]], axis=1)
    xp, xs = _res_proj(gated, gated_s, w_out, layer, xp, xs, 1.0, name="conv_out")
    return xp, xs, state_p, state_s


def _hgrn_mixer(xp, xs, gain, w_in, lower_bound, norm_gain, w_out, layer, state, *, nbat, t, nbat_s):
    xn = _rms_norm(xp, gain, tm=TM_ROWS, out_dtype=BF16, name="hgrn_norm")
    (proj,), (proj_s,) = _proj(xn, xs, gain, w_in, layer, w_col_blocks=(0,), n_col_blocks=w_in.shape[2] // TN_COLS,
                               epilogue=_ep_plain, out_dtypes=(F32,), name="hgrn_in")
    y, st = _hgrn_seq(proj, lower_bound, norm_gain, batch=nbat, seq_len=t)
    y_s, state_s = _hgrn_step(proj_s[:nbat_s], lower_bound, norm_gain, state)
    xp, xs = _res_proj(y, _pad_rows(y_s, ROWS_S).astype(BF16), w_out, layer, xp, xs, 1.0, name="hgrn_out")
    return xp, xs, jnp.swapaxes(st, -1, -2), state_s


def _attn_mixer(xp, xs, gain, w_qkv, w_out, layer, caches, *, nbat, t, nbat_s):
    d = xp.shape[1]
    heads = d // ATT_DH
    n_sec = 3 * N_GROUPS
    sec_blocks = d // TN_COLS
    tiles = t // TM_ROWS
    sel = lambda j: ((j // sec_blocks) % 3) // 2
    tab_spec = pl.BlockSpec((1, TM_ROWS, LANES), lambda j, i: (sel(j), i % tiles, 0))
    tab_s_spec = pl.BlockSpec((1, ROWS_S, LANES), lambda j, i: (sel(j), 0, 0))
    xn = _rms_norm(xp, gain, tm=TM_ROWS, out_dtype=BF16, name="attn_norm")
    (qkv,), (qkv_s,) = _proj(
        xn, xs, gain, w_qkv, layer, w_col_blocks=(0,), n_col_blocks=n_sec * sec_blocks,
        epilogue=_ep_rotary, out_dtypes=(F32,), name="attn_qkv",
        extra=_rope_tables(jnp.arange(t)), extra_specs=(tab_spec,) * 3,
        extra_s=_rope_tables(jnp.full((ROWS_S,), PAST_LEN, jnp.int32)), extra_s_specs=(tab_s_spec,) * 3)
    y = _dil_attn_seq(qkv.reshape(nbat, t, n_sec * d), batch=nbat, seq_len=t, heads=heads).reshape(nbat * t, d)
    y_s = _dil_attn_step(qkv_s[:nbat_s], caches, heads=heads)
    q5 = qkv.reshape(nbat, t, N_GROUPS, 3, heads, ATT_DH)
    q5_s = qkv_s[:nbat_s].reshape(nbat_s, 1, N_GROUPS, 3, heads, ATT_DH)
    rows_p, rows_s = [], []
    for g, (win, _) in enumerate(DILATED_GROUPS):
        keep = min(win, t)
        rows_p += [q5[:, t - keep:, g, 1], q5[:, t - keep:, g, 2]]
        rows_s += [q5_s[:, :, g, 1], q5_s[:, :, g, 2]]
    xp, xs = _res_proj(y, _pad_rows(y_s, ROWS_S).astype(BF16), w_out, layer, xp, xs, 1.0, name="attn_out")
    return xp, xs, rows_p, rows_s


def _cross_attention(xp, xs, gain, w_q, w_o, layer, mem_kv, mem_k_s, mem_v_s, *, nbat, t, nbat_s):
    hd = w_q.shape[2]
    xn = _rms_norm(xp, gain, tm=TM_ROWS, out_dtype=BF16, name="xattn_norm")
    (q,), (q_s,) = _proj(xn, xs, gain, w_q, layer, w_col_blocks=(0,), n_col_blocks=hd // TN_COLS,
                         epilogue=_ep_plain, out_dtypes=(F32,), name="xattn_q")
    o = _xattn(q.reshape(nbat, t, hd), mem_kv, mem_kv, 0, 1, tq=TM_ROWS, out_dtype=BF16).reshape(nbat * t, hd)
    o_s = _xattn(q_s[:nbat_s].reshape(nbat_s, 1, hd), mem_k_s, mem_v_s, 0, 0, tq=1, out_dtype=F32)
    o_s = _pad_rows(o_s.reshape(nbat_s, hd), ROWS_S).astype(BF16)
    return _res_proj(o, o_s, w_o, layer, xp, xs, 1.0, name="xattn_out")


def kernel(x_prompt, x_sample, state_conv, state_hgrn,
           cache_win_k0, cache_win_v0, cache_win_k1, cache_win_v1, cache_win_k2, cache_win_v2,
           cache_mem_k, cache_mem_v, mem_prompt,
           norm_ffn1, ffn1_w_gu, ffn1_w_down, norm_mix,
           conv_w_in, conv_w, conv_w_out,
           hgrn_w_in, hgrn_lb_logits, hgrn_norm, hgrn_w_out,
           attn_w_qkv, attn_w_out,
           norm_mem, xattn_w_kv, norm_xattn, xattn_w_q, xattn_w_o,
           norm_ffn2, ffn2_w_gu, ffn2_w_down, norm_final):
    batch, seq, d = x_prompt.shape
    dec_batch, dec_seq, _ = x_sample.shape
    assert dec_seq == 1 and dec_batch <= ROWS_S
    depth = norm_ffn1.shape[0]
    mem_len = mem_prompt.shape[1]
    xa_hd = xattn_w_q.shape[2]
    xa_heads = xa_hd // XA_DH
    win_cache = ((cache_win_k0, cache_win_v0), (cache_win_k1, cache_win_v1), (cache_win_k2, cache_win_v2))
    sizes = dict(nbat=batch, t=seq, nbat_s=dec_batch)

    lb_p = jax.nn.softmax(hgrn_lb_logits.astype(F32), axis=0)
    lower_bounds = jnp.cumsum(lb_p, axis=0) - lb_p[0]

    xp = x_prompt.reshape(batch * seq, d)
    xs = _pad_rows(x_sample.reshape(dec_batch, d), ROWS_S)
    mem = mem_prompt.reshape(batch * mem_len, d)
    p_conv, p_hgrn, p_win, p_mk, p_mv, s_conv, s_hgrn, s_win = ([] for _ in range(8))
    for i in range(depth):
        j, kind = divmod(i, N_MIXERS)
        xp, xs = _ffn(xp, xs, norm_ffn1[i], ffn1_w_gu, ffn1_w_down, i, "ffn1")
        if kind == 0:
            xp, xs, st_p, st_s = _conv_mixer(xp, xs, norm_mix[i], conv_w_in, conv_w[j], conv_w_out, j,
                                             state_conv[j], **sizes)
            p_conv.append(st_p)
            s_conv.append(st_s)
        elif kind == 1:
            xp, xs, st_p, st_s = _hgrn_mixer(xp, xs, norm_mix[i], hgrn_w_in, lower_bounds[i], hgrn_norm[j],
                                             hgrn_w_out, j, state_hgrn[j], **sizes)
            p_hgrn.append(st_p)
            s_hgrn.append(st_s)
        else:
            caches = [(kc[j], vc[j]) for kc, vc in win_cache]
            xp, xs, rows_p, rows_s = _attn_mixer(xp, xs, norm_mix[i], attn_w_qkv, attn_w_out, j, caches, **sizes)
            p_win.append(rows_p)
            s_win.append(rows_s)
        mem_n = _rms_norm(mem, norm_mem[i], tm=TM_ROWS, out_dtype=BF16, name="mem_norm")
        (kv,), _ = _proj(mem_n, None, None, xattn_w_kv, i, w_col_blocks=(0,), n_col_blocks=2 * xa_hd // TN_COLS,
                         epilogue=_ep_plain, out_dtypes=(F32,), name="mem_kv")
        kv = kv.reshape(batch, mem_len, 2 * xa_hd)
        p_mk.append(kv[:, :, :xa_hd].reshape(batch, mem_len, xa_heads, XA_DH))
        p_mv.append(kv[:, :, xa_hd:].reshape(batch, mem_len, xa_heads, XA_DH))
        xp, xs = _cross_attention(xp, xs, norm_xattn[i], xattn_w_q, xattn_w_o, i, kv,
                                  cache_mem_k[i].reshape(dec_batch, mem_len, xa_hd),
                                  cache_mem_v[i].reshape(dec_batch, mem_len, xa_hd), **sizes)
        xp, xs = _ffn(xp, xs, norm_ffn2[i], ffn2_w_gu, ffn2_w_down, i, "ffn2")
    y_prompt = _rms_norm(xp, norm_final, tm=TM_ROWS, out_dtype=F32, name="final_norm").reshape(batch, seq, d)
    y_sample = _rms_norm(xs, norm_final, tm=ROWS_S, out_dtype=F32, name="final_norm_s")[:dec_batch]
    y_sample = y_sample.reshape(dec_batch, 1, d)

    stack_win = lambda rows: [jnp.stack([r[g] for r in rows]) for g in range(2 * N_GROUPS)]
    return (y_prompt, y_sample,
            jnp.stack(p_conv), jnp.stack(p_hgrn), *stack_win(p_win), jnp.stack(p_mk), jnp.stack(p_mv),
            jnp.stack(s_conv), jnp.stack(s_hgrn), *stack_win(s_win))
```

```python
import functools

import jax
import jax.numpy as jnp
from jax import lax
from jax.experimental import pallas as pl
from jax.experimental.pallas import tpu as pltpu

F32 = jnp.float32
BF16 = jnp.bfloat16
EPS = 1e-6
LANES = 128
SUBLANES = 8
VMEM_LIMIT_BYTES = 56 << 20
NEG_BIG = -1e30

N_MIXERS = 3
CONV_W = 3
HG_DK = 128
ATT_DH = 128
DILATED_GROUPS = ((128, 1), (512, 4), (2048, 16))
N_GROUPS = 3
ROPE_DIM = ATT_DH // 4
ROPE_THETA = 500000.0
XA_DH = 128
GLA_BLOCK = 16
PAST_LEN = 16384

NT_DIMS = (((1,), (1,)), ((), ()))
TN_DIMS = (((0,), (0,)), ((), ()))


def _cparams(*semantics):
    return pltpu.CompilerParams(dimension_semantics=semantics, vmem_limit_bytes=VMEM_LIMIT_BYTES)


def _sigmoid(x):
    return 1.0 / (1.0 + jnp.exp(-x))


def _rms_body(x_ref, g_ref, o_ref):
    x = x_ref[...]
    ms = jnp.mean(x * x, axis=-1, keepdims=True)
    o_ref[...] = (x * lax.rsqrt(ms + EPS) * g_ref[...]).astype(o_ref.dtype)


def _rms_norm(x, gain, *, tm, out_dtype, name):
    m, k = x.shape
    return pl.pallas_call(
        _rms_body,
        grid=(m // tm,),
        in_specs=[pl.BlockSpec((tm, k), lambda i: (i, 0)), pl.BlockSpec((1, k), lambda i: (0, 0))],
        out_specs=pl.BlockSpec((tm, k), lambda i: (i, 0)),
        out_shape=jax.ShapeDtypeStruct((m, k), out_dtype),
        compiler_params=_cparams("parallel"),
        name=name,
    )(x, gain.reshape(1, k))


CAST_ROWS = 256


def _cast_weight(w_ref, wb_ref, slot):
    def body(c, carry):
        rows = pl.ds(pl.multiple_of(c * CAST_ROWS, CAST_ROWS), CAST_ROWS)
        wb_ref[slot, rows, :] = w_ref[0, rows, :].astype(BF16)
        return carry
    lax.fori_loop(0, w_ref.shape[1] // CAST_ROWS, body, 0)


def _ep_plain(accs, extra, outs):
    outs[0][...] = accs[0].astype(outs[0].dtype)


def _ep_swiglu(accs, extra, outs):
    gate, up = accs
    outs[0][...] = (gate * _sigmoid(gate) * up).astype(BF16)


def _ep_conv_in(accs, extra, outs):
    b_gate, c_gate, z = accs
    outs[0][...] = b_gate
    outs[1][...] = c_gate * z


def _ep_rotary(accs, extra, outs):
    cos, sin_lo, sin_hi = (r[0] for r in extra)
    acc = accs[0]
    half = ROPE_DIM // 2
    for c in range(acc.shape[1] // LANES):
        xc = acc[:, c * LANES:(c + 1) * LANES]
        outs[0][:, c * LANES:(c + 1) * LANES] = (
            xc * cos + pltpu.roll(xc, LANES - half, 1) * sin_lo + pltpu.roll(xc, half, 1) * sin_hi)


def _proj_body(*refs, n_w, n_extra, n_out, has_sample, epilogue):
    refs = list(refs)
    take = lambda n: [refs.pop(0) for _ in range(n)]
    (x_ref,) = take(1)
    xs_ref, gs_ref = take(2) if has_sample else (None, None)
    w_refs = take(n_w)
    extra = take(n_extra)
    extra_s = take(n_extra) if has_sample else []
    outs = take(n_out)
    outs_s = take(n_out) if has_sample else []
    (wb_ref,) = refs

    @pl.when(pl.program_id(1) == 0)
    def _():
        for k, w_ref in enumerate(w_refs):
            _cast_weight(w_ref, wb_ref, k)
        if has_sample:
            xs = xs_ref[...]
            ms = jnp.mean(xs * xs, axis=-1, keepdims=True)
            xsn = (xs * lax.rsqrt(ms + EPS) * gs_ref[...]).astype(BF16)
            epilogue([jnp.dot(xsn, wb_ref[k], preferred_element_type=F32) for k in range(n_w)],
                     extra_s, outs_s)

    x = x_ref[...]
    epilogue([jnp.dot(x, wb_ref[k], preferred_element_type=F32) for k in range(n_w)], extra, outs)


def _proj(xn, xs, gain, w_stack, layer, *, w_col_blocks, n_col_blocks, epilogue, out_dtypes, name,
          tm=1024, tn=512, extra=(), extra_specs=(), extra_s=(), extra_s_specs=()):
    m, k = xn.shape
    tm = min(tm, m)
    has_sample = xs is not None
    assert m % tm == 0 and w_stack.shape[1] == k and k % CAST_ROWS == 0
    n_w, n_out = len(w_col_blocks), len(out_dtypes)
    n_total = n_col_blocks * tn
    args = [xn]
    in_specs = [pl.BlockSpec((tm, k), lambda j, i: (i, 0))]
    if has_sample:
        rs = xs.shape[0]
        args += [xs, gain.reshape(1, k)]
        in_specs += [pl.BlockSpec((rs, k), lambda j, i: (0, 0)), pl.BlockSpec((1, k), lambda j, i: (0, 0))]
    for off in w_col_blocks:
        args.append(w_stack)
        in_specs.append(pl.BlockSpec((1, k, tn), lambda j, i, off=off: (layer, 0, j + off)))
    args += list(extra)
    in_specs += list(extra_specs)
    out_shape = [jax.ShapeDtypeStruct((m, n_total), dt) for dt in out_dtypes]
    out_specs = [pl.BlockSpec((tm, tn), lambda j, i: (i, j)) for _ in out_dtypes]
    if has_sample:
        args += list(extra_s)
        in_specs += list(extra_s_specs)
        out_shape += [jax.ShapeDtypeStruct((rs, n_total), dt) for dt in out_dtypes]
        out_specs += [pl.BlockSpec((rs, tn), lambda j, i: (0, j)) for _ in out_dtypes]
    res = pl.pallas_call(
        functools.partial(_proj_body, n_w=n_w, n_extra=len(extra), n_out=n_out,
                          has_sample=has_sample, epilogue=epilogue),
        grid=(n_col_blocks, m // tm),
        in_specs=in_specs,
        out_specs=out_specs,
        out_shape=out_shape,
        scratch_shapes=[pltpu.VMEM((n_w, k, tn), BF16)],
        compiler_params=_cparams("arbitrary", "arbitrary"),
        name=name,
    )(*args)
    return (res[:n_out], res[n_out:]) if has_sample else (res, None)


def _res_proj_body(a_ref, as_ref, w_ref, r_ref, rs_ref, o_ref, os_ref, wb_ref, *, scale):
    @pl.when(pl.program_id(1) == 0)
    def _():
        _cast_weight(w_ref, wb_ref, 0)
        os_ref[...] = rs_ref[...] + scale * jnp.dot(as_ref[...], wb_ref[0], preferred_element_type=F32)

    o_ref[...] = r_ref[...] + scale * jnp.dot(a_ref[...], wb_ref[0], preferred_element_type=F32)


def _res_proj(a, a_s, w_stack, layer, res, res_s, scale, *, name, tm=512, tn=512):
    m, k = a.shape
    rs = a_s.shape[0]
    n = w_stack.shape[2]
    assert m % tm == 0 and n % tn == 0 and k % CAST_ROWS == 0 and res.shape == (m, n)
    return pl.pallas_call(
        functools.partial(_res_proj_body, scale=scale),
        grid=(n // tn, m // tm),
        in_specs=[pl.BlockSpec((tm, k), lambda j, i: (i, 0)),
                  pl.BlockSpec((rs, k), lambda j, i: (0, 0)),
                  pl.BlockSpec((1, k, tn), lambda j, i: (layer, 0, j)),
                  pl.BlockSpec((tm, tn), lambda j, i: (i, j)),
                  pl.BlockSpec((rs, tn), lambda j, i: (0, j))],
        out_specs=[pl.BlockSpec((tm, tn), lambda j, i: (i, j)),
                   pl.BlockSpec((rs, tn), lambda j, i: (0, j))],
        out_shape=[jax.ShapeDtypeStruct((m, n), F32), jax.ShapeDtypeStruct((rs, n), F32)],
        scratch_shapes=[pltpu.VMEM((1, k, tn), BF16)],
        compiler_params=_cparams("arbitrary", "arbitrary"),
        name=name,
    )(a, a_s, w_stack, res, res_s)


def _res_proj_norm_body(a_ref, as_ref, w_ref, r_ref, rs_ref, g_ref, o_ref, os_ref, xn_ref, wb_ref, *, scale):
    @pl.when(pl.program_id(0) == 0)
    def _():
        _cast_weight(w_ref, wb_ref, 0)
        os_ref[...] = rs_ref[...] + scale * jnp.dot(as_ref[...], wb_ref[0], preferred_element_type=F32)

    x = r_ref[...] + scale * jnp.dot(a_ref[...], wb_ref[0], preferred_element_type=F32)
    o_ref[...] = x
    ms = jnp.mean(x * x, axis=-1, keepdims=True)
    xn_ref[...] = (x * lax.rsqrt(ms + EPS) * g_ref[...]).astype(BF16)


def _res_proj_norm(a, a_s, w_stack, layer, res, res_s, scale, next_gain, *, name, tm=512):
    m, k = a.shape
    rs = a_s.shape[0]
    n = w_stack.shape[2]
    assert m % tm == 0 and k % CAST_ROWS == 0 and res.shape == (m, n)
    return pl.pallas_call(
        functools.partial(_res_proj_norm_body, scale=scale),
        grid=(m // tm,),
        in_specs=[pl.BlockSpec((tm, k), lambda i: (i, 0)),
                  pl.BlockSpec((rs, k), lambda i: (0, 0)),
                  pl.BlockSpec((1, k, n), lambda i: (layer, 0, 0), pipeline_mode=pl.Buffered(1)),
                  pl.BlockSpec((tm, n), lambda i: (i, 0)),
                  pl.BlockSpec((rs, n), lambda i: (0, 0)),
                  pl.BlockSpec((1, n), lambda i: (0, 0))],
        out_specs=[pl.BlockSpec((tm, n), lambda i: (i, 0)),
                   pl.BlockSpec((rs, n), lambda i: (0, 0)),
                   pl.BlockSpec((tm, n), lambda i: (i, 0))],
        out_shape=[jax.ShapeDtypeStruct((m, n), F32), jax.ShapeDtypeStruct((rs, n), F32),
                   jax.ShapeDtypeStruct((m, n), BF16)],
        scratch_shapes=[pltpu.VMEM((1, k, n), BF16)],
        compiler_params=_cparams("arbitrary"),
        name=name,
    )(a, a_s, w_stack, res, res_s, next_gain.reshape(1, n))


def _conv_seq_body(b_ref, u_ref, up_ref, w_ref, o_ref, ue_ref, *, tiles_per_seq):
    tt = u_ref.shape[0]
    first = (pl.program_id(0) % tiles_per_seq) == 0
    ue_ref[0:SUBLANES, :] = jnp.where(first, 0.0, up_ref[...])
    ue_ref[SUBLANES:SUBLANES + tt, :] = u_ref[...]
    w = w_ref[...]
    conv = (w[0:1] * ue_ref[pl.ds(SUBLANES - 2, tt), :] + w[1:2] * ue_ref[pl.ds(SUBLANES - 1, tt), :]
            + w[2:3] * u_ref[...])
    o_ref[...] = (b_ref[...] * conv).astype(BF16)


def _conv_seq(b_gate, u, w_conv, *, seq_len, tt, tc):
    m, c = u.shape
    rb = tt // SUBLANES
    return pl.pallas_call(
        functools.partial(_conv_seq_body, tiles_per_seq=seq_len // tt),
        grid=(m // tt, c // tc),
        in_specs=[pl.BlockSpec((tt, tc), lambda i, j: (i, j)),
                  pl.BlockSpec((tt, tc), lambda i, j: (i, j)),
                  pl.BlockSpec((SUBLANES, tc), lambda i, j: (jnp.maximum(i * rb - 1, 0), j)),
                  pl.BlockSpec((CONV_W, tc), lambda i, j: (0, j))],
        out_specs=pl.BlockSpec((tt, tc), lambda i, j: (i, j)),
        out_shape=jax.ShapeDtypeStruct((m, c), BF16),
        scratch_shapes=[pltpu.VMEM((tt + SUBLANES, tc), F32)],
        compiler_params=_cparams("parallel", "arbitrary"),
        name="conv_seq",
    )(b_gate, u, u, w_conv)


def _conv_step_body(b_ref, u_ref, s0_ref, s1_ref, w_ref, o_ref):
    w = w_ref[...]
    conv = w[0:1] * s0_ref[...] + w[1:2] * s1_ref[...] + w[2:3] * u_ref[...]
    o_ref[...] = (b_ref[...] * conv).astype(BF16)


def _conv_step(b_gate, u, s0, s1, w_conv, *, tc):
    m, c = u.shape
    spec = pl.BlockSpec((m, tc), lambda j: (0, j))
    return pl.pallas_call(
        _conv_step_body,
        grid=(c // tc,),
        in_specs=[spec, spec, spec, spec, pl.BlockSpec((CONV_W, tc), lambda j: (0, j))],
        out_specs=spec,
        out_shape=jax.ShapeDtypeStruct((m, c), BF16),
        compiler_params=_cparams("parallel"),
        name="conv_step",
    )(b_gate, u, s0, s1, w_conv)


def _hgrn_seq_body(q_ref, z_ref, v_ref, g_ref, lb_ref, gn_ref, tri_ref, y_ref, st_out_ref,
                   st_ref, qs_ref, ks_ref, cum_ref, o_ref, *, hb):
    @pl.when(pl.program_id(2) == 0)
    def _():
        st_ref[...] = jnp.zeros_like(st_ref)

    tt = q_ref.shape[0]
    lb = lb_ref[...]
    z = z_ref[...]
    forget = lb + (1.0 - lb) * _sigmoid(z)
    ks_ref[...] = (1.0 - lb) * _sigmoid(-z)
    q = q_ref[...]
    qs_ref[...] = q * _sigmoid(q)
    cum_ref[...] = jnp.dot(tri_ref[...], jnp.log(forget), preferred_element_type=F32,
                           precision=lax.Precision.HIGHEST)

    t_idx = lax.broadcasted_iota(jnp.int32, (GLA_BLOCK, 1), 0)
    half = GLA_BLOCK // 2

    def block(j, carry):
        rows = pl.ds(pl.multiple_of(j * GLA_BLOCK, GLA_BLOCK), GLA_BLOCK)
        for h in range(hb):
            cols = slice(h * HG_DK, (h + 1) * HG_DK)
            cb = cum_ref[rows, cols]
            qb = qs_ref[rows, cols]
            kb = ks_ref[rows, cols]
            vb = v_ref[rows, cols]
            last = cb[GLA_BLOCK - 1:GLA_BLOCK, :]
            st = st_ref[h]
            o = lax.dot_general((qb * jnp.exp(cb)).astype(BF16), st.astype(BF16), NT_DIMS,
                                preferred_element_type=F32)
            o_lo = jnp.zeros((half, LANES), F32)
            o_hi = jnp.zeros((half, LANES), F32)
            for s in range(GLA_BLOCK):
                cs = cb[s:s + 1, :]
                qk_hi = qb[half:, :] * kb[s:s + 1, :]
                g_hi = jnp.where(t_idx[half:] >= s, jnp.exp(cb[half:, :] - cs), 0.0) * qk_hi
                o_hi = o_hi + jnp.sum(g_hi, axis=-1, keepdims=True) * vb[s:s + 1, :]
                if s < half:
                    qk_lo = qb[:half, :] * kb[s:s + 1, :]
                    g_lo = jnp.where(t_idx[:half] >= s, jnp.exp(cb[:half, :] - cs), 0.0) * qk_lo
                    o_lo = o_lo + jnp.sum(g_lo, axis=-1, keepdims=True) * vb[s:s + 1, :]
            o_ref[rows, cols] = o + jnp.concatenate([o_lo, o_hi], axis=0)
            kd = kb * jnp.exp(last - cb)
            upd = lax.dot_general(vb.astype(BF16), kd.astype(BF16), TN_DIMS, preferred_element_type=F32)
            st_ref[h] = st * jnp.exp(last) + upd
        return carry

    lax.fori_loop(0, tt // GLA_BLOCK, block, 0)

    for h in range(hb):
        cols = slice(h * HG_DK, (h + 1) * HG_DK)
        o = o_ref[:, cols]
        ms = jnp.mean(o * o, axis=-1, keepdims=True)
        g = g_ref[:, cols]
        y_ref[:, cols] = (o * lax.rsqrt(ms + EPS) * gn_ref[0, h:h + 1, :] * (g * _sigmoid(g))).astype(BF16)

    @pl.when(pl.program_id(2) == pl.num_programs(2) - 1)
    def _():
        st_out_ref[0] = st_ref[...]


def _hgrn_seq(proj, lower_bound, norm_gain, *, batch, seq_len, tt=256, hb=4):
    m, n4 = proj.shape
    d = n4 // 4
    heads = d // HG_DK
    hg = heads // hb
    nt = seq_len // tt
    wcols = hb * HG_DK
    tri = (jnp.arange(tt)[:, None] >= jnp.arange(tt)[None, :]) & (
        jnp.arange(tt)[:, None] // GLA_BLOCK == jnp.arange(tt)[None, :] // GLA_BLOCK)
    tri = tri.astype(F32)

    def col(sec):
        return pl.BlockSpec((tt, wcols), lambda b, h, t, sec=sec: (b * nt + t, sec * hg + h))

    y, st = pl.pallas_call(
        functools.partial(_hgrn_seq_body, hb=hb),
        grid=(batch, hg, nt),
        in_specs=[col(0), col(1), col(2), col(3),
                  pl.BlockSpec((1, wcols), lambda b, h, t: (0, h)),
                  pl.BlockSpec((1, hb, HG_DK), lambda b, h, t: (h, 0, 0)),
                  pl.BlockSpec((tt, tt), lambda b, h, t: (0, 0))],
        out_specs=[pl.BlockSpec((tt, wcols), lambda b, h, t: (b * nt + t, h)),
                   pl.BlockSpec((1, hb, HG_DK, HG_DK), lambda b, h, t: (b, h, 0, 0))],
        out_shape=[jax.ShapeDtypeStruct((m, d), BF16),
                   jax.ShapeDtypeStruct((batch, heads, HG_DK, HG_DK), F32)],
        scratch_shapes=[pltpu.VMEM((hb, HG_DK, HG_DK), F32), pltpu.VMEM((tt, wcols), F32),
                        pltpu.VMEM((tt, wcols), F32), pltpu.VMEM((tt, wcols), F32),
                        pltpu.VMEM((tt, wcols), F32)],
        compiler_params=_cparams("parallel", "parallel", "arbitrary"),
        name="hgrn_seq",
    )(proj, proj, proj, proj, lower_bound.reshape(1, d), norm_gain.reshape(hg, hb, HG_DK), tri)
    return y, st


def _hgrn_step_body(q_ref, z_ref, v_ref, g_ref, lb_ref, gn_ref, s_ref, y_ref, s_out_ref):
    heads = s_ref.shape[1]
    for h in range(heads):
        lb = lb_ref[h]
        z = z_ref[0, h]
        forget = lb + (1.0 - lb) * _sigmoid(z)
        k = (1.0 - lb) * _sigmoid(-z)
        q = q_ref[0, h]
        q = q * _sigmoid(q)
        s_new = forget * s_ref[0, h] + k * v_ref[0, h]
        s_out_ref[0, h] = s_new
        o = jnp.sum(q * s_new, axis=0, keepdims=True)
        ms = jnp.mean(o * o, axis=-1, keepdims=True)
        g = g_ref[0, h]
        y_ref[0, h] = o * lax.rsqrt(ms + EPS) * gn_ref[h] * (g * _sigmoid(g))


def _hgrn_step(proj, lower_bound, norm_gain, state):
    bsz, heads, dk, dv = state.shape
    d = heads * dk
    q, z, v, g = (proj[:, s * d:(s + 1) * d] for s in range(4))
    col = lambda a: a.reshape(bsz, heads, dk, 1)
    row = lambda a: a.reshape(bsz, heads, 1, dv)
    col_spec = pl.BlockSpec((1, heads, dk, 1), lambda b: (b, 0, 0, 0))
    row_spec = pl.BlockSpec((1, heads, 1, dv), lambda b: (b, 0, 0, 0))
    st_spec = pl.BlockSpec((1, heads, dk, dv), lambda b: (b, 0, 0, 0))
    y, s_new = pl.pallas_call(
        _hgrn_step_body,
        grid=(bsz,),
        in_specs=[col_spec, col_spec, row_spec, row_spec,
                  pl.BlockSpec((heads, dk, 1), lambda b: (0, 0, 0)),
                  pl.BlockSpec((heads, 1, dv), lambda b: (0, 0, 0)),
                  st_spec],
        out_specs=[row_spec, st_spec],
        out_shape=[jax.ShapeDtypeStruct((bsz, heads, 1, dv), F32),
                   jax.ShapeDtypeStruct(state.shape, F32)],
        compiler_params=_cparams("parallel"),
        name="hgrn_step",
    )(col(q), col(z), row(v), row(g), lower_bound.reshape(heads, dk, 1),
      norm_gain.reshape(heads, 1, dv), state)
    return y.reshape(bsz, d), s_new


def _band_block(q, k, v, scale, has_prev):
    n, nkeys = q.shape[0], k.shape[0]
    qi = lax.broadcasted_iota(jnp.int32, (n, nkeys), 0)
    ki = lax.broadcasted_iota(jnp.int32, (n, nkeys), 1)
    s = lax.dot_general(q.astype(BF16), k.astype(BF16), NT_DIMS, preferred_element_type=F32) * scale
    visible = ((ki >= qi) & (ki <= qi + n)) if has_prev else (ki <= qi)
    s = jnp.where(visible, s, NEG_BIG)
    m = jnp.max(s, axis=-1, keepdims=True)
    p = jnp.exp(s - m)
    l = jnp.sum(p, axis=-1, keepdims=True)
    acc = jnp.dot(p.astype(BF16), v.astype(BF16), preferred_element_type=F32)
    return acc / l, m + jnp.log(l)


def _dil_attn_seq_body(*refs, seq_len):
    qkv = refs[:3 * N_GROUPS]
    y_ref = refs[3 * N_GROUPS]
    og_ref, lse_ref = refs[3 * N_GROUPS + 1:]
    scale = ATT_DH ** -0.5
    for g, (win, dil) in enumerate(DILATED_GROUPS):
        nk = win // dil
        q_ref, k_ref, v_ref = qkv[3 * g:3 * g + 3]
        for r in range(dil):
            for n in range(seq_len // dil // nk):
                def rows(first_blk, n_blk):
                    start, size = r + dil * nk * first_blk, nk * n_blk
                    return pl.ds(start, size, stride=dil) if dil > 1 else pl.ds(start, size)
                cur = rows(n, 1)
                keys = rows(n - 1, 2) if n else cur
                o, lse = _band_block(q_ref[0, cur, :], k_ref[0, keys, :], v_ref[0, keys, :], scale, n > 0)
                og_ref[g, cur, :] = o
                lse_ref[g, cur, :] = jnp.broadcast_to(lse, (nk, LANES))
    lse = [lse_ref[g] for g in range(N_GROUPS)]
    m = jnp.maximum(jnp.maximum(lse[0], lse[1]), lse[2])
    w = [jnp.exp(x - m) for x in lse]
    tot = w[0] + w[1] + w[2]
    y_ref[0] = ((w[0] * og_ref[0] + w[1] * og_ref[1] + w[2] * og_ref[2]) / tot).astype(BF16)


def _dil_attn_seq(qkv, *, batch, seq_len, heads):
    in_specs = []
    for g in range(N_GROUPS):
        for c in range(3):
            in_specs.append(pl.BlockSpec((1, seq_len, ATT_DH),
                                         lambda b, h, s=(g * 3 + c): (b, 0, s * heads + h)))
    return pl.pallas_call(
        functools.partial(_dil_attn_seq_body, seq_len=seq_len),
        grid=(batch, heads),
        in_specs=in_specs,
        out_specs=pl.BlockSpec((1, seq_len, ATT_DH), lambda b, h: (b, 0, h)),
        out_shape=jax.ShapeDtypeStruct((batch, seq_len, heads * ATT_DH), BF16),
        scratch_shapes=[pltpu.VMEM((N_GROUPS, seq_len, ATT_DH), F32),
                        pltpu.VMEM((N_GROUPS, seq_len, LANES), F32)],
        compiler_params=_cparams("parallel", "parallel"),
        name="dil_attn_seq",
    )(*([qkv] * (3 * N_GROUPS)))


def _dil_attn_step_body(*refs, heads):
    new = refs[:3 * N_GROUPS]
    caches = refs[3 * N_GROUPS:5 * N_GROUPS]
    y_ref = refs[5 * N_GROUPS]
    scale = ATT_DH ** -0.5
    for h in range(heads):
        cols = slice(h * ATT_DH, (h + 1) * ATT_DH)
        outs, lses = [], []
        for g in range(N_GROUPS):
            q = new[3 * g][0, 0][:, cols]
            k_new = new[3 * g + 1][0, 0][:, cols]
            v_new = new[3 * g + 2][0, 0][:, cols]
            k_c = caches[2 * g][0][:, cols]
            v_c = caches[2 * g + 1][0][:, cols]
            q8 = jnp.broadcast_to(q, (SUBLANES, ATT_DH)).astype(BF16)
            s_c = lax.dot_general(q8, k_c.astype(BF16), NT_DIMS, preferred_element_type=F32)[0:1] * scale
            s_n = jnp.sum(q.astype(BF16).astype(F32) * k_new.astype(BF16).astype(F32),
                          axis=-1, keepdims=True) * scale
            m = jnp.maximum(jnp.max(s_c, axis=-1, keepdims=True), s_n)
            p_c = jnp.exp(s_c - m)
            p_n = jnp.exp(s_n - m)
            l = jnp.sum(p_c, axis=-1, keepdims=True) + p_n
            p8 = jnp.broadcast_to(p_c, (SUBLANES, p_c.shape[1])).astype(BF16)
            acc = jnp.dot(p8, v_c.astype(BF16), preferred_element_type=F32)[0:1]
            acc = acc + p_n.astype(BF16).astype(F32) * v_new.astype(BF16).astype(F32)
            outs.append(acc / l)
            lses.append(m + jnp.log(l))
        m = jnp.maximum(jnp.maximum(lses[0], lses[1]), lses[2])
        w = [jnp.exp(x - m) for x in lses]
        y_ref[0, :, cols] = (w[0] * outs[0] + w[1] * outs[1] + w[2] * outs[2]) / (w[0] + w[1] + w[2])


def _dil_attn_step(qkv, caches, layer, *, heads):
    bsz = qkv.shape[0]
    hd = heads * ATT_DH
    new = qkv.reshape(bsz, 3 * N_GROUPS, 1, hd)
    args, in_specs = [], []
    for s in range(3 * N_GROUPS):
        args.append(new)
        in_specs.append(pl.BlockSpec((1, 1, 1, hd), lambda b, s=s: (b, s, 0, 0)))
    for g, (win, dil) in enumerate(DILATED_GROUPS):
        nk = win // dil
        for buf in caches[g]:
            assert buf.shape[1:3] == (bsz, win)
            args.append(buf.reshape(buf.shape[0] * bsz, nk, dil * hd))
            in_specs.append(pl.BlockSpec((1, nk, hd), lambda b: (layer * bsz + b, 0, 0)))
    y = pl.pallas_call(
        functools.partial(_dil_attn_step_body, heads=heads),
        grid=(bsz,),
        in_specs=in_specs,
        out_specs=pl.BlockSpec((1, 1, hd), lambda b: (b, 0, 0)),
        out_shape=jax.ShapeDtypeStruct((bsz, 1, hd), F32),
        compiler_params=_cparams("parallel"),
        name="dil_attn_step",
    )(*args)
    return y.reshape(bsz, hd)


def _xattn_body(q_ref, k_ref, v_ref, o_ref, *, heads, pad_rows):
    scale = XA_DH ** -0.5
    for h in range(heads):
        cols = slice(h * XA_DH, (h + 1) * XA_DH)
        q = q_ref[0][:, cols]
        rows = q.shape[0]
        if pad_rows:
            q = jnp.broadcast_to(q, (pad_rows, XA_DH))
        s = lax.dot_general(q.astype(BF16), k_ref[0][:, cols].astype(BF16), NT_DIMS,
                            preferred_element_type=F32) * scale
        m = jnp.max(s, axis=-1, keepdims=True)
        p = jnp.exp(s - m)
        l = jnp.sum(p, axis=-1, keepdims=True)
        o = jnp.dot(p.astype(BF16), v_ref[0][:, cols].astype(BF16), preferred_element_type=F32) / l
        o_ref[0, :, cols] = o[0:rows].astype(o_ref.dtype)


def _xattn(q, mem_k, mem_v, k_block, v_block, *, tq, out_dtype, mem_row0=0):
    bsz, t, hd = q.shape
    mem = mem_k.shape[1]
    heads = hd // XA_DH
    return pl.pallas_call(
        functools.partial(_xattn_body, heads=heads, pad_rows=SUBLANES if tq < SUBLANES else 0),
        grid=(bsz, t // tq),
        in_specs=[pl.BlockSpec((1, tq, hd), lambda b, i: (b, i, 0)),
                  pl.BlockSpec((1, mem, hd), lambda b, i: (mem_row0 + b, 0, k_block)),
                  pl.BlockSpec((1, mem, hd), lambda b, i: (mem_row0 + b, 0, v_block))],
        out_specs=pl.BlockSpec((1, tq, hd), lambda b, i: (b, i, 0)),
        out_shape=jax.ShapeDtypeStruct((bsz, t, hd), out_dtype),
        compiler_params=_cparams("parallel", "parallel"),
        name="xattn_seq" if tq >= SUBLANES else "xattn_step",
    )(q, mem_k, mem_v)


TN_COLS = 512


def _rope_tables(pos):
    rows = pos.shape[0]
    half = ROPE_DIM // 2
    inv_freq = ROPE_THETA ** (-jnp.arange(half, dtype=F32) * 2.0 / ROPE_DIM)
    ang = pos.astype(F32)[:, None] * inv_freq[None, :]
    cos, sin = jnp.cos(ang), jnp.sin(ang)
    rest = jnp.zeros((rows, LANES - ROPE_DIM), F32)
    zh = jnp.zeros((rows, half), F32)
    c = jnp.concatenate([cos, cos, rest + 1.0], axis=1)
    s_lo = jnp.concatenate([-sin, zh, rest], axis=1)
    s_hi = jnp.concatenate([zh, sin, rest], axis=1)
    one, zero = jnp.ones((rows, LANES), F32), jnp.zeros((rows, LANES), F32)
    return jnp.stack([c, one]), jnp.stack([s_lo, zero]), jnp.stack([s_hi, zero])


def _pad_rows(a, rows):
    return jnp.pad(a, ((0, rows - a.shape[0]), (0, 0)))


ROWS_S = 16
TM_ROWS = 512
TN_WIDE = 1024


def _ffn(xp, xs, xn, gain, w_gu, w_down, layer, tag):
    nf = w_down.shape[1] // TN_COLS
    if xn is None:
        xn = _rms_norm(xp, gain, tm=TM_ROWS, out_dtype=BF16, name=tag + "_norm")
    (hid,), (hid_s,) = _proj(xn, xs, gain, w_gu, layer, w_col_blocks=(0, nf), n_col_blocks=nf,
                             epilogue=_ep_swiglu, out_dtypes=(BF16,), name=tag + "_gate_up")
    return _res_proj(hid, hid_s, w_down, layer, xp, xs, 0.5, name=tag + "_down")


def _conv_mixer(xp, xs, gain, w_in, w_conv, w_out, layer, state, next_gain, *, nbat, t, nbat_s):
    d = xp.shape[1]
    nb = d // TN_COLS
    xn = _rms_norm(xp, gain, tm=TM_ROWS, out_dtype=BF16, name="conv_norm")
    (b_gate, u), (b_s, u_s) = _proj(xn, xs, gain, w_in, layer, w_col_blocks=(0, nb, 2 * nb), n_col_blocks=nb,
                                    epilogue=_ep_conv_in, out_dtypes=(F32, F32), name="conv_in")
    gated = _conv_seq(b_gate, u, w_conv, seq_len=t, tt=TM_ROWS, tc=TN_COLS)
    state_p = u.reshape(nbat, t, d)[:, t - (CONV_W - 1):]
    s0, s1 = state[:, 0], state[:, 1]
    gated_s = _conv_step(b_s, u_s, _pad_rows(s0, ROWS_S), _pad_rows(s1, ROWS_S), w_conv, tc=TN_COLS)
    state_s = jnp.stack([s1, u_s[:nbat_s]], axis=1)
    xp, xs, xn = _res_proj_norm(gated, gated_s, w_out, layer, xp, xs, 1.0, next_gain, name="conv_out")
    return xp, xs, xn, state_p, state_s


def _hgrn_mixer(xp, xs, gain, w_in, lower_bound, norm_gain, w_out, layer, state, next_gain, *, nbat, t, nbat_s):
    xn = _rms_norm(xp, gain, tm=TM_ROWS, out_dtype=BF16, name="hgrn_norm")
    (proj,), (proj_s,) = _proj(xn, xs, gain, w_in, layer, w_col_blocks=(0,), n_col_blocks=w_in.shape[2] // TN_WIDE,
                               epilogue=_ep_plain, out_dtypes=(F32,), name="hgrn_in", tn=TN_WIDE)
    y, st = _hgrn_seq(proj, lower_bound, norm_gain, batch=nbat, seq_len=t)
    y_s, state_s = _hgrn_step(proj_s[:nbat_s], lower_bound, norm_gain, state)
    xp, xs, xn = _res_proj_norm(y, _pad_rows(y_s, ROWS_S).astype(BF16), w_out, layer, xp, xs, 1.0, next_gain,
                                name="hgrn_out")
    return xp, xs, xn, jnp.swapaxes(st, -1, -2), state_s


def _attn_mixer(xp, xs, gain, w_qkv, w_out, layer, caches, next_gain, *, nbat, t, nbat_s):
    d = xp.shape[1]
    heads = d // ATT_DH
    n_sec = 3 * N_GROUPS
    sec_blocks = d // TN_WIDE
    tm = min(1024, t)
    tiles = t // tm
    sel = lambda j: ((j // sec_blocks) % 3) // 2
    tab_spec = pl.BlockSpec((1, tm, LANES), lambda j, i: (sel(j), i % tiles, 0))
    tab_s_spec = pl.BlockSpec((1, ROWS_S, LANES), lambda j, i: (sel(j), 0, 0))
    xn = _rms_norm(xp, gain, tm=TM_ROWS, out_dtype=BF16, name="attn_norm")
    (qkv,), (qkv_s,) = _proj(
        xn, xs, gain, w_qkv, layer, w_col_blocks=(0,), n_col_blocks=n_sec * sec_blocks,
        epilogue=_ep_rotary, out_dtypes=(F32,), name="attn_qkv", tm=tm, tn=TN_WIDE,
        extra=_rope_tables(jnp.arange(t)), extra_specs=(tab_spec,) * 3,
        extra_s=_rope_tables(jnp.full((ROWS_S,), PAST_LEN, jnp.int32)), extra_s_specs=(tab_s_spec,) * 3)
    y = _dil_attn_seq(qkv.reshape(nbat, t, n_sec * d), batch=nbat, seq_len=t, heads=heads).reshape(nbat * t, d)
    y_s = _dil_attn_step(qkv_s[:nbat_s], caches, layer, heads=heads)
    q5 = qkv.reshape(nbat, t, N_GROUPS, 3, heads, ATT_DH)
    q5_s = qkv_s[:nbat_s].reshape(nbat_s, 1, N_GROUPS, 3, heads, ATT_DH)
    rows_p, rows_s = [], []
    for g, (win, _) in enumerate(DILATED_GROUPS):
        keep = min(win, t)
        rows_p += [q5[:, t - keep:, g, 1], q5[:, t - keep:, g, 2]]
        rows_s += [q5_s[:, :, g, 1], q5_s[:, :, g, 2]]
    xp, xs, xn = _res_proj_norm(y, _pad_rows(y_s, ROWS_S).astype(BF16), w_out, layer, xp, xs, 1.0, next_gain,
                                name="attn_out")
    return xp, xs, xn, rows_p, rows_s


def _cross_attention(xp, xs, xn, gain, w_q, w_o, layer, mem_kv, mem_k_s, mem_v_s, next_gain, *, nbat, t, nbat_s):
    hd = w_q.shape[2]
    (q,), (q_s,) = _proj(xn, xs, gain, w_q, layer, w_col_blocks=(0,), n_col_blocks=hd // TN_COLS,
                         epilogue=_ep_plain, out_dtypes=(F32,), name="xattn_q")
    o = _xattn(q.reshape(nbat, t, hd), mem_kv, mem_kv, 0, 1, tq=TM_ROWS, out_dtype=BF16).reshape(nbat * t, hd)
    o_s = _xattn(q_s[:nbat_s].reshape(nbat_s, 1, hd), mem_k_s, mem_v_s, 0, 0, tq=1, out_dtype=F32,
                 mem_row0=layer * nbat_s)
    o_s = _pad_rows(o_s.reshape(nbat_s, hd), ROWS_S).astype(BF16)
    return _res_proj_norm(o, o_s, w_o, layer, xp, xs, 1.0, next_gain, name="xattn_out")


def kernel(x_prompt, x_sample, state_conv, state_hgrn,
           cache_win_k0, cache_win_v0, cache_win_k1, cache_win_v1, cache_win_k2, cache_win_v2,
           cache_mem_k, cache_mem_v, mem_prompt,
           norm_ffn1, ffn1_w_gu, ffn1_w_down, norm_mix,
           conv_w_in, conv_w, conv_w_out,
           hgrn_w_in, hgrn_lb_logits, hgrn_norm, hgrn_w_out,
           attn_w_qkv, attn_w_out,
           norm_mem, xattn_w_kv, norm_xattn, xattn_w_q, xattn_w_o,
           norm_ffn2, ffn2_w_gu, ffn2_w_down, norm_final):
    batch, seq, d = x_prompt.shape
    dec_batch, dec_seq, _ = x_sample.shape
    assert dec_seq == 1 and dec_batch <= ROWS_S
    depth = norm_ffn1.shape[0]
    mem_len = mem_prompt.shape[1]
    xa_hd = xattn_w_q.shape[2]
    xa_heads = xa_hd // XA_DH
    win_cache = [(cache_win_k0, cache_win_v0), (cache_win_k1, cache_win_v1), (cache_win_k2, cache_win_v2)]
    mem_k_s = cache_mem_k.reshape(depth * dec_batch, mem_len, xa_hd)
    mem_v_s = cache_mem_v.reshape(depth * dec_batch, mem_len, xa_hd)
    sizes = dict(nbat=batch, t=seq, nbat_s=dec_batch)

    lb_p = jax.nn.softmax(hgrn_lb_logits.astype(F32), axis=0)
    lower_bounds = jnp.cumsum(lb_p, axis=0) - lb_p[0]

    xp = x_prompt.reshape(batch * seq, d)
    xs = _pad_rows(x_sample.reshape(dec_batch, d), ROWS_S)
    mem = mem_prompt.reshape(batch * mem_len, d)
    p_conv, p_hgrn, p_win, p_mk, p_mv, s_conv, s_hgrn, s_win = ([] for _ in range(8))
    for i in range(depth):
        j, kind = divmod(i, N_MIXERS)
        xp, xs = _ffn(xp, xs, None, norm_ffn1[i], ffn1_w_gu, ffn1_w_down, i, "ffn1")
        if kind == 0:
            xp, xs, xn, st_p, st_s = _conv_mixer(xp, xs, norm_mix[i], conv_w_in, conv_w[j], conv_w_out, j,
                                                 state_conv[j], norm_xattn[i], **sizes)
            p_conv.append(st_p)
            s_conv.append(st_s)
        elif kind == 1:
            xp, xs, xn, st_p, st_s = _hgrn_mixer(xp, xs, norm_mix[i], hgrn_w_in, lower_bounds[i], hgrn_norm[j],
                                                 hgrn_w_out, j, state_hgrn[j], norm_xattn[i], **sizes)
            p_hgrn.append(st_p)
            s_hgrn.append(st_s)
        else:
            xp, xs, xn, rows_p, rows_s = _attn_mixer(xp, xs, norm_mix[i], attn_w_qkv, attn_w_out, j, win_cache,
                                                     norm_xattn[i], **sizes)
            p_win.append(rows_p)
            s_win.append(rows_s)
        mem_n = _rms_norm(mem, norm_mem[i], tm=TM_ROWS, out_dtype=BF16, name="mem_norm")
        (kv,), _ = _proj(mem_n, None, None, xattn_w_kv, i, w_col_blocks=(0,), n_col_blocks=2 * xa_hd // TN_COLS,
                         epilogue=_ep_plain, out_dtypes=(F32,), name="mem_kv")
        kv = kv.reshape(batch, mem_len, 2 * xa_hd)
        p_mk.append(kv[:, :, :xa_hd].reshape(batch, mem_len, xa_heads, XA_DH))
        p_mv.append(kv[:, :, xa_hd:].reshape(batch, mem_len, xa_heads, XA_DH))
        xp, xs, xn = _cross_attention(xp, xs, xn, norm_xattn[i], xattn_w_q, xattn_w_o, i, kv, mem_k_s, mem_v_s,
                                      norm_ffn2[i], **sizes)
        xp, xs = _ffn(xp, xs, xn, norm_ffn2[i], ffn2_w_gu, ffn2_w_down, i, "ffn2")
    y_prompt = _rms_norm(xp, norm_final, tm=TM_ROWS, out_dtype=F32, name="final_norm").reshape(batch, seq, d)
    y_sample = _rms_norm(xs, norm_final, tm=ROWS_S, out_dtype=F32, name="final_norm_s")[:dec_batch]
    y_sample = y_sample.reshape(dec_batch, 1, d)

    stack_win = lambda rows: [jnp.stack([r[g] for r in rows]) for g in range(2 * N_GROUPS)]
    return (y_prompt, y_sample,
            jnp.stack(p_conv), jnp.stack(p_hgrn), *stack_win(p_win), jnp.stack(p_mk), jnp.stack(p_mv),
            jnp.stack(s_conv), jnp.stack(s_hgrn), *stack_win(s_win))
```

```python
import functools

import jax
import jax.numpy as jnp
from jax import lax
from jax.experimental import pallas as pl
from jax.experimental.pallas import tpu as pltpu

F32 = jnp.float32
BF16 = jnp.bfloat16
EPS = 1e-6
LANES = 128
SUBLANES = 8
MXU_COLS = 256
VMEM_LIMIT_BYTES = 56 << 20
NEG_BIG = -1e30

N_MIXERS = 3
CONV_W = 3
HG_DK = 128
ATT_DH = 128
DILATED_GROUPS = ((128, 1), (512, 4), (2048, 16))
N_GROUPS = 3
ROPE_DIM = ATT_DH // 4
ROPE_THETA = 500000.0
XA_DH = 128
GLA_BLOCK = 16
PAST_LEN = 16384

NT_DIMS = (((1,), (1,)), ((), ()))
TN_DIMS = (((0,), (0,)), ((), ()))


def _cparams(*semantics):
    return pltpu.CompilerParams(dimension_semantics=semantics, vmem_limit_bytes=VMEM_LIMIT_BYTES)


def _sigmoid(x):
    return 1.0 / (1.0 + jnp.exp(-x))


def _rms_body(x_ref, g_ref, o_ref):
    x = x_ref[...]
    ms = jnp.mean(x * x, axis=-1, keepdims=True)
    o_ref[...] = (x * lax.rsqrt(ms + EPS) * g_ref[...]).astype(o_ref.dtype)


def _rms_norm(x, gain, *, tm, out_dtype, name):
    m, k = x.shape
    return pl.pallas_call(
        _rms_body,
        grid=(m // tm,),
        in_specs=[pl.BlockSpec((tm, k), lambda i: (i, 0)), pl.BlockSpec((1, k), lambda i: (0, 0))],
        out_specs=pl.BlockSpec((tm, k), lambda i: (i, 0)),
        out_shape=jax.ShapeDtypeStruct((m, k), out_dtype),
        compiler_params=_cparams("parallel"),
        name=name,
    )(x, gain.reshape(1, k))


CAST_ROWS = 256


def _cast_weight(w_ref, wb_ref, slot):
    def body(c, carry):
        rows = pl.ds(pl.multiple_of(c * CAST_ROWS, CAST_ROWS), CAST_ROWS)
        wb_ref[slot, rows, :] = w_ref[0, rows, :].astype(BF16)
        return carry
    lax.fori_loop(0, w_ref.shape[1] // CAST_ROWS, body, 0)


def _ep_plain(accs, extra, outs, col0):
    acc = accs[0]
    outs[0][:, col0:col0 + acc.shape[1]] = acc.astype(outs[0].dtype)


def _ep_swiglu(accs, extra, outs, col0):
    gate, up = accs
    outs[0][:, col0:col0 + gate.shape[1]] = (gate * _sigmoid(gate) * up).astype(BF16)


def _ep_conv_seq(accs, extra, outs, col0, scratch, *, tiles_per_seq):
    b_gate, c_gate, z = accs
    (w_ref,), (gated_ref, tail_ref), (carry_ref, ue_ref) = extra, outs, scratch
    tm = z.shape[0]
    u = c_gate * z
    first = (pl.program_id(1) % tiles_per_seq) == 0
    ue_ref[0:SUBLANES, :] = jnp.where(first, 0.0, carry_ref[...])
    ue_ref[SUBLANES:SUBLANES + tm, :] = u
    w = w_ref[...]
    conv = (w[0:1] * ue_ref[pl.ds(SUBLANES - 2, tm), :] + w[1:2] * ue_ref[pl.ds(SUBLANES - 1, tm), :]
            + w[2:3] * u)
    gated_ref[...] = (b_gate * conv).astype(BF16)
    tail = u[tm - SUBLANES:, :]
    carry_ref[...] = tail
    tail_ref[...] = tail


def _ep_conv_step(accs, extra, outs, col0):
    b_gate, c_gate, z = accs
    s0_ref, s1_ref, w_ref = extra
    u = c_gate * z
    w = w_ref[...]
    conv = w[0:1] * s0_ref[...] + w[1:2] * s1_ref[...] + w[2:3] * u
    outs[0][...] = (b_gate * conv).astype(BF16)
    outs[1][...] = u


def _ep_rotary(accs, extra, outs, col0):
    cos, sin_lo, sin_hi = (r[0] for r in extra)
    acc = accs[0]
    half = ROPE_DIM // 2
    for c in range(acc.shape[1] // LANES):
        xc = acc[:, c * LANES:(c + 1) * LANES]
        outs[0][:, col0 + c * LANES:col0 + (c + 1) * LANES] = (
            xc * cos + pltpu.roll(xc, LANES - half, 1) * sin_lo + pltpu.roll(xc, half, 1) * sin_hi)


def _proj_body(*refs, n_w, n_extra, n_extra_s, n_out, n_out_s, has_sample, epilogue, epilogue_s, col_chunk):
    refs = list(refs)
    take = lambda n: [refs.pop(0) for _ in range(n)]
    (x_ref,) = take(1)
    xs_ref, gs_ref = take(2) if has_sample else (None, None)
    w_refs = take(n_w)
    extra = take(n_extra)
    extra_s = take(n_extra_s)
    outs = take(n_out)
    outs_s = take(n_out_s)
    wb_ref, *scratch = refs

    @pl.when(pl.program_id(1) == 0)
    def _():
        for k, w_ref in enumerate(w_refs):
            _cast_weight(w_ref, wb_ref, k)
        if has_sample:
            xs = xs_ref[...]
            ms = jnp.mean(xs * xs, axis=-1, keepdims=True)
            xsn = (xs * lax.rsqrt(ms + EPS) * gs_ref[...]).astype(BF16)
            epilogue_s([jnp.dot(xsn, wb_ref[k], preferred_element_type=F32) for k in range(n_w)],
                       extra_s, outs_s, 0)

    x = x_ref[...]
    for c0 in range(0, wb_ref.shape[2], col_chunk):
        accs = [jnp.dot(x, wb_ref[k, :, c0:c0 + col_chunk], preferred_element_type=F32) for k in range(n_w)]
        epilogue(accs, extra, outs, c0, *([scratch] if scratch else []))


def _proj(xn, xs, gain, w_stack, layer, *, w_col_blocks, n_col_blocks, epilogue, out_dtypes, name,
          tm=1024, tn=512, col_chunk=None, extra=(), extra_specs=(), extra_s=(), extra_s_specs=(),
          epilogue_s=None, out_s_dtypes=None, more_outs=(), scratch=()):
    m, k = xn.shape
    tm = min(tm, m)
    has_sample = xs is not None
    assert m % tm == 0 and w_stack.shape[1] == k and k % CAST_ROWS == 0
    out_s_dtypes = (out_dtypes if out_s_dtypes is None else out_s_dtypes) if has_sample else ()
    n_total = n_col_blocks * tn
    args = [xn]
    in_specs = [pl.BlockSpec((tm, k), lambda j, i: (i, 0))]
    if has_sample:
        rs = xs.shape[0]
        args += [xs, gain.reshape(1, k)]
        in_specs += [pl.BlockSpec((rs, k), lambda j, i: (0, 0)), pl.BlockSpec((1, k), lambda j, i: (0, 0))]
    for off in w_col_blocks:
        args.append(w_stack)
        in_specs.append(pl.BlockSpec((1, k, tn), lambda j, i, off=off: (layer, 0, j + off)))
    args += list(extra) + list(extra_s)
    in_specs += list(extra_specs) + list(extra_s_specs)
    out_shape = [jax.ShapeDtypeStruct((m, n_total), dt) for dt in out_dtypes] + [o[0] for o in more_outs]
    out_specs = [pl.BlockSpec((tm, tn), lambda j, i: (i, j)) for _ in out_dtypes] + [o[1] for o in more_outs]
    n_out = len(out_shape)
    if has_sample:
        out_shape += [jax.ShapeDtypeStruct((rs, n_total), dt) for dt in out_s_dtypes]
        out_specs += [pl.BlockSpec((rs, tn), lambda j, i: (0, j)) for _ in out_s_dtypes]
    res = pl.pallas_call(
        functools.partial(_proj_body, n_w=len(w_col_blocks), n_extra=len(extra), n_extra_s=len(extra_s),
                          n_out=n_out, n_out_s=len(out_s_dtypes), has_sample=has_sample,
                          epilogue=epilogue, epilogue_s=epilogue_s or epilogue, col_chunk=col_chunk or tn),
        grid=(n_col_blocks, m // tm),
        in_specs=in_specs,
        out_specs=out_specs,
        out_shape=out_shape,
        scratch_shapes=[pltpu.VMEM((len(w_col_blocks), k, tn), BF16)] + list(scratch),
        compiler_params=_cparams("arbitrary", "arbitrary"),
        name=name,
    )(*args)
    return (res[:n_out], res[n_out:]) if has_sample else (res, None)


def _res_proj_body(a_ref, as_ref, w_ref, r_ref, rs_ref, o_ref, os_ref, wb_ref, *, scale):
    @pl.when(pl.program_id(1) == 0)
    def _():
        _cast_weight(w_ref, wb_ref, 0)
        os_ref[...] = rs_ref[...] + scale * jnp.dot(as_ref[...], wb_ref[0], preferred_element_type=F32)

    o_ref[...] = r_ref[...] + scale * jnp.dot(a_ref[...], wb_ref[0], preferred_element_type=F32)


def _res_proj(a, a_s, w_stack, layer, res, res_s, scale, *, name, tm=512, tn=512):
    m, k = a.shape
    rs = a_s.shape[0]
    n = w_stack.shape[2]
    assert m % tm == 0 and n % tn == 0 and k % CAST_ROWS == 0 and res.shape == (m, n)
    return pl.pallas_call(
        functools.partial(_res_proj_body, scale=scale),
        grid=(n // tn, m // tm),
        in_specs=[pl.BlockSpec((tm, k), lambda j, i: (i, 0)),
                  pl.BlockSpec((rs, k), lambda j, i: (0, 0)),
                  pl.BlockSpec((1, k, tn), lambda j, i: (layer, 0, j)),
                  pl.BlockSpec((tm, tn), lambda j, i: (i, j)),
                  pl.BlockSpec((rs, tn), lambda j, i: (0, j))],
        out_specs=[pl.BlockSpec((tm, tn), lambda j, i: (i, j)),
                   pl.BlockSpec((rs, tn), lambda j, i: (0, j))],
        out_shape=[jax.ShapeDtypeStruct((m, n), F32), jax.ShapeDtypeStruct((rs, n), F32)],
        scratch_shapes=[pltpu.VMEM((1, k, tn), BF16)],
        compiler_params=_cparams("arbitrary", "arbitrary"),
        name=name,
    )(a, a_s, w_stack, res, res_s)


def _res_proj_norm_body(a_ref, as_ref, w_ref, r_ref, rs_ref, g_ref, o_ref, os_ref, xn_ref, wb_ref, *, scale):
    @pl.when(pl.program_id(0) == 0)
    def _():
        _cast_weight(w_ref, wb_ref, 0)
        os_ref[...] = rs_ref[...] + scale * jnp.dot(as_ref[...], wb_ref[0], preferred_element_type=F32)

    x = r_ref[...] + scale * jnp.dot(a_ref[...], wb_ref[0], preferred_element_type=F32)
    o_ref[...] = x
    ms = jnp.mean(x * x, axis=-1, keepdims=True)
    xn_ref[...] = (x * lax.rsqrt(ms + EPS) * g_ref[...]).astype(BF16)


def _res_proj_norm(a, a_s, w_stack, layer, res, res_s, scale, next_gain, *, name, tm=512):
    m, k = a.shape
    rs = a_s.shape[0]
    n = w_stack.shape[2]
    assert m % tm == 0 and k % CAST_ROWS == 0 and res.shape == (m, n)
    return pl.pallas_call(
        functools.partial(_res_proj_norm_body, scale=scale),
        grid=(m // tm,),
        in_specs=[pl.BlockSpec((tm, k), lambda i: (i, 0)),
                  pl.BlockSpec((rs, k), lambda i: (0, 0)),
                  pl.BlockSpec((1, k, n), lambda i: (layer, 0, 0), pipeline_mode=pl.Buffered(1)),
                  pl.BlockSpec((tm, n), lambda i: (i, 0)),
                  pl.BlockSpec((rs, n), lambda i: (0, 0)),
                  pl.BlockSpec((1, n), lambda i: (0, 0))],
        out_specs=[pl.BlockSpec((tm, n), lambda i: (i, 0)),
                   pl.BlockSpec((rs, n), lambda i: (0, 0)),
                   pl.BlockSpec((tm, n), lambda i: (i, 0))],
        out_shape=[jax.ShapeDtypeStruct((m, n), F32), jax.ShapeDtypeStruct((rs, n), F32),
                   jax.ShapeDtypeStruct((m, n), BF16)],
        scratch_shapes=[pltpu.VMEM((1, k, n), BF16)],
        compiler_params=_cparams("arbitrary"),
        name=name,
    )(a, a_s, w_stack, res, res_s, next_gain.reshape(1, n))


def _hgrn_seq_body(q_ref, z_ref, v_ref, g_ref, lb_ref, gn_ref, tri_ref, y_ref, st_out_ref,
                   st_ref, qs_ref, ks_ref, cum_ref, o_ref, *, hb):
    @pl.when(pl.program_id(2) == 0)
    def _():
        st_ref[...] = jnp.zeros_like(st_ref)

    tt = q_ref.shape[0]
    lb = lb_ref[...]
    z = z_ref[...]
    forget = lb + (1.0 - lb) * _sigmoid(z)
    ks_ref[...] = (1.0 - lb) * _sigmoid(-z)
    q = q_ref[...]
    qs_ref[...] = q * _sigmoid(q)
    cum_ref[...] = jnp.dot(tri_ref[...], jnp.log(forget), preferred_element_type=F32,
                           precision=lax.Precision.HIGHEST)

    t_idx = lax.broadcasted_iota(jnp.int32, (GLA_BLOCK, 1), 0)
    half = GLA_BLOCK // 2

    def block(j, carry):
        rows = pl.ds(pl.multiple_of(j * GLA_BLOCK, GLA_BLOCK), GLA_BLOCK)
        for h in range(hb):
            cols = slice(h * HG_DK, (h + 1) * HG_DK)
            cb = cum_ref[rows, cols]
            qb = qs_ref[rows, cols]
            kb = ks_ref[rows, cols]
            vb = v_ref[rows, cols]
            last = cb[GLA_BLOCK - 1:GLA_BLOCK, :]
            st = st_ref[h]
            o = lax.dot_general((qb * jnp.exp(cb)).astype(BF16), st.astype(BF16), NT_DIMS,
                                preferred_element_type=F32)
            o_lo = jnp.zeros((half, LANES), F32)
            o_hi = jnp.zeros((half, LANES), F32)
            for s in range(GLA_BLOCK):
                cs = cb[s:s + 1, :]
                qk_hi = qb[half:, :] * kb[s:s + 1, :]
                g_hi = jnp.where(t_idx[half:] >= s, jnp.exp(cb[half:, :] - cs), 0.0) * qk_hi
                o_hi = o_hi + jnp.sum(g_hi, axis=-1, keepdims=True) * vb[s:s + 1, :]
                if s < half:
                    qk_lo = qb[:half, :] * kb[s:s + 1, :]
                    g_lo = jnp.where(t_idx[:half] >= s, jnp.exp(cb[:half, :] - cs), 0.0) * qk_lo
                    o_lo = o_lo + jnp.sum(g_lo, axis=-1, keepdims=True) * vb[s:s + 1, :]
            o_ref[rows, cols] = o + jnp.concatenate([o_lo, o_hi], axis=0)
            kd = kb * jnp.exp(last - cb)
            upd = lax.dot_general(vb.astype(BF16), kd.astype(BF16), TN_DIMS, preferred_element_type=F32)
            st_ref[h] = st * jnp.exp(last) + upd
        return carry

    lax.fori_loop(0, tt // GLA_BLOCK, block, 0)

    for h in range(hb):
        cols = slice(h * HG_DK, (h + 1) * HG_DK)
        o = o_ref[:, cols]
        ms = jnp.mean(o * o, axis=-1, keepdims=True)
        g = g_ref[:, cols]
        y_ref[:, cols] = (o * lax.rsqrt(ms + EPS) * gn_ref[0, h:h + 1, :] * (g * _sigmoid(g))).astype(BF16)

    @pl.when(pl.program_id(2) == pl.num_programs(2) - 1)
    def _():
        st_out_ref[0] = st_ref[...]


def _hgrn_seq(proj, lower_bound, norm_gain, *, batch, seq_len, tt=256, hb=4):
    m, n4 = proj.shape
    d = n4 // 4
    heads = d // HG_DK
    hg = heads // hb
    nt = seq_len // tt
    wcols = hb * HG_DK
    tri = (jnp.arange(tt)[:, None] >= jnp.arange(tt)[None, :]) & (
        jnp.arange(tt)[:, None] // GLA_BLOCK == jnp.arange(tt)[None, :] // GLA_BLOCK)
    tri = tri.astype(F32)

    def col(sec):
        return pl.BlockSpec((tt, wcols), lambda b, h, t, sec=sec: (b * nt + t, sec * hg + h))

    y, st = pl.pallas_call(
        functools.partial(_hgrn_seq_body, hb=hb),
        grid=(batch, hg, nt),
        in_specs=[col(0), col(1), col(2), col(3),
                  pl.BlockSpec((1, wcols), lambda b, h, t: (0, h)),
                  pl.BlockSpec((1, hb, HG_DK), lambda b, h, t: (h, 0, 0)),
                  pl.BlockSpec((tt, tt), lambda b, h, t: (0, 0))],
        out_specs=[pl.BlockSpec((tt, wcols), lambda b, h, t: (b * nt + t, h)),
                   pl.BlockSpec((1, hb, HG_DK, HG_DK), lambda b, h, t: (b, h, 0, 0))],
        out_shape=[jax.ShapeDtypeStruct((m, d), BF16),
                   jax.ShapeDtypeStruct((batch, heads, HG_DK, HG_DK), F32)],
        scratch_shapes=[pltpu.VMEM((hb, HG_DK, HG_DK), F32), pltpu.VMEM((tt, wcols), F32),
                        pltpu.VMEM((tt, wcols), F32), pltpu.VMEM((tt, wcols), F32),
                        pltpu.VMEM((tt, wcols), F32)],
        compiler_params=_cparams("parallel", "parallel", "arbitrary"),
        name="hgrn_seq",
    )(proj, proj, proj, proj, lower_bound.reshape(1, d), norm_gain.reshape(hg, hb, HG_DK), tri)
    return y, st


def _hgrn_step_body(q_ref, z_ref, v_ref, g_ref, lb_ref, gn_ref, s_ref, y_ref, s_out_ref):
    heads = s_ref.shape[1]
    for h in range(heads):
        lb = lb_ref[h]
        z = z_ref[0, h]
        forget = lb + (1.0 - lb) * _sigmoid(z)
        k = (1.0 - lb) * _sigmoid(-z)
        q = q_ref[0, h]
        q = q * _sigmoid(q)
        s_new = forget * s_ref[0, h] + k * v_ref[0, h]
        s_out_ref[0, h] = s_new
        o = jnp.sum(q * s_new, axis=0, keepdims=True)
        ms = jnp.mean(o * o, axis=-1, keepdims=True)
        g = g_ref[0, h]
        y_ref[0, h] = o * lax.rsqrt(ms + EPS) * gn_ref[h] * (g * _sigmoid(g))


def _hgrn_step(proj, lower_bound, norm_gain, state):
    bsz, heads, dk, dv = state.shape
    d = heads * dk
    q, z, v, g = (proj[:, s * d:(s + 1) * d] for s in range(4))
    col = lambda a: a.reshape(bsz, heads, dk, 1)
    row = lambda a: a.reshape(bsz, heads, 1, dv)
    col_spec = pl.BlockSpec((1, heads, dk, 1), lambda b: (b, 0, 0, 0))
    row_spec = pl.BlockSpec((1, heads, 1, dv), lambda b: (b, 0, 0, 0))
    st_spec = pl.BlockSpec((1, heads, dk, dv), lambda b: (b, 0, 0, 0))
    y, s_new = pl.pallas_call(
        _hgrn_step_body,
        grid=(bsz,),
        in_specs=[col_spec, col_spec, row_spec, row_spec,
                  pl.BlockSpec((heads, dk, 1), lambda b: (0, 0, 0)),
                  pl.BlockSpec((heads, 1, dv), lambda b: (0, 0, 0)),
                  st_spec],
        out_specs=[row_spec, st_spec],
        out_shape=[jax.ShapeDtypeStruct((bsz, heads, 1, dv), F32),
                   jax.ShapeDtypeStruct(state.shape, F32)],
        compiler_params=_cparams("parallel"),
        name="hgrn_step",
    )(col(q), col(z), row(v), row(g), lower_bound.reshape(heads, dk, 1),
      norm_gain.reshape(heads, 1, dv), state)
    return y.reshape(bsz, d), s_new


def _band_bias(n, has_prev):
    qi = jnp.arange(n)[:, None]
    ki = jnp.arange(2 * n if has_prev else n)[None, :]
    visible = ((ki >= qi) & (ki <= qi + n)) if has_prev else (ki <= qi)
    return jnp.where(visible, 0.0, NEG_BIG).astype(F32)


def _band_block(q, k, v, bias, scale):
    s = lax.dot_general(q.astype(BF16), k.astype(BF16), NT_DIMS, preferred_element_type=F32) * scale + bias
    m = jnp.max(s, axis=-1, keepdims=True)
    p = jnp.exp(s - m)
    l = jnp.sum(p, axis=-1, keepdims=True)
    acc = jnp.dot(p.astype(BF16), v.astype(BF16), preferred_element_type=F32)
    return acc / l, m + jnp.log(l)


def _dil_attn_seq_body(*refs, seq_len):
    qkv = refs[:3 * N_GROUPS]
    bias_first_ref, bias_ref, y_ref, og_ref, lse_ref = refs[3 * N_GROUPS:]
    scale = ATT_DH ** -0.5
    for g, (win, dil) in enumerate(DILATED_GROUPS):
        nk = win // dil
        q_ref, k_ref, v_ref = qkv[3 * g:3 * g + 3]
        for r in range(dil):
            for n in range(seq_len // dil // nk):
                def rows(first_blk, n_blk):
                    start, size = r + dil * nk * first_blk, nk * n_blk
                    return pl.ds(start, size, stride=dil) if dil > 1 else pl.ds(start, size)
                cur = rows(n, 1)
                keys = rows(n - 1, 2) if n else cur
                bias = bias_ref[...] if n else bias_first_ref[...]
                o, lse = _band_block(q_ref[0, cur, :], k_ref[0, keys, :], v_ref[0, keys, :], bias, scale)
                og_ref[g, cur, :] = o
                lse_ref[g, cur, :] = jnp.broadcast_to(lse, (nk, LANES))
    lse = [lse_ref[g] for g in range(N_GROUPS)]
    m = jnp.maximum(jnp.maximum(lse[0], lse[1]), lse[2])
    w = [jnp.exp(x - m) for x in lse]
    tot = w[0] + w[1] + w[2]
    y_ref[0] = ((w[0] * og_ref[0] + w[1] * og_ref[1] + w[2] * og_ref[2]) / tot).astype(BF16)


def _dil_attn_seq(qkv, *, batch, seq_len, heads):
    in_specs = []
    for g in range(N_GROUPS):
        for c in range(3):
            in_specs.append(pl.BlockSpec((1, seq_len, ATT_DH),
                                         lambda b, h, s=(g * 3 + c): (b, 0, s * heads + h)))
    nk = DILATED_GROUPS[0][0] // DILATED_GROUPS[0][1]
    assert all(win // dil == nk for win, dil in DILATED_GROUPS)
    in_specs += [pl.BlockSpec((nk, nk), lambda b, h: (0, 0)), pl.BlockSpec((nk, 2 * nk), lambda b, h: (0, 0))]
    return pl.pallas_call(
        functools.partial(_dil_attn_seq_body, seq_len=seq_len),
        grid=(batch, heads),
        in_specs=in_specs,
        out_specs=pl.BlockSpec((1, seq_len, ATT_DH), lambda b, h: (b, 0, h)),
        out_shape=jax.ShapeDtypeStruct((batch, seq_len, heads * ATT_DH), BF16),
        scratch_shapes=[pltpu.VMEM((N_GROUPS, seq_len, ATT_DH), F32),
                        pltpu.VMEM((N_GROUPS, seq_len, LANES), F32)],
        compiler_params=_cparams("parallel", "parallel"),
        name="dil_attn_seq",
    )(*([qkv] * (3 * N_GROUPS)), _band_bias(nk, False), _band_bias(nk, True))


def _dil_attn_step_body(*refs, heads):
    new = refs[:3 * N_GROUPS]
    caches = refs[3 * N_GROUPS:5 * N_GROUPS]
    y_ref = refs[5 * N_GROUPS]
    scale = ATT_DH ** -0.5
    for h in range(heads):
        cols = slice(h * ATT_DH, (h + 1) * ATT_DH)
        outs, lses = [], []
        for g in range(N_GROUPS):
            q = new[3 * g][0, 0][:, cols]
            k_new = new[3 * g + 1][0, 0][:, cols]
            v_new = new[3 * g + 2][0, 0][:, cols]
            k_c = caches[2 * g][:, h, :]
            v_c = caches[2 * g + 1][:, h, :]
            q8 = jnp.broadcast_to(q, (SUBLANES, ATT_DH)).astype(BF16)
            s_c = lax.dot_general(q8, k_c.astype(BF16), NT_DIMS, preferred_element_type=F32)[0:1] * scale
            s_n = jnp.sum(q.astype(BF16).astype(F32) * k_new.astype(BF16).astype(F32),
                          axis=-1, keepdims=True) * scale
            m = jnp.maximum(jnp.max(s_c, axis=-1, keepdims=True), s_n)
            p_c = jnp.exp(s_c - m)
            p_n = jnp.exp(s_n - m)
            l = jnp.sum(p_c, axis=-1, keepdims=True) + p_n
            p8 = jnp.broadcast_to(p_c, (SUBLANES, p_c.shape[1])).astype(BF16)
            acc = jnp.dot(p8, v_c.astype(BF16), preferred_element_type=F32)[0:1]
            acc = acc + p_n.astype(BF16).astype(F32) * v_new.astype(BF16).astype(F32)
            outs.append(acc / l)
            lses.append(m + jnp.log(l))
        m = jnp.maximum(jnp.maximum(lses[0], lses[1]), lses[2])
        w = [jnp.exp(x - m) for x in lses]
        y_ref[0, :, cols] = (w[0] * outs[0] + w[1] * outs[1] + w[2] * outs[2]) / (w[0] + w[1] + w[2])


def _dil_attn_step(qkv, caches, layer, *, heads):
    bsz = qkv.shape[0]
    hd = heads * ATT_DH
    new = qkv.reshape(bsz, 3 * N_GROUPS, 1, hd)
    args, in_specs = [], []
    for s in range(3 * N_GROUPS):
        args.append(new)
        in_specs.append(pl.BlockSpec((1, 1, 1, hd), lambda b, s=s: (b, s, 0, 0)))
    for g, (win, dil) in enumerate(DILATED_GROUPS):
        nk = win // dil
        for buf in caches[g]:
            assert buf.shape[1:] == (bsz, win, heads, ATT_DH)
            args.append(buf.reshape(buf.shape[0], bsz, nk, dil, heads, ATT_DH))
            in_specs.append(pl.BlockSpec((None, None, nk, None, heads, ATT_DH),
                                         lambda b: (layer, b, 0, 0, 0, 0)))
    y = pl.pallas_call(
        functools.partial(_dil_attn_step_body, heads=heads),
        grid=(bsz,),
        in_specs=in_specs,
        out_specs=pl.BlockSpec((1, 1, hd), lambda b: (b, 0, 0)),
        out_shape=jax.ShapeDtypeStruct((bsz, 1, hd), F32),
        compiler_params=_cparams("parallel"),
        name="dil_attn_step",
    )(*args)
    return y.reshape(bsz, hd)


def _xattn_body(q_ref, k_ref, v_ref, o_ref, *, heads, pad_rows, head_axis):
    scale = XA_DH ** -0.5
    for h in range(heads):
        cols = slice(h * XA_DH, (h + 1) * XA_DH)
        k = k_ref[:, h, :] if head_axis else k_ref[0][:, cols]
        v = v_ref[:, h, :] if head_axis else v_ref[0][:, cols]
        q = q_ref[0][:, cols]
        rows = q.shape[0]
        if pad_rows:
            q = jnp.broadcast_to(q, (pad_rows, XA_DH))
        s = lax.dot_general(q.astype(BF16), k.astype(BF16), NT_DIMS, preferred_element_type=F32) * scale
        m = jnp.max(s, axis=-1, keepdims=True)
        p = jnp.exp(s - m)
        l = jnp.sum(p, axis=-1, keepdims=True)
        o = jnp.dot(p.astype(BF16), v.astype(BF16), preferred_element_type=F32) / l
        o_ref[0, :, cols] = o[0:rows].astype(o_ref.dtype)


def _xattn(q, mem_k, mem_v, k_block, v_block, *, tq, out_dtype, name, layer=None):
    bsz, t, hd = q.shape
    heads = hd // XA_DH
    if layer is None:
        mem = mem_k.shape[1]
        mem_specs = [pl.BlockSpec((1, mem, hd), lambda b, i: (b, 0, k_block)),
                     pl.BlockSpec((1, mem, hd), lambda b, i: (b, 0, v_block))]
    else:
        mem = mem_k.shape[2]
        assert mem_k.shape[1:] == (bsz, mem, heads, XA_DH)
        mem_specs = [pl.BlockSpec((None, None, mem, heads, XA_DH), lambda b, i: (layer, b, 0, 0, 0))] * 2
    return pl.pallas_call(
        functools.partial(_xattn_body, heads=heads, pad_rows=SUBLANES if tq < SUBLANES else 0,
                          head_axis=layer is not None),
        grid=(bsz, t // tq),
        in_specs=[pl.BlockSpec((1, tq, hd), lambda b, i: (b, i, 0))] + mem_specs,
        out_specs=pl.BlockSpec((1, tq, hd), lambda b, i: (b, i, 0)),
        out_shape=jax.ShapeDtypeStruct((bsz, t, hd), out_dtype),
        compiler_params=_cparams("parallel", "parallel"),
        name=name,
    )(q, mem_k, mem_v)


TN_COLS = 512


def _rope_tables(pos):
    rows = pos.shape[0]
    half = ROPE_DIM // 2
    inv_freq = ROPE_THETA ** (-jnp.arange(half, dtype=F32) * 2.0 / ROPE_DIM)
    ang = pos.astype(F32)[:, None] * inv_freq[None, :]
    cos, sin = jnp.cos(ang), jnp.sin(ang)
    rest = jnp.zeros((rows, LANES - ROPE_DIM), F32)
    zh = jnp.zeros((rows, half), F32)
    c = jnp.concatenate([cos, cos, rest + 1.0], axis=1)
    s_lo = jnp.concatenate([-sin, zh, rest], axis=1)
    s_hi = jnp.concatenate([zh, sin, rest], axis=1)
    one, zero = jnp.ones((rows, LANES), F32), jnp.zeros((rows, LANES), F32)
    return jnp.stack([c, one]), jnp.stack([s_lo, zero]), jnp.stack([s_hi, zero])


def _pad_rows(a, rows):
    return jnp.pad(a, ((0, rows - a.shape[0]), (0, 0)))


ROWS_S = 16
TM_ROWS = 512
TN_WIDE = 1024


def _ffn(xp, xs, xn, gain, w_gu, w_down, layer, tag):
    nf = w_down.shape[1] // TN_COLS
    if xn is None:
        xn = _rms_norm(xp, gain, tm=TM_ROWS, out_dtype=BF16, name=tag + "_norm")
    (hid,), (hid_s,) = _proj(xn, xs, gain, w_gu, layer, w_col_blocks=(0, nf), n_col_blocks=nf,
                             epilogue=_ep_swiglu, out_dtypes=(BF16,), name=tag + "_gate_up")
    return _res_proj(hid, hid_s, w_down, layer, xp, xs, 0.5, name=tag + "_down")


def _conv_mixer(xp, xs, gain, w_in, w_conv, w_out, layer, state, next_gain, *, nbat, t, nbat_s):
    d = xp.shape[1]
    nb = d // TN_COLS
    tm = min(1024, t)
    tiles = t // tm
    xn = _rms_norm(xp, gain, tm=TM_ROWS, out_dtype=BF16, name="conv_norm")
    w_spec = pl.BlockSpec((CONV_W, TN_COLS), lambda j, i: (0, j))
    st_spec = pl.BlockSpec((ROWS_S, TN_COLS), lambda j, i: (0, j))
    tail_out = (jax.ShapeDtypeStruct((nbat * SUBLANES, d), F32),
                pl.BlockSpec((SUBLANES, TN_COLS), lambda j, i: (i // tiles, j)))
    (gated, tail), (gated_s, u_s) = _proj(
        xn, xs, gain, w_in, layer, w_col_blocks=(0, nb, 2 * nb), n_col_blocks=nb, name="conv_in", tm=tm,
        epilogue=functools.partial(_ep_conv_seq, tiles_per_seq=tiles), out_dtypes=(BF16,), more_outs=(tail_out,),
        epilogue_s=_ep_conv_step, out_s_dtypes=(BF16, F32),
        extra=(w_conv,), extra_specs=(w_spec,),
        extra_s=(_pad_rows(state[:, 0], ROWS_S), _pad_rows(state[:, 1], ROWS_S), w_conv),
        extra_s_specs=(st_spec, st_spec, w_spec),
        scratch=(pltpu.VMEM((SUBLANES, TN_COLS), F32), pltpu.VMEM((tm + SUBLANES, TN_COLS), F32)))
    state_p = tail.reshape(nbat, SUBLANES, d)[:, SUBLANES - (CONV_W - 1):]
    state_s = jnp.stack([state[:, 1], u_s[:nbat_s]], axis=1)
    xp, xs, xn = _res_proj_norm(gated, gated_s, w_out, layer, xp, xs, 1.0, next_gain, name="conv_out")
    return xp, xs, xn, state_p, state_s


def _hgrn_mixer(xp, xs, gain, w_in, lower_bound, norm_gain, w_out, layer, state, next_gain, *, nbat, t, nbat_s):
    xn = _rms_norm(xp, gain, tm=TM_ROWS, out_dtype=BF16, name="hgrn_norm")
    (proj,), (proj_s,) = _proj(xn, xs, gain, w_in, layer, w_col_blocks=(0,), n_col_blocks=w_in.shape[2] // TN_WIDE,
                               epilogue=_ep_plain, out_dtypes=(F32,), name="hgrn_in", tn=TN_WIDE)
    y, st = _hgrn_seq(proj, lower_bound, norm_gain, batch=nbat, seq_len=t)
    y_s, state_s = _hgrn_step(proj_s[:nbat_s], lower_bound, norm_gain, state)
    xp, xs, xn = _res_proj_norm(y, _pad_rows(y_s, ROWS_S).astype(BF16), w_out, layer, xp, xs, 1.0, next_gain,
                                name="hgrn_out")
    return xp, xs, xn, jnp.swapaxes(st, -1, -2), state_s


def _attn_mixer(xp, xs, gain, w_qkv, w_out, layer, caches, next_gain, *, nbat, t, nbat_s):
    d = xp.shape[1]
    heads = d // ATT_DH
    n_sec = 3 * N_GROUPS
    sec_blocks = d // TN_WIDE
    tm = min(1024, t)
    tiles = t // tm
    sel = lambda j: ((j // sec_blocks) % 3) // 2
    tab_spec = pl.BlockSpec((1, tm, LANES), lambda j, i: (sel(j), i % tiles, 0))
    tab_s_spec = pl.BlockSpec((1, ROWS_S, LANES), lambda j, i: (sel(j), 0, 0))
    xn = _rms_norm(xp, gain, tm=TM_ROWS, out_dtype=BF16, name="attn_norm")
    (qkv,), (qkv_s,) = _proj(
        xn, xs, gain, w_qkv, layer, w_col_blocks=(0,), n_col_blocks=n_sec * sec_blocks,
        epilogue=_ep_rotary, out_dtypes=(F32,), name="attn_qkv", tm=tm, tn=TN_WIDE, col_chunk=MXU_COLS,
        extra=_rope_tables(jnp.arange(t)), extra_specs=(tab_spec,) * 3,
        extra_s=_rope_tables(jnp.full((ROWS_S,), PAST_LEN, jnp.int32)), extra_s_specs=(tab_s_spec,) * 3)
    y = _dil_attn_seq(qkv.reshape(nbat, t, n_sec * d), batch=nbat, seq_len=t, heads=heads).reshape(nbat * t, d)
    y_s = _dil_attn_step(qkv_s[:nbat_s], caches, layer, heads=heads)
    q5 = qkv.reshape(nbat, t, N_GROUPS, 3, heads, ATT_DH)
    q5_s = qkv_s[:nbat_s].reshape(nbat_s, 1, N_GROUPS, 3, heads, ATT_DH)
    rows_p, rows_s = [], []
    for g, (win, _) in enumerate(DILATED_GROUPS):
        keep = min(win, t)
        rows_p += [q5[:, t - keep:, g, 1], q5[:, t - keep:, g, 2]]
        rows_s += [q5_s[:, :, g, 1], q5_s[:, :, g, 2]]
    xp, xs, xn = _res_proj_norm(y, _pad_rows(y_s, ROWS_S).astype(BF16), w_out, layer, xp, xs, 1.0, next_gain,
                                name="attn_out")
    return xp, xs, xn, rows_p, rows_s


def _cross_attention(xp, xs, xn, gain, w_q, w_o, layer, mem_kv, mem_k_s, mem_v_s, next_gain, *, nbat, t, nbat_s):
    hd = w_q.shape[2]
    (q,), (q_s,) = _proj(xn, xs, gain, w_q, layer, w_col_blocks=(0,), n_col_blocks=hd // TN_COLS,
                         epilogue=_ep_plain, out_dtypes=(F32,), name="xattn_q")
    o = _xattn(q.reshape(nbat, t, hd), mem_kv, mem_kv, 0, 1, tq=TM_ROWS, out_dtype=BF16, name="xattn_seq")
    o = o.reshape(nbat * t, hd)
    o_s = _xattn(q_s[:nbat_s].reshape(nbat_s, 1, hd), mem_k_s, mem_v_s, 0, 0, tq=1, out_dtype=F32,
                 name="xattn_step", layer=layer)
    o_s = _pad_rows(o_s.reshape(nbat_s, hd), ROWS_S).astype(BF16)
    return _res_proj_norm(o, o_s, w_o, layer, xp, xs, 1.0, next_gain, name="xattn_out")


def kernel(x_prompt, x_sample, state_conv, state_hgrn,
           cache_win_k0, cache_win_v0, cache_win_k1, cache_win_v1, cache_win_k2, cache_win_v2,
           cache_mem_k, cache_mem_v, mem_prompt,
           norm_ffn1, ffn1_w_gu, ffn1_w_down, norm_mix,
           conv_w_in, conv_w, conv_w_out,
           hgrn_w_in, hgrn_lb_logits, hgrn_norm, hgrn_w_out,
           attn_w_qkv, attn_w_out,
           norm_mem, xattn_w_kv, norm_xattn, xattn_w_q, xattn_w_o,
           norm_ffn2, ffn2_w_gu, ffn2_w_down, norm_final):
    batch, seq, d = x_prompt.shape
    dec_batch, dec_seq, _ = x_sample.shape
    assert dec_seq == 1 and dec_batch <= ROWS_S
    depth = norm_ffn1.shape[0]
    mem_len = mem_prompt.shape[1]
    xa_hd = xattn_w_q.shape[2]
    xa_heads = xa_hd // XA_DH
    win_cache = [(cache_win_k0, cache_win_v0), (cache_win_k1, cache_win_v1), (cache_win_k2, cache_win_v2)]
    sizes = dict(nbat=batch, t=seq, nbat_s=dec_batch)

    lb_p = jax.nn.softmax(hgrn_lb_logits.astype(F32), axis=0)
    lower_bounds = jnp.cumsum(lb_p, axis=0) - lb_p[0]

    xp = x_prompt.reshape(batch * seq, d)
    xs = _pad_rows(x_sample.reshape(dec_batch, d), ROWS_S)
    mem = mem_prompt.reshape(batch * mem_len, d)
    p_conv, p_hgrn, p_win, p_mk, p_mv, s_conv, s_hgrn, s_win = ([] for _ in range(8))
    for i in range(depth):
        j, kind = divmod(i, N_MIXERS)
        xp, xs = _ffn(xp, xs, None, norm_ffn1[i], ffn1_w_gu, ffn1_w_down, i, "ffn1")
        if kind == 0:
            xp, xs, xn, st_p, st_s = _conv_mixer(xp, xs, norm_mix[i], conv_w_in, conv_w[j], conv_w_out, j,
                                                 state_conv[j], norm_xattn[i], **sizes)
            p_conv.append(st_p)
            s_conv.append(st_s)
        elif kind == 1:
            xp, xs, xn, st_p, st_s = _hgrn_mixer(xp, xs, norm_mix[i], hgrn_w_in, lower_bounds[i], hgrn_norm[j],
                                                 hgrn_w_out, j, state_hgrn[j], norm_xattn[i], **sizes)
            p_hgrn.append(st_p)
            s_hgrn.append(st_s)
        else:
            xp, xs, xn, rows_p, rows_s = _attn_mixer(xp, xs, norm_mix[i], attn_w_qkv, attn_w_out, j, win_cache,
                                                     norm_xattn[i], **sizes)
            p_win.append(rows_p)
            s_win.append(rows_s)
        mem_n = _rms_norm(mem, norm_mem[i], tm=TM_ROWS, out_dtype=BF16, name="mem_norm")
        (kv,), _ = _proj(mem_n, None, None, xattn_w_kv, i, w_col_blocks=(0,), n_col_blocks=2 * xa_hd // TN_COLS,
                         epilogue=_ep_plain, out_dtypes=(F32,), name="mem_kv")
        kv = kv.reshape(batch, mem_len, 2 * xa_hd)
        p_mk.append(kv[:, :, :xa_hd].reshape(batch, mem_len, xa_heads, XA_DH))
        p_mv.append(kv[:, :, xa_hd:].reshape(batch, mem_len, xa_heads, XA_DH))
        xp, xs, xn = _cross_attention(xp, xs, xn, norm_xattn[i], xattn_w_q, xattn_w_o, i, kv, cache_mem_k, cache_mem_v,
                                      norm_ffn2[i], **sizes)
        xp, xs = _ffn(xp, xs, xn, norm_ffn2[i], ffn2_w_gu, ffn2_w_down, i, "ffn2")
    y_prompt = _rms_norm(xp, norm_final, tm=TM_ROWS, out_dtype=F32, name="final_norm").reshape(batch, seq, d)
    y_sample = _rms_norm(xs, norm_final, tm=ROWS_S, out_dtype=F32, name="final_norm_s")[:dec_batch]
    y_sample = y_sample.reshape(dec_batch, 1, d)

    stack_win = lambda rows: [jnp.stack([r[g] for r in rows]) for g in range(2 * N_GROUPS)]
    return (y_prompt, y_sample,
            jnp.stack(p_conv), jnp.stack(p_hgrn), *stack_win(p_win), jnp.stack(p_mk), jnp.stack(p_mv),
            jnp.stack(s_conv), jnp.stack(s_hgrn), *stack_win(s_win))
```

```python
import functools

import jax
import jax.numpy as jnp
from jax import lax
from jax.experimental import pallas as pl
from jax.experimental.pallas import tpu as pltpu

F32 = jnp.float32
BF16 = jnp.bfloat16
EPS = 1e-6
LANES = 128
SUBLANES = 8
MXU_COLS = 256
VMEM_LIMIT_BYTES = 56 << 20
NEG_BIG = -1e30

N_MIXERS = 3
CONV_W = 3
HG_DK = 128
ATT_DH = 128
DILATED_GROUPS = ((128, 1), (512, 4), (2048, 16))
N_GROUPS = 3
ROPE_DIM = ATT_DH // 4
ROPE_THETA = 500000.0
XA_DH = 128
GLA_BLOCK = 16
PAST_LEN = 16384

NT_DIMS = (((1,), (1,)), ((), ()))
TN_DIMS = (((0,), (0,)), ((), ()))


def _cparams(*semantics):
    return pltpu.CompilerParams(dimension_semantics=semantics, vmem_limit_bytes=VMEM_LIMIT_BYTES)


def _sigmoid(x):
    return 1.0 / (1.0 + jnp.exp(-x))


def _rms_body(x_ref, g_ref, o_ref):
    x = x_ref[...]
    ms = jnp.mean(x * x, axis=-1, keepdims=True)
    o_ref[...] = (x * lax.rsqrt(ms + EPS) * g_ref[...]).astype(o_ref.dtype)


def _rms_norm(x, gain, *, tm, out_dtype, name):
    m, k = x.shape
    return pl.pallas_call(
        _rms_body,
        grid=(m // tm,),
        in_specs=[pl.BlockSpec((tm, k), lambda i: (i, 0)), pl.BlockSpec((1, k), lambda i: (0, 0))],
        out_specs=pl.BlockSpec((tm, k), lambda i: (i, 0)),
        out_shape=jax.ShapeDtypeStruct((m, k), out_dtype),
        compiler_params=_cparams("parallel"),
        name=name,
    )(x, gain.reshape(1, k))


CAST_ROWS = 256


def _cast_weight(w_ref, wb_ref, slot):
    def body(c, carry):
        rows = pl.ds(pl.multiple_of(c * CAST_ROWS, CAST_ROWS), CAST_ROWS)
        wb_ref[slot, rows, :] = w_ref[0, rows, :].astype(BF16)
        return carry
    lax.fori_loop(0, w_ref.shape[1] // CAST_ROWS, body, 0)


def _ep_plain(accs, extra, outs, col0):
    acc = accs[0]
    outs[0][:, col0:col0 + acc.shape[1]] = acc.astype(outs[0].dtype)


def _ep_swiglu(accs, extra, outs, col0):
    gate, up = accs
    outs[0][:, col0:col0 + gate.shape[1]] = (gate * _sigmoid(gate) * up).astype(BF16)


def _ep_conv_seq(accs, extra, outs, col0, scratch, *, tiles_per_seq):
    b_gate, c_gate, z = accs
    (w_ref,), (gated_ref, tail_ref), (carry_ref, ue_ref) = extra, outs, scratch
    tm = z.shape[0]
    u = c_gate * z
    first = (pl.program_id(1) % tiles_per_seq) == 0
    ue_ref[0:SUBLANES, :] = jnp.where(first, 0.0, carry_ref[...])
    ue_ref[SUBLANES:SUBLANES + tm, :] = u
    w = w_ref[...]
    conv = (w[0:1] * ue_ref[pl.ds(SUBLANES - 2, tm), :] + w[1:2] * ue_ref[pl.ds(SUBLANES - 1, tm), :]
            + w[2:3] * u)
    gated_ref[...] = (b_gate * conv).astype(BF16)
    tail = u[tm - SUBLANES:, :]
    carry_ref[...] = tail
    tail_ref[...] = tail


def _ep_conv_step(accs, extra, outs, col0):
    b_gate, c_gate, z = accs
    s0_ref, s1_ref, w_ref = extra
    u = c_gate * z
    w = w_ref[...]
    conv = w[0:1] * s0_ref[...] + w[1:2] * s1_ref[...] + w[2:3] * u
    outs[0][...] = (b_gate * conv).astype(BF16)
    outs[1][...] = u


def _ep_rotary(accs, extra, outs, col0):
    cos, sin_lo, sin_hi = (r[...] for r in extra)
    acc = accs[0]
    half = ROPE_DIM // 2
    for c in range(acc.shape[1] // LANES):
        xc = acc[:, c * LANES:(c + 1) * LANES]
        outs[0][:, col0 + c * LANES:col0 + (c + 1) * LANES] = (
            xc * cos + pltpu.roll(xc, LANES - half, 1) * sin_lo + pltpu.roll(xc, half, 1) * sin_hi)


def _proj_body(*refs, n_w, n_extra, n_extra_s, n_out, n_out_s, has_sample, epilogue, epilogue_s, col_chunk,
               alt_epilogue, alt_pred):
    refs = list(refs)
    take = lambda n: [refs.pop(0) for _ in range(n)]
    (x_ref,) = take(1)
    xs_ref, gs_ref = take(2) if has_sample else (None, None)
    w_refs = take(n_w)
    extra = take(n_extra)
    extra_s = take(n_extra_s)
    outs = take(n_out)
    outs_s = take(n_out_s)
    wb_ref, *scratch = refs

    @pl.when(pl.program_id(1) == 0)
    def _():
        for k, w_ref in enumerate(w_refs):
            _cast_weight(w_ref, wb_ref, k)
        if has_sample:
            xs = xs_ref[...]
            ms = jnp.mean(xs * xs, axis=-1, keepdims=True)
            xsn = (xs * lax.rsqrt(ms + EPS) * gs_ref[...]).astype(BF16)
            accs_s = [jnp.dot(xsn, wb_ref[k], preferred_element_type=F32) for k in range(n_w)]
            if alt_epilogue is None:
                epilogue_s(accs_s, extra_s, outs_s, 0)
            else:
                use_alt = alt_pred(pl.program_id(0))
                pl.when(use_alt)(lambda: alt_epilogue(accs_s, extra_s, outs_s, 0))
                pl.when(jnp.logical_not(use_alt))(lambda: epilogue_s(accs_s, extra_s, outs_s, 0))

    def tile(ep):
        x = x_ref[...]
        for c0 in range(0, wb_ref.shape[2], col_chunk):
            accs = [jnp.dot(x, wb_ref[k, :, c0:c0 + col_chunk], preferred_element_type=F32) for k in range(n_w)]
            ep(accs, extra, outs, c0, *([scratch] if scratch else []))

    if alt_epilogue is None:
        tile(epilogue)
    else:
        use_alt = alt_pred(pl.program_id(0))
        pl.when(use_alt)(lambda: tile(alt_epilogue))
        pl.when(jnp.logical_not(use_alt))(lambda: tile(epilogue))


def _proj(xn, xs, gain, w_stack, layer, *, w_col_blocks, n_col_blocks, epilogue, out_dtypes, name,
          tm=1024, tn=512, col_chunk=None, extra=(), extra_specs=(), extra_s=(), extra_s_specs=(),
          epilogue_s=None, out_s_dtypes=None, more_outs=(), scratch=(), alt_epilogue=None, alt_pred=None):
    m, k = xn.shape
    tm = min(tm, m)
    has_sample = xs is not None
    assert m % tm == 0 and w_stack.shape[1] == k and k % CAST_ROWS == 0
    out_s_dtypes = (out_dtypes if out_s_dtypes is None else out_s_dtypes) if has_sample else ()
    n_total = n_col_blocks * tn
    args = [xn]
    in_specs = [pl.BlockSpec((tm, k), lambda j, i: (i, 0))]
    if has_sample:
        rs = xs.shape[0]
        args += [xs, gain.reshape(1, k)]
        in_specs += [pl.BlockSpec((rs, k), lambda j, i: (0, 0)), pl.BlockSpec((1, k), lambda j, i: (0, 0))]
    for off in w_col_blocks:
        args.append(w_stack)
        in_specs.append(pl.BlockSpec((1, k, tn), lambda j, i, off=off: (layer, 0, j + off)))
    args += list(extra) + list(extra_s)
    in_specs += list(extra_specs) + list(extra_s_specs)
    out_shape = [jax.ShapeDtypeStruct((m, n_total), dt) for dt in out_dtypes] + [o[0] for o in more_outs]
    out_specs = [pl.BlockSpec((tm, tn), lambda j, i: (i, j)) for _ in out_dtypes] + [o[1] for o in more_outs]
    n_out = len(out_shape)
    if has_sample:
        out_shape += [jax.ShapeDtypeStruct((rs, n_total), dt) for dt in out_s_dtypes]
        out_specs += [pl.BlockSpec((rs, tn), lambda j, i: (0, j)) for _ in out_s_dtypes]
    res = pl.pallas_call(
        functools.partial(_proj_body, n_w=len(w_col_blocks), n_extra=len(extra), n_extra_s=len(extra_s),
                          n_out=n_out, n_out_s=len(out_s_dtypes), has_sample=has_sample,
                          epilogue=epilogue, epilogue_s=epilogue_s or epilogue, col_chunk=col_chunk or tn,
                          alt_epilogue=alt_epilogue, alt_pred=alt_pred),
        grid=(n_col_blocks, m // tm),
        in_specs=in_specs,
        out_specs=out_specs,
        out_shape=out_shape,
        scratch_shapes=[pltpu.VMEM((len(w_col_blocks), k, tn), BF16)] + list(scratch),
        compiler_params=_cparams("arbitrary", "arbitrary"),
        name=name,
    )(*args)
    return (res[:n_out], res[n_out:]) if has_sample else (res, None)


def _res_proj_body(a_ref, as_ref, w_ref, r_ref, rs_ref, o_ref, os_ref, wb_ref, *, scale):
    @pl.when(pl.program_id(1) == 0)
    def _():
        _cast_weight(w_ref, wb_ref, 0)
        os_ref[...] = rs_ref[...] + scale * jnp.dot(as_ref[...], wb_ref[0], preferred_element_type=F32)

    o_ref[...] = r_ref[...] + scale * jnp.dot(a_ref[...], wb_ref[0], preferred_element_type=F32)


def _res_proj(a, a_s, w_stack, layer, res, res_s, scale, *, name, tm=512, tn=512):
    m, k = a.shape
    rs = a_s.shape[0]
    n = w_stack.shape[2]
    assert m % tm == 0 and n % tn == 0 and k % CAST_ROWS == 0 and res.shape == (m, n)
    return pl.pallas_call(
        functools.partial(_res_proj_body, scale=scale),
        grid=(n // tn, m // tm),
        in_specs=[pl.BlockSpec((tm, k), lambda j, i: (i, 0)),
                  pl.BlockSpec((rs, k), lambda j, i: (0, 0)),
                  pl.BlockSpec((1, k, tn), lambda j, i: (layer, 0, j)),
                  pl.BlockSpec((tm, tn), lambda j, i: (i, j)),
                  pl.BlockSpec((rs, tn), lambda j, i: (0, j))],
        out_specs=[pl.BlockSpec((tm, tn), lambda j, i: (i, j)),
                   pl.BlockSpec((rs, tn), lambda j, i: (0, j))],
        out_shape=[jax.ShapeDtypeStruct((m, n), F32), jax.ShapeDtypeStruct((rs, n), F32)],
        scratch_shapes=[pltpu.VMEM((1, k, tn), BF16)],
        compiler_params=_cparams("arbitrary", "arbitrary"),
        name=name,
    )(a, a_s, w_stack, res, res_s)


def _res_proj_norm_body(a_ref, as_ref, w_ref, r_ref, rs_ref, g_ref, o_ref, os_ref, xn_ref, wb_ref, *, scale):
    @pl.when(pl.program_id(0) == 0)
    def _():
        _cast_weight(w_ref, wb_ref, 0)
        os_ref[...] = rs_ref[...] + scale * jnp.dot(as_ref[...], wb_ref[0], preferred_element_type=F32)

    x = r_ref[...] + scale * jnp.dot(a_ref[...], wb_ref[0], preferred_element_type=F32)
    o_ref[...] = x
    ms = jnp.mean(x * x, axis=-1, keepdims=True)
    xn_ref[...] = (x * lax.rsqrt(ms + EPS) * g_ref[...]).astype(BF16)


def _res_proj_norm(a, a_s, w_stack, layer, res, res_s, scale, next_gain, *, name, tm=512):
    m, k = a.shape
    rs = a_s.shape[0]
    n = w_stack.shape[2]
    assert m % tm == 0 and k % CAST_ROWS == 0 and res.shape == (m, n)
    return pl.pallas_call(
        functools.partial(_res_proj_norm_body, scale=scale),
        grid=(m // tm,),
        in_specs=[pl.BlockSpec((tm, k), lambda i: (i, 0)),
                  pl.BlockSpec((rs, k), lambda i: (0, 0)),
                  pl.BlockSpec((1, k, n), lambda i: (layer, 0, 0), pipeline_mode=pl.Buffered(1)),
                  pl.BlockSpec((tm, n), lambda i: (i, 0)),
                  pl.BlockSpec((rs, n), lambda i: (0, 0)),
                  pl.BlockSpec((1, n), lambda i: (0, 0))],
        out_specs=[pl.BlockSpec((tm, n), lambda i: (i, 0)),
                   pl.BlockSpec((rs, n), lambda i: (0, 0)),
                   pl.BlockSpec((tm, n), lambda i: (i, 0))],
        out_shape=[jax.ShapeDtypeStruct((m, n), F32), jax.ShapeDtypeStruct((rs, n), F32),
                   jax.ShapeDtypeStruct((m, n), BF16)],
        scratch_shapes=[pltpu.VMEM((1, k, n), BF16)],
        compiler_params=_cparams("arbitrary"),
        name=name,
    )(a, a_s, w_stack, res, res_s, next_gain.reshape(1, n))


def _hgrn_seq_body(q_ref, z_ref, v_ref, g_ref, lb_ref, gn_ref, tri_ref, y_ref, st_out_ref,
                   st_ref, qs_ref, ks_ref, cum_ref, o_ref, *, hb):
    @pl.when(pl.program_id(2) == 0)
    def _():
        st_ref[...] = jnp.zeros_like(st_ref)

    tt = q_ref.shape[0]
    lb = lb_ref[...]
    z = z_ref[...]
    forget = lb + (1.0 - lb) * _sigmoid(z)
    ks_ref[...] = (1.0 - lb) * _sigmoid(-z)
    q = q_ref[...]
    qs_ref[...] = q * _sigmoid(q)
    cum_ref[...] = jnp.dot(tri_ref[...], jnp.log(forget), preferred_element_type=F32,
                           precision=lax.Precision.HIGHEST)

    t_idx = lax.broadcasted_iota(jnp.int32, (GLA_BLOCK, 1), 0)
    half = GLA_BLOCK // 2

    def block(j, carry):
        rows = pl.ds(pl.multiple_of(j * GLA_BLOCK, GLA_BLOCK), GLA_BLOCK)
        for h in range(hb):
            cols = slice(h * HG_DK, (h + 1) * HG_DK)
            cb = cum_ref[rows, cols]
            qb = qs_ref[rows, cols]
            kb = ks_ref[rows, cols]
            vb = v_ref[rows, cols]
            last = cb[GLA_BLOCK - 1:GLA_BLOCK, :]
            st = st_ref[h]
            o = lax.dot_general((qb * jnp.exp(cb)).astype(BF16), st.astype(BF16), NT_DIMS,
                                preferred_element_type=F32)
            o_lo = jnp.zeros((half, LANES), F32)
            o_hi = jnp.zeros((half, LANES), F32)
            for s in range(GLA_BLOCK):
                cs = cb[s:s + 1, :]
                qk_hi = qb[half:, :] * kb[s:s + 1, :]
                g_hi = jnp.where(t_idx[half:] >= s, jnp.exp(cb[half:, :] - cs), 0.0) * qk_hi
                o_hi = o_hi + jnp.sum(g_hi, axis=-1, keepdims=True) * vb[s:s + 1, :]
                if s < half:
                    qk_lo = qb[:half, :] * kb[s:s + 1, :]
                    g_lo = jnp.where(t_idx[:half] >= s, jnp.exp(cb[:half, :] - cs), 0.0) * qk_lo
                    o_lo = o_lo + jnp.sum(g_lo, axis=-1, keepdims=True) * vb[s:s + 1, :]
            o_ref[rows, cols] = o + jnp.concatenate([o_lo, o_hi], axis=0)
            kd = kb * jnp.exp(last - cb)
            upd = lax.dot_general(vb.astype(BF16), kd.astype(BF16), TN_DIMS, preferred_element_type=F32)
            st_ref[h] = st * jnp.exp(last) + upd
        return carry

    lax.fori_loop(0, tt // GLA_BLOCK, block, 0)

    for h in range(hb):
        cols = slice(h * HG_DK, (h + 1) * HG_DK)
        o = o_ref[:, cols]
        ms = jnp.mean(o * o, axis=-1, keepdims=True)
        g = g_ref[:, cols]
        y_ref[:, cols] = (o * lax.rsqrt(ms + EPS) * gn_ref[0, h:h + 1, :] * (g * _sigmoid(g))).astype(BF16)

    @pl.when(pl.program_id(2) == pl.num_programs(2) - 1)
    def _():
        st_out_ref[0] = st_ref[...]


def _hgrn_seq(proj, lower_bound, norm_gain, *, batch, seq_len, tt=256, hb=8):
    m, n4 = proj.shape
    d = n4 // 4
    heads = d // HG_DK
    hg = heads // hb
    nt = seq_len // tt
    wcols = hb * HG_DK
    tri = (jnp.arange(tt)[:, None] >= jnp.arange(tt)[None, :]) & (
        jnp.arange(tt)[:, None] // GLA_BLOCK == jnp.arange(tt)[None, :] // GLA_BLOCK)
    tri = tri.astype(F32)

    def col(sec):
        return pl.BlockSpec((tt, wcols), lambda b, h, t, sec=sec: (b * nt + t, sec * hg + h))

    y, st = pl.pallas_call(
        functools.partial(_hgrn_seq_body, hb=hb),
        grid=(batch, hg, nt),
        in_specs=[col(0), col(1), col(2), col(3),
                  pl.BlockSpec((1, wcols), lambda b, h, t: (0, h)),
                  pl.BlockSpec((1, hb, HG_DK), lambda b, h, t: (h, 0, 0)),
                  pl.BlockSpec((tt, tt), lambda b, h, t: (0, 0))],
        out_specs=[pl.BlockSpec((tt, wcols), lambda b, h, t: (b * nt + t, h)),
                   pl.BlockSpec((1, hb, HG_DK, HG_DK), lambda b, h, t: (b, h, 0, 0))],
        out_shape=[jax.ShapeDtypeStruct((m, d), BF16),
                   jax.ShapeDtypeStruct((batch, heads, HG_DK, HG_DK), F32)],
        scratch_shapes=[pltpu.VMEM((hb, HG_DK, HG_DK), F32), pltpu.VMEM((tt, wcols), F32),
                        pltpu.VMEM((tt, wcols), F32), pltpu.VMEM((tt, wcols), F32),
                        pltpu.VMEM((tt, wcols), F32)],
        compiler_params=_cparams("parallel", "parallel", "arbitrary"),
        name="hgrn_seq",
    )(proj, proj, proj, proj, lower_bound.reshape(1, d), norm_gain.reshape(hg, hb, HG_DK), tri)
    return y, st


def _hgrn_step_body(q_ref, z_ref, v_ref, g_ref, lb_ref, gn_ref, s_ref, y_ref, s_out_ref):
    heads = s_ref.shape[1]
    for h in range(heads):
        lb = lb_ref[h]
        z = z_ref[0, h]
        forget = lb + (1.0 - lb) * _sigmoid(z)
        k = (1.0 - lb) * _sigmoid(-z)
        q = q_ref[0, h]
        q = q * _sigmoid(q)
        s_new = forget * s_ref[0, h] + k * v_ref[0, h]
        s_out_ref[0, h] = s_new
        o = jnp.sum(q * s_new, axis=0, keepdims=True)
        ms = jnp.mean(o * o, axis=-1, keepdims=True)
        g = g_ref[0, h]
        y_ref[0, h] = o * lax.rsqrt(ms + EPS) * gn_ref[h] * (g * _sigmoid(g))


def _hgrn_step(proj, lower_bound, norm_gain, state):
    bsz, heads, dk, dv = state.shape
    d = heads * dk
    q, z, v, g = (proj[:, s * d:(s + 1) * d] for s in range(4))
    col = lambda a: a.reshape(bsz, heads, dk, 1)
    row = lambda a: a.reshape(bsz, heads, 1, dv)
    col_spec = pl.BlockSpec((1, heads, dk, 1), lambda b: (b, 0, 0, 0))
    row_spec = pl.BlockSpec((1, heads, 1, dv), lambda b: (b, 0, 0, 0))
    st_spec = pl.BlockSpec((1, heads, dk, dv), lambda b: (b, 0, 0, 0))
    y, s_new = pl.pallas_call(
        _hgrn_step_body,
        grid=(bsz,),
        in_specs=[col_spec, col_spec, row_spec, row_spec,
                  pl.BlockSpec((heads, dk, 1), lambda b: (0, 0, 0)),
                  pl.BlockSpec((heads, 1, dv), lambda b: (0, 0, 0)),
                  st_spec],
        out_specs=[row_spec, st_spec],
        out_shape=[jax.ShapeDtypeStruct((bsz, heads, 1, dv), F32),
                   jax.ShapeDtypeStruct(state.shape, F32)],
        compiler_params=_cparams("parallel"),
        name="hgrn_step",
    )(col(q), col(z), row(v), row(g), lower_bound.reshape(heads, dk, 1),
      norm_gain.reshape(heads, 1, dv), state)
    return y.reshape(bsz, d), s_new


def _band_block(q, k, v, scale, has_prev):
    n, nkeys = q.shape[0], k.shape[0]
    qi = lax.broadcasted_iota(jnp.int32, (n, nkeys), 0)
    ki = lax.broadcasted_iota(jnp.int32, (n, nkeys), 1)
    s = lax.dot_general(q.astype(BF16), k.astype(BF16), NT_DIMS, preferred_element_type=F32) * scale
    visible = ((ki >= qi) & (ki <= qi + n)) if has_prev else (ki <= qi)
    s = jnp.where(visible, s, NEG_BIG)
    m = jnp.max(s, axis=-1, keepdims=True)
    p = jnp.exp(s - m)
    l = jnp.sum(p, axis=-1, keepdims=True)
    acc = jnp.dot(p.astype(BF16), v.astype(BF16), preferred_element_type=F32)
    return acc / l, m + jnp.log(l)


def _dil_attn_seq_body(*refs, seq_len):
    qkv = refs[:3 * N_GROUPS]
    y_ref, og_ref, lse_ref = refs[3 * N_GROUPS:]
    scale = ATT_DH ** -0.5
    for g, (win, dil) in enumerate(DILATED_GROUPS):
        nk = win // dil
        q_ref, k_ref, v_ref = qkv[3 * g:3 * g + 3]
        for r in range(dil):
            for n in range(seq_len // dil // nk):
                def rows(first_blk, n_blk):
                    start, size = r + dil * nk * first_blk, nk * n_blk
                    return pl.ds(start, size, stride=dil) if dil > 1 else pl.ds(start, size)
                cur = rows(n, 1)
                keys = rows(n - 1, 2) if n else cur
                o, lse = _band_block(q_ref[0, cur, :], k_ref[0, keys, :], v_ref[0, keys, :], scale, n > 0)
                og_ref[g, cur, :] = o
                lse_ref[g, cur, :] = jnp.broadcast_to(lse, (nk, LANES))
    lse = [lse_ref[g] for g in range(N_GROUPS)]
    m = jnp.maximum(jnp.maximum(lse[0], lse[1]), lse[2])
    w = [jnp.exp(x - m) for x in lse]
    tot = w[0] + w[1] + w[2]
    y_ref[0] = ((w[0] * og_ref[0] + w[1] * og_ref[1] + w[2] * og_ref[2]) / tot).astype(BF16)


def _dil_attn_seq(qkv, *, batch, seq_len, heads):
    in_specs = []
    for g in range(N_GROUPS):
        for c in range(3):
            in_specs.append(pl.BlockSpec((1, seq_len, ATT_DH),
                                         lambda b, h, s=(g * 3 + c): (b, 0, s * heads + h)))
    return pl.pallas_call(
        functools.partial(_dil_attn_seq_body, seq_len=seq_len),
        grid=(batch, heads),
        in_specs=in_specs,
        out_specs=pl.BlockSpec((1, seq_len, ATT_DH), lambda b, h: (b, 0, h)),
        out_shape=jax.ShapeDtypeStruct((batch, seq_len, heads * ATT_DH), BF16),
        scratch_shapes=[pltpu.VMEM((N_GROUPS, seq_len, ATT_DH), F32),
                        pltpu.VMEM((N_GROUPS, seq_len, LANES), F32)],
        compiler_params=_cparams("parallel", "parallel"),
        name="dil_attn_seq",
    )(*([qkv] * (3 * N_GROUPS)))


def _bf16_round(x):
    return x.astype(BF16).astype(F32)


def _step_attention(q, k_new, v_new, k_c, v_c, scale):
    qb = _bf16_round(q)
    s_c = jnp.sum(_bf16_round(k_c) * qb[None], axis=-1, keepdims=True) * scale
    m = jnp.max(s_c, axis=0)
    if k_new is not None:
        s_n = jnp.sum(qb * _bf16_round(k_new), axis=-1, keepdims=True) * scale
        m = jnp.maximum(m, s_n)
    p_c = jnp.exp(s_c - m[None])
    l = jnp.sum(p_c, axis=0)
    acc = jnp.sum(_bf16_round(p_c) * _bf16_round(v_c), axis=0)
    if k_new is not None:
        p_n = jnp.exp(s_n - m)
        l = l + p_n
        acc = acc + _bf16_round(p_n) * _bf16_round(v_new)
    return acc / l, m + jnp.log(l)


def _dil_attn_step_body(*refs):
    new = refs[:3 * N_GROUPS]
    caches = refs[3 * N_GROUPS:5 * N_GROUPS]
    y_ref = refs[5 * N_GROUPS]
    scale = ATT_DH ** -0.5
    outs, lses = [], []
    for g in range(N_GROUPS):
        o, lse = _step_attention(new[3 * g][0, 0], new[3 * g + 1][0, 0], new[3 * g + 2][0, 0],
                                 caches[2 * g][...], caches[2 * g + 1][...], scale)
        outs.append(o)
        lses.append(lse)
    m = jnp.maximum(jnp.maximum(lses[0], lses[1]), lses[2])
    w = [jnp.exp(x - m) for x in lses]
    y_ref[0] = (w[0] * outs[0] + w[1] * outs[1] + w[2] * outs[2]) / (w[0] + w[1] + w[2])


def _dil_attn_step(qkv, caches, layer, *, heads):
    bsz = qkv.shape[0]
    new = qkv.reshape(bsz, 3 * N_GROUPS, heads, ATT_DH)
    args, in_specs = [], []
    for s in range(3 * N_GROUPS):
        args.append(new)
        in_specs.append(pl.BlockSpec((1, 1, heads, ATT_DH), lambda b, s=s: (b, s, 0, 0)))
    for g, (win, dil) in enumerate(DILATED_GROUPS):
        nk = win // dil
        for buf in caches[g]:
            assert buf.shape[1:] == (bsz, win, heads, ATT_DH)
            args.append(buf.reshape(buf.shape[0], bsz, nk, dil, heads, ATT_DH))
            in_specs.append(pl.BlockSpec((None, None, nk, None, heads, ATT_DH),
                                         lambda b: (layer, b, 0, 0, 0, 0)))
    y = pl.pallas_call(
        _dil_attn_step_body,
        grid=(bsz,),
        in_specs=in_specs,
        out_specs=pl.BlockSpec((1, heads, ATT_DH), lambda b: (b, 0, 0)),
        out_shape=jax.ShapeDtypeStruct((bsz, heads, ATT_DH), F32),
        compiler_params=_cparams("parallel"),
        name="dil_attn_step",
    )(*args)
    return y.reshape(bsz, heads * ATT_DH)


def _xattn_body(q_ref, k_ref, v_ref, o_ref, *, heads):
    scale = XA_DH ** -0.5
    for h in range(heads):
        cols = slice(h * XA_DH, (h + 1) * XA_DH)
        s = lax.dot_general(q_ref[0][:, cols].astype(BF16), k_ref[0][:, cols].astype(BF16), NT_DIMS,
                            preferred_element_type=F32) * scale
        m = jnp.max(s, axis=-1, keepdims=True)
        p = jnp.exp(s - m)
        l = jnp.sum(p, axis=-1, keepdims=True)
        o = jnp.dot(p.astype(BF16), v_ref[0][:, cols].astype(BF16), preferred_element_type=F32) / l
        o_ref[0, :, cols] = o.astype(o_ref.dtype)


def _xattn(q, mem_kv, *, tq):
    bsz, t, hd = q.shape
    mem = mem_kv.shape[1]
    return pl.pallas_call(
        functools.partial(_xattn_body, heads=hd // XA_DH),
        grid=(bsz, t // tq),
        in_specs=[pl.BlockSpec((1, tq, hd), lambda b, i: (b, i, 0)),
                  pl.BlockSpec((1, mem, hd), lambda b, i: (b, 0, 0)),
                  pl.BlockSpec((1, mem, hd), lambda b, i: (b, 0, 1))],
        out_specs=pl.BlockSpec((1, tq, hd), lambda b, i: (b, i, 0)),
        out_shape=jax.ShapeDtypeStruct((bsz, t, hd), BF16),
        compiler_params=_cparams("parallel", "parallel"),
        name="xattn_seq",
    )(q, mem_kv, mem_kv)


def _xattn_step_body(q_ref, k_ref, v_ref, o_ref):
    o, _ = _step_attention(q_ref[0], None, None, k_ref[...], v_ref[...], XA_DH ** -0.5)
    o_ref[0] = o


def _xattn_step(q, mem_k, mem_v, layer):
    bsz, hd = q.shape
    _, _, mem, heads, dh = mem_k.shape
    assert mem_k.shape[1] == bsz and heads * dh == hd
    mem_spec = pl.BlockSpec((None, None, mem, heads, dh), lambda b: (layer, b, 0, 0, 0))
    o = pl.pallas_call(
        _xattn_step_body,
        grid=(bsz,),
        in_specs=[pl.BlockSpec((1, heads, dh), lambda b: (b, 0, 0)), mem_spec, mem_spec],
        out_specs=pl.BlockSpec((1, heads, dh), lambda b: (b, 0, 0)),
        out_shape=jax.ShapeDtypeStruct((bsz, heads, dh), F32),
        compiler_params=_cparams("parallel"),
        name="xattn_step",
    )(q.reshape(bsz, heads, dh), mem_k, mem_v)
    return o.reshape(bsz, hd)


TN_COLS = 512


def _rope_tables(pos):
    rows = pos.shape[0]
    half = ROPE_DIM // 2
    inv_freq = ROPE_THETA ** (-jnp.arange(half, dtype=F32) * 2.0 / ROPE_DIM)
    ang = pos.astype(F32)[:, None] * inv_freq[None, :]
    cos, sin = jnp.cos(ang), jnp.sin(ang)
    rest = jnp.zeros((rows, LANES - ROPE_DIM), F32)
    zh = jnp.zeros((rows, half), F32)
    return (jnp.concatenate([cos, cos, rest + 1.0], axis=1), jnp.concatenate([-sin, zh, rest], axis=1),
            jnp.concatenate([zh, sin, rest], axis=1))


def _pad_rows(a, rows):
    return jnp.pad(a, ((0, rows - a.shape[0]), (0, 0)))


ROWS_S = 16
TM_ROWS = 512
TN_WIDE = 1024
TM_GATE_UP = 2048


def _ffn(xp, xs, xn, gain, w_gu, w_down, layer, tag):
    nf = w_down.shape[1] // TN_COLS
    if xn is None:
        xn = _rms_norm(xp, gain, tm=TM_ROWS, out_dtype=BF16, name=tag + "_norm")
    (hid,), (hid_s,) = _proj(xn, xs, gain, w_gu, layer, w_col_blocks=(0, nf), n_col_blocks=nf,
                             epilogue=_ep_swiglu, out_dtypes=(BF16,), name=tag + "_gate_up", tm=TM_GATE_UP)
    return _res_proj(hid, hid_s, w_down, layer, xp, xs, 0.5, name=tag + "_down")


def _conv_mixer(xp, xs, gain, w_in, w_conv, w_out, layer, state, next_gain, *, nbat, t, nbat_s):
    d = xp.shape[1]
    nb = d // TN_COLS
    tm = min(1024, t)
    tiles = t // tm
    xn = _rms_norm(xp, gain, tm=TM_ROWS, out_dtype=BF16, name="conv_norm")
    w_spec = pl.BlockSpec((CONV_W, TN_COLS), lambda j, i: (0, j))
    st_spec = pl.BlockSpec((ROWS_S, TN_COLS), lambda j, i: (0, j))
    tail_out = (jax.ShapeDtypeStruct((nbat * SUBLANES, d), F32),
                pl.BlockSpec((SUBLANES, TN_COLS), lambda j, i: (i // tiles, j)))
    (gated, tail), (gated_s, u_s) = _proj(
        xn, xs, gain, w_in, layer, w_col_blocks=(0, nb, 2 * nb), n_col_blocks=nb, name="conv_in", tm=tm,
        epilogue=functools.partial(_ep_conv_seq, tiles_per_seq=tiles), out_dtypes=(BF16,), more_outs=(tail_out,),
        epilogue_s=_ep_conv_step, out_s_dtypes=(BF16, F32),
        extra=(w_conv,), extra_specs=(w_spec,),
        extra_s=(_pad_rows(state[:, 0], ROWS_S), _pad_rows(state[:, 1], ROWS_S), w_conv),
        extra_s_specs=(st_spec, st_spec, w_spec),
        scratch=(pltpu.VMEM((SUBLANES, TN_COLS), F32), pltpu.VMEM((tm + SUBLANES, TN_COLS), F32)))
    state_p = tail.reshape(nbat, SUBLANES, d)[:, SUBLANES - (CONV_W - 1):]
    state_s = jnp.stack([state[:, 1], u_s[:nbat_s]], axis=1)
    xp, xs, xn = _res_proj_norm(gated, gated_s, w_out, layer, xp, xs, 1.0, next_gain, name="conv_out")
    return xp, xs, xn, state_p, state_s


def _hgrn_mixer(xp, xs, gain, w_in, lower_bound, norm_gain, w_out, layer, state, next_gain, *, nbat, t, nbat_s):
    xn = _rms_norm(xp, gain, tm=TM_ROWS, out_dtype=BF16, name="hgrn_norm")
    (proj,), (proj_s,) = _proj(xn, xs, gain, w_in, layer, w_col_blocks=(0,), n_col_blocks=w_in.shape[2] // TN_WIDE,
                               epilogue=_ep_plain, out_dtypes=(F32,), name="hgrn_in", tn=TN_WIDE)
    y, st = _hgrn_seq(proj, lower_bound, norm_gain, batch=nbat, seq_len=t)
    y_s, state_s = _hgrn_step(proj_s[:nbat_s], lower_bound, norm_gain, state)
    xp, xs, xn = _res_proj_norm(y, _pad_rows(y_s, ROWS_S).astype(BF16), w_out, layer, xp, xs, 1.0, next_gain,
                                name="hgrn_out")
    return xp, xs, xn, jnp.swapaxes(st, -1, -2), state_s


def _attn_mixer(xp, xs, gain, w_qkv, w_out, layer, caches, next_gain, *, nbat, t, nbat_s):
    d = xp.shape[1]
    heads = d // ATT_DH
    n_sec = 3 * N_GROUPS
    sec_blocks = d // TN_WIDE
    tm = min(1024, t)
    tiles = t // tm
    tab_spec = pl.BlockSpec((tm, LANES), lambda j, i: (i % tiles, 0))
    tab_s_spec = pl.BlockSpec((ROWS_S, LANES), lambda j, i: (0, 0))
    xn = _rms_norm(xp, gain, tm=TM_ROWS, out_dtype=BF16, name="attn_norm")
    (qkv,), (qkv_s,) = _proj(
        xn, xs, gain, w_qkv, layer, w_col_blocks=(0,), n_col_blocks=n_sec * sec_blocks,
        epilogue=_ep_rotary, out_dtypes=(F32,), name="attn_qkv", tm=tm, tn=TN_WIDE,
        alt_epilogue=_ep_plain, alt_pred=lambda j: (j // sec_blocks) % 3 == 2,
        extra=_rope_tables(jnp.arange(t)), extra_specs=(tab_spec,) * 3,
        extra_s=_rope_tables(jnp.full((ROWS_S,), PAST_LEN, jnp.int32)), extra_s_specs=(tab_s_spec,) * 3)
    y = _dil_attn_seq(qkv.reshape(nbat, t, n_sec * d), batch=nbat, seq_len=t, heads=heads).reshape(nbat * t, d)
    y_s = _dil_attn_step(qkv_s[:nbat_s], caches, layer, heads=heads)
    q5 = qkv.reshape(nbat, t, N_GROUPS, 3, heads, ATT_DH)
    q5_s = qkv_s[:nbat_s].reshape(nbat_s, 1, N_GROUPS, 3, heads, ATT_DH)
    rows_p, rows_s = [], []
    for g, (win, _) in enumerate(DILATED_GROUPS):
        keep = min(win, t)
        rows_p += [q5[:, t - keep:, g, 1], q5[:, t - keep:, g, 2]]
        rows_s += [q5_s[:, :, g, 1], q5_s[:, :, g, 2]]
    xp, xs, xn = _res_proj_norm(y, _pad_rows(y_s, ROWS_S).astype(BF16), w_out, layer, xp, xs, 1.0, next_gain,
                                name="attn_out")
    return xp, xs, xn, rows_p, rows_s


def _cross_attention(xp, xs, xn, gain, w_q, w_o, layer, mem_kv, mem_k_s, mem_v_s, next_gain, *, nbat, t, nbat_s):
    hd = w_q.shape[2]
    (q,), (q_s,) = _proj(xn, xs, gain, w_q, layer, w_col_blocks=(0,), n_col_blocks=hd // TN_COLS,
                         epilogue=_ep_plain, out_dtypes=(F32,), name="xattn_q")
    o = _xattn(q.reshape(nbat, t, hd), mem_kv, tq=TM_ROWS).reshape(nbat * t, hd)
    o_s = _pad_rows(_xattn_step(q_s[:nbat_s], mem_k_s, mem_v_s, layer), ROWS_S).astype(BF16)
    return _res_proj_norm(o, o_s, w_o, layer, xp, xs, 1.0, next_gain, name="xattn_out")


def kernel(x_prompt, x_sample, state_conv, state_hgrn,
           cache_win_k0, cache_win_v0, cache_win_k1, cache_win_v1, cache_win_k2, cache_win_v2,
           cache_mem_k, cache_mem_v, mem_prompt,
           norm_ffn1, ffn1_w_gu, ffn1_w_down, norm_mix,
           conv_w_in, conv_w, conv_w_out,
           hgrn_w_in, hgrn_lb_logits, hgrn_norm, hgrn_w_out,
           attn_w_qkv, attn_w_out,
           norm_mem, xattn_w_kv, norm_xattn, xattn_w_q, xattn_w_o,
           norm_ffn2, ffn2_w_gu, ffn2_w_down, norm_final):
    batch, seq, d = x_prompt.shape
    dec_batch, dec_seq, _ = x_sample.shape
    assert dec_seq == 1 and dec_batch <= ROWS_S
    depth = norm_ffn1.shape[0]
    mem_len = mem_prompt.shape[1]
    xa_hd = xattn_w_q.shape[2]
    xa_heads = xa_hd // XA_DH
    win_cache = [(cache_win_k0, cache_win_v0), (cache_win_k1, cache_win_v1), (cache_win_k2, cache_win_v2)]
    sizes = dict(nbat=batch, t=seq, nbat_s=dec_batch)

    lb_p = jax.nn.softmax(hgrn_lb_logits.astype(F32), axis=0)
    lower_bounds = jnp.cumsum(lb_p, axis=0) - lb_p[0]

    xp = x_prompt.reshape(batch * seq, d)
    xs = _pad_rows(x_sample.reshape(dec_batch, d), ROWS_S)
    mem = mem_prompt.reshape(batch * mem_len, d)
    p_conv, p_hgrn, p_win, p_mk, p_mv, s_conv, s_hgrn, s_win = ([] for _ in range(8))
    for i in range(depth):
        j, kind = divmod(i, N_MIXERS)
        xp, xs = _ffn(xp, xs, None, norm_ffn1[i], ffn1_w_gu, ffn1_w_down, i, "ffn1")
        if kind == 0:
            xp, xs, xn, st_p, st_s = _conv_mixer(xp, xs, norm_mix[i], conv_w_in, conv_w[j], conv_w_out, j,
                                                 state_conv[j], norm_xattn[i], **sizes)
            p_conv.append(st_p)
            s_conv.append(st_s)
        elif kind == 1:
            xp, xs, xn, st_p, st_s = _hgrn_mixer(xp, xs, norm_mix[i], hgrn_w_in, lower_bounds[i], hgrn_norm[j],
                                                 hgrn_w_out, j, state_hgrn[j], norm_xattn[i], **sizes)
            p_hgrn.append(st_p)
            s_hgrn.append(st_s)
        else:
            xp, xs, xn, rows_p, rows_s = _attn_mixer(xp, xs, norm_mix[i], attn_w_qkv, attn_w_out, j, win_cache,
                                                     norm_xattn[i], **sizes)
            p_win.append(rows_p)
            s_win.append(rows_s)
        mem_n = _rms_norm(mem, norm_mem[i], tm=TM_ROWS, out_dtype=BF16, name="mem_norm")
        (kv,), _ = _proj(mem_n, None, None, xattn_w_kv, i, w_col_blocks=(0,), n_col_blocks=2 * xa_hd // TN_COLS,
                         epilogue=_ep_plain, out_dtypes=(F32,), name="mem_kv")
        kv = kv.reshape(batch, mem_len, 2 * xa_hd)
        p_mk.append(kv[:, :, :xa_hd].reshape(batch, mem_len, xa_heads, XA_DH))
        p_mv.append(kv[:, :, xa_hd:].reshape(batch, mem_len, xa_heads, XA_DH))
        xp, xs, xn = _cross_attention(xp, xs, xn, norm_xattn[i], xattn_w_q, xattn_w_o, i, kv, cache_mem_k, cache_mem_v,
                                      norm_ffn2[i], **sizes)
        xp, xs = _ffn(xp, xs, xn, norm_ffn2[i], ffn2_w_gu, ffn2_w_down, i, "ffn2")
    y_prompt = _rms_norm(xp, norm_final, tm=TM_ROWS, out_dtype=F32, name="final_norm").reshape(batch, seq, d)
    y_sample = _rms_norm(xs, norm_final, tm=ROWS_S, out_dtype=F32, name="final_norm_s")[:dec_batch]
    y_sample = y_sample.reshape(dec_batch, 1, d)

    stack_win = lambda rows: [jnp.stack([r[g] for r in rows]) for g in range(2 * N_GROUPS)]
    return (y_prompt, y_sample,
            jnp.stack(p_conv), jnp.stack(p_hgrn), *stack_win(p_win), jnp.stack(p_mk), jnp.stack(p_mv),
            jnp.stack(s_conv), jnp.stack(s_hgrn), *stack_win(s_win))
```

```python
import functools

import jax
import jax.numpy as jnp
from jax import lax
from jax.experimental import pallas as pl
from jax.experimental.pallas import tpu as pltpu

F32 = jnp.float32
BF16 = jnp.bfloat16
EPS = 1e-6
LANES = 128
SUBLANES = 8
MXU_COLS = 256
VMEM_LIMIT_BYTES = 56 << 20
NEG_BIG = -1e30

N_MIXERS = 3
CONV_W = 3
HG_DK = 128
ATT_DH = 128
DILATED_GROUPS = ((128, 1), (512, 4), (2048, 16))
N_GROUPS = 3
ROPE_DIM = ATT_DH // 4
ROPE_THETA = 500000.0
XA_DH = 128
GLA_BLOCK = 16
PAST_LEN = 16384

NT_DIMS = (((1,), (1,)), ((), ()))
TN_DIMS = (((0,), (0,)), ((), ()))


def _cparams(*semantics):
    return pltpu.CompilerParams(dimension_semantics=semantics, vmem_limit_bytes=VMEM_LIMIT_BYTES)


def _sigmoid(x):
    return 1.0 / (1.0 + jnp.exp(-x))


def _rms_body(x_ref, g_ref, o_ref):
    x = x_ref[...]
    ms = jnp.mean(x * x, axis=-1, keepdims=True)
    o_ref[...] = (x * lax.rsqrt(ms + EPS) * g_ref[...]).astype(o_ref.dtype)


def _rms_norm(x, gain, *, tm, out_dtype, name):
    m, k = x.shape
    return pl.pallas_call(
        _rms_body,
        grid=(m // tm,),
        in_specs=[pl.BlockSpec((tm, k), lambda i: (i, 0)), pl.BlockSpec((1, k), lambda i: (0, 0))],
        out_specs=pl.BlockSpec((tm, k), lambda i: (i, 0)),
        out_shape=jax.ShapeDtypeStruct((m, k), out_dtype),
        compiler_params=_cparams("parallel"),
        name=name,
    )(x, gain.reshape(1, k))


CAST_ROWS = 256


def _cast_weight(w_ref, wb_ref, slot):
    def body(c, carry):
        rows = pl.ds(pl.multiple_of(c * CAST_ROWS, CAST_ROWS), CAST_ROWS)
        wb_ref[slot, rows, :] = w_ref[0, rows, :].astype(BF16)
        return carry
    lax.fori_loop(0, w_ref.shape[1] // CAST_ROWS, body, 0)


def _ep_plain(accs, extra, outs, col0):
    acc = accs[0]
    outs[0][:, col0:col0 + acc.shape[1]] = acc.astype(outs[0].dtype)


def _ep_swiglu(accs, extra, outs, col0):
    gate, up = accs
    outs[0][:, col0:col0 + gate.shape[1]] = (gate * _sigmoid(gate) * up).astype(BF16)


def _ep_conv_seq(accs, extra, outs, col0, scratch, *, tiles_per_seq):
    b_gate, c_gate, z = accs
    (w_ref,), (gated_ref, tail_ref), (carry_ref, ue_ref) = extra, outs, scratch
    tm = z.shape[0]
    u = c_gate * z
    first = (pl.program_id(1) % tiles_per_seq) == 0
    ue_ref[0:SUBLANES, :] = jnp.where(first, 0.0, carry_ref[...])
    ue_ref[SUBLANES:SUBLANES + tm, :] = u
    w = w_ref[...]
    conv = (w[0:1] * ue_ref[pl.ds(SUBLANES - 2, tm), :] + w[1:2] * ue_ref[pl.ds(SUBLANES - 1, tm), :]
            + w[2:3] * u)
    gated_ref[...] = (b_gate * conv).astype(BF16)
    tail = u[tm - SUBLANES:, :]
    carry_ref[...] = tail
    tail_ref[...] = tail


def _ep_conv_step(accs, extra, outs, col0):
    b_gate, c_gate, z = accs
    s0_ref, s1_ref, w_ref = extra
    u = c_gate * z
    w = w_ref[...]
    conv = w[0:1] * s0_ref[...] + w[1:2] * s1_ref[...] + w[2:3] * u
    outs[0][...] = (b_gate * conv).astype(BF16)
    outs[1][...] = u


def _ep_rotary(accs, extra, outs, col0):
    cos, sin_lo, sin_hi = (r[...] for r in extra)
    acc = accs[0]
    half = ROPE_DIM // 2
    for c in range(acc.shape[1] // LANES):
        xc = acc[:, c * LANES:(c + 1) * LANES]
        outs[0][:, col0 + c * LANES:col0 + (c + 1) * LANES] = (
            xc * cos + pltpu.roll(xc, LANES - half, 1) * sin_lo + pltpu.roll(xc, half, 1) * sin_hi)


def _proj_body(*refs, n_w, n_extra, n_extra_s, n_out, n_out_s, has_sample, epilogue, epilogue_s, col_chunk,
               alt_epilogue, alt_pred):
    refs = list(refs)
    take = lambda n: [refs.pop(0) for _ in range(n)]
    (x_ref,) = take(1)
    xs_ref, gs_ref = take(2) if has_sample else (None, None)
    w_refs = take(n_w)
    extra = take(n_extra)
    extra_s = take(n_extra_s)
    outs = take(n_out)
    outs_s = take(n_out_s)
    wb_ref, *scratch = refs

    @pl.when(pl.program_id(1) == 0)
    def _():
        for k, w_ref in enumerate(w_refs):
            _cast_weight(w_ref, wb_ref, k)
        if has_sample:
            xs = xs_ref[...]
            ms = jnp.mean(xs * xs, axis=-1, keepdims=True)
            xsn = (xs * lax.rsqrt(ms + EPS) * gs_ref[...]).astype(BF16)
            accs_s = [jnp.dot(xsn, wb_ref[k], preferred_element_type=F32) for k in range(n_w)]
            if alt_epilogue is None:
                epilogue_s(accs_s, extra_s, outs_s, 0)
            else:
                use_alt = alt_pred(pl.program_id(0))
                pl.when(use_alt)(lambda: alt_epilogue(accs_s, extra_s, outs_s, 0))
                pl.when(jnp.logical_not(use_alt))(lambda: epilogue_s(accs_s, extra_s, outs_s, 0))

    def tile(ep):
        x = x_ref[...]
        for c0 in range(0, wb_ref.shape[2], col_chunk):
            accs = [jnp.dot(x, wb_ref[k, :, c0:c0 + col_chunk], preferred_element_type=F32) for k in range(n_w)]
            ep(accs, extra, outs, c0, *([scratch] if scratch else []))

    if alt_epilogue is None:
        tile(epilogue)
    else:
        use_alt = alt_pred(pl.program_id(0))
        pl.when(use_alt)(lambda: tile(alt_epilogue))
        pl.when(jnp.logical_not(use_alt))(lambda: tile(epilogue))


def _proj(xn, xs, gain, w_stack, layer, *, w_col_blocks, n_col_blocks, epilogue, out_dtypes, name,
          tm=1024, tn=512, col_chunk=None, extra=(), extra_specs=(), extra_s=(), extra_s_specs=(),
          epilogue_s=None, out_s_dtypes=None, more_outs=(), scratch=(), alt_epilogue=None, alt_pred=None):
    m, k = xn.shape
    tm = min(tm, m)
    has_sample = xs is not None
    assert m % tm == 0 and w_stack.shape[1] == k and k % CAST_ROWS == 0
    out_s_dtypes = (out_dtypes if out_s_dtypes is None else out_s_dtypes) if has_sample else ()
    n_total = n_col_blocks * tn
    args = [xn]
    in_specs = [pl.BlockSpec((tm, k), lambda j, i: (i, 0))]
    if has_sample:
        rs = xs.shape[0]
        args += [xs, gain.reshape(1, k)]
        in_specs += [pl.BlockSpec((rs, k), lambda j, i: (0, 0)), pl.BlockSpec((1, k), lambda j, i: (0, 0))]
    for off in w_col_blocks:
        args.append(w_stack)
        in_specs.append(pl.BlockSpec((1, k, tn), lambda j, i, off=off: (layer, 0, j + off)))
    args += list(extra) + list(extra_s)
    in_specs += list(extra_specs) + list(extra_s_specs)
    out_shape = [jax.ShapeDtypeStruct((m, n_total), dt) for dt in out_dtypes] + [o[0] for o in more_outs]
    out_specs = [pl.BlockSpec((tm, tn), lambda j, i: (i, j)) for _ in out_dtypes] + [o[1] for o in more_outs]
    n_out = len(out_shape)
    if has_sample:
        out_shape += [jax.ShapeDtypeStruct((rs, n_total), dt) for dt in out_s_dtypes]
        out_specs += [pl.BlockSpec((rs, tn), lambda j, i: (0, j)) for _ in out_s_dtypes]
    res = pl.pallas_call(
        functools.partial(_proj_body, n_w=len(w_col_blocks), n_extra=len(extra), n_extra_s=len(extra_s),
                          n_out=n_out, n_out_s=len(out_s_dtypes), has_sample=has_sample,
                          epilogue=epilogue, epilogue_s=epilogue_s or epilogue, col_chunk=col_chunk or tn,
                          alt_epilogue=alt_epilogue, alt_pred=alt_pred),
        grid=(n_col_blocks, m // tm),
        in_specs=in_specs,
        out_specs=out_specs,
        out_shape=out_shape,
        scratch_shapes=[pltpu.VMEM((len(w_col_blocks), k, tn), BF16)] + list(scratch),
        compiler_params=_cparams("arbitrary", "arbitrary"),
        name=name,
    )(*args)
    return (res[:n_out], res[n_out:]) if has_sample else (res, None)


def _res_proj_body(a_ref, as_ref, w_ref, r_ref, rs_ref, o_ref, os_ref, wb_ref, *, scale):
    @pl.when(pl.program_id(1) == 0)
    def _():
        _cast_weight(w_ref, wb_ref, 0)
        os_ref[...] = rs_ref[...] + scale * jnp.dot(as_ref[...], wb_ref[0], preferred_element_type=F32)

    o_ref[...] = r_ref[...] + scale * jnp.dot(a_ref[...], wb_ref[0], preferred_element_type=F32)


def _res_proj(a, a_s, w_stack, layer, res, res_s, scale, *, name, tm=512, tn=512):
    m, k = a.shape
    rs = a_s.shape[0]
    n = w_stack.shape[2]
    assert m % tm == 0 and n % tn == 0 and k % CAST_ROWS == 0 and res.shape == (m, n)
    return pl.pallas_call(
        functools.partial(_res_proj_body, scale=scale),
        grid=(n // tn, m // tm),
        in_specs=[pl.BlockSpec((tm, k), lambda j, i: (i, 0)),
                  pl.BlockSpec((rs, k), lambda j, i: (0, 0)),
                  pl.BlockSpec((1, k, tn), lambda j, i: (layer, 0, j)),
                  pl.BlockSpec((tm, tn), lambda j, i: (i, j)),
                  pl.BlockSpec((rs, tn), lambda j, i: (0, j))],
        out_specs=[pl.BlockSpec((tm, tn), lambda j, i: (i, j)),
                   pl.BlockSpec((rs, tn), lambda j, i: (0, j))],
        out_shape=[jax.ShapeDtypeStruct((m, n), F32), jax.ShapeDtypeStruct((rs, n), F32)],
        scratch_shapes=[pltpu.VMEM((1, k, tn), BF16)],
        compiler_params=_cparams("arbitrary", "arbitrary"),
        name=name,
    )(a, a_s, w_stack, res, res_s)


def _res_proj_norm_body(a_ref, as_ref, w_ref, r_ref, rs_ref, g_ref, o_ref, os_ref, xn_ref, wb_ref, *, scale):
    @pl.when(pl.program_id(0) == 0)
    def _():
        _cast_weight(w_ref, wb_ref, 0)
        os_ref[...] = rs_ref[...] + scale * jnp.dot(as_ref[...], wb_ref[0], preferred_element_type=F32)

    x = r_ref[...] + scale * jnp.dot(a_ref[...], wb_ref[0], preferred_element_type=F32)
    o_ref[...] = x
    ms = jnp.mean(x * x, axis=-1, keepdims=True)
    xn_ref[...] = (x * lax.rsqrt(ms + EPS) * g_ref[...]).astype(BF16)


def _res_proj_norm(a, a_s, w_stack, layer, res, res_s, scale, next_gain, *, name, tm=512):
    m, k = a.shape
    rs = a_s.shape[0]
    n = w_stack.shape[2]
    assert m % tm == 0 and k % CAST_ROWS == 0 and res.shape == (m, n)
    return pl.pallas_call(
        functools.partial(_res_proj_norm_body, scale=scale),
        grid=(m // tm,),
        in_specs=[pl.BlockSpec((tm, k), lambda i: (i, 0)),
                  pl.BlockSpec((rs, k), lambda i: (0, 0)),
                  pl.BlockSpec((1, k, n), lambda i: (layer, 0, 0), pipeline_mode=pl.Buffered(1)),
                  pl.BlockSpec((tm, n), lambda i: (i, 0)),
                  pl.BlockSpec((rs, n), lambda i: (0, 0)),
                  pl.BlockSpec((1, n), lambda i: (0, 0))],
        out_specs=[pl.BlockSpec((tm, n), lambda i: (i, 0)),
                   pl.BlockSpec((rs, n), lambda i: (0, 0)),
                   pl.BlockSpec((tm, n), lambda i: (i, 0))],
        out_shape=[jax.ShapeDtypeStruct((m, n), F32), jax.ShapeDtypeStruct((rs, n), F32),
                   jax.ShapeDtypeStruct((m, n), BF16)],
        scratch_shapes=[pltpu.VMEM((1, k, n), BF16)],
        compiler_params=_cparams("arbitrary"),
        name=name,
    )(a, a_s, w_stack, res, res_s, next_gain.reshape(1, n))


def _hgrn_seq_body(q_ref, z_ref, v_ref, g_ref, lb_ref, gn_ref, tri_ref, y_ref, st_out_ref,
                   st_ref, qs_ref, ks_ref, cum_ref, o_ref, *, hb):
    @pl.when(pl.program_id(2) == 0)
    def _():
        st_ref[...] = jnp.zeros_like(st_ref)

    tt = q_ref.shape[0]
    lb = lb_ref[...]
    z = z_ref[...]
    forget = lb + (1.0 - lb) * _sigmoid(z)
    ks_ref[...] = (1.0 - lb) * _sigmoid(-z)
    q = q_ref[...]
    qs_ref[...] = q * _sigmoid(q)
    cum_ref[...] = jnp.dot(tri_ref[...], jnp.log(forget), preferred_element_type=F32,
                           precision=lax.Precision.HIGHEST)

    t_idx = lax.broadcasted_iota(jnp.int32, (GLA_BLOCK, 1), 0)
    half = GLA_BLOCK // 2

    def block(j, carry):
        rows = pl.ds(pl.multiple_of(j * GLA_BLOCK, GLA_BLOCK), GLA_BLOCK)
        for h in range(hb):
            cols = slice(h * HG_DK, (h + 1) * HG_DK)
            cb = cum_ref[rows, cols]
            qb = qs_ref[rows, cols]
            kb = ks_ref[rows, cols]
            vb = v_ref[rows, cols]
            last = cb[GLA_BLOCK - 1:GLA_BLOCK, :]
            st = st_ref[h]
            o = lax.dot_general((qb * jnp.exp(cb)).astype(BF16), st.astype(BF16), NT_DIMS,
                                preferred_element_type=F32)
            o_lo = jnp.zeros((half, LANES), F32)
            o_hi = jnp.zeros((half, LANES), F32)
            for s in range(GLA_BLOCK):
                cs = cb[s:s + 1, :]
                qk_hi = qb[half:, :] * kb[s:s + 1, :]
                g_hi = jnp.where(t_idx[half:] >= s, jnp.exp(cb[half:, :] - cs), 0.0) * qk_hi
                o_hi = o_hi + jnp.sum(g_hi, axis=-1, keepdims=True) * vb[s:s + 1, :]
                if s < half:
                    qk_lo = qb[:half, :] * kb[s:s + 1, :]
                    g_lo = jnp.where(t_idx[:half] >= s, jnp.exp(cb[:half, :] - cs), 0.0) * qk_lo
                    o_lo = o_lo + jnp.sum(g_lo, axis=-1, keepdims=True) * vb[s:s + 1, :]
            o_ref[rows, cols] = o + jnp.concatenate([o_lo, o_hi], axis=0)
            kd = kb * jnp.exp(last - cb)
            upd = lax.dot_general(vb.astype(BF16), kd.astype(BF16), TN_DIMS, preferred_element_type=F32)
            st_ref[h] = st * jnp.exp(last) + upd
        return carry

    lax.fori_loop(0, tt // GLA_BLOCK, block, 0)

    for h in range(hb):
        cols = slice(h * HG_DK, (h + 1) * HG_DK)
        o = o_ref[:, cols]
        ms = jnp.mean(o * o, axis=-1, keepdims=True)
        g = g_ref[:, cols]
        y_ref[:, cols] = (o * lax.rsqrt(ms + EPS) * gn_ref[0, h:h + 1, :] * (g * _sigmoid(g))).astype(BF16)

    @pl.when(pl.program_id(2) == pl.num_programs(2) - 1)
    def _():
        st_out_ref[0] = st_ref[...]


def _hgrn_seq(proj, lower_bound, norm_gain, *, batch, seq_len, tt=256, hb=8):
    m, n4 = proj.shape
    d = n4 // 4
    heads = d // HG_DK
    hg = heads // hb
    nt = seq_len // tt
    wcols = hb * HG_DK
    tri = (jnp.arange(tt)[:, None] >= jnp.arange(tt)[None, :]) & (
        jnp.arange(tt)[:, None] // GLA_BLOCK == jnp.arange(tt)[None, :] // GLA_BLOCK)
    tri = tri.astype(F32)

    def col(sec):
        return pl.BlockSpec((tt, wcols), lambda b, h, t, sec=sec: (b * nt + t, sec * hg + h))

    y, st = pl.pallas_call(
        functools.partial(_hgrn_seq_body, hb=hb),
        grid=(batch, hg, nt),
        in_specs=[col(0), col(1), col(2), col(3),
                  pl.BlockSpec((1, wcols), lambda b, h, t: (0, h)),
                  pl.BlockSpec((1, hb, HG_DK), lambda b, h, t: (h, 0, 0)),
                  pl.BlockSpec((tt, tt), lambda b, h, t: (0, 0))],
        out_specs=[pl.BlockSpec((tt, wcols), lambda b, h, t: (b * nt + t, h)),
                   pl.BlockSpec((1, hb, HG_DK, HG_DK), lambda b, h, t: (b, h, 0, 0))],
        out_shape=[jax.ShapeDtypeStruct((m, d), BF16),
                   jax.ShapeDtypeStruct((batch, heads, HG_DK, HG_DK), F32)],
        scratch_shapes=[pltpu.VMEM((hb, HG_DK, HG_DK), F32), pltpu.VMEM((tt, wcols), F32),
                        pltpu.VMEM((tt, wcols), F32), pltpu.VMEM((tt, wcols), F32),
                        pltpu.VMEM((tt, wcols), F32)],
        compiler_params=_cparams("parallel", "parallel", "arbitrary"),
        name="hgrn_seq",
    )(proj, proj, proj, proj, lower_bound.reshape(1, d), norm_gain.reshape(hg, hb, HG_DK), tri)
    return y, st


def _hgrn_step_body(q_ref, z_ref, v_ref, g_ref, lb_ref, gn_ref, s_ref, y_ref, s_out_ref):
    heads = s_ref.shape[1]
    for h in range(heads):
        lb = lb_ref[h]
        z = z_ref[0, h]
        forget = lb + (1.0 - lb) * _sigmoid(z)
        k = (1.0 - lb) * _sigmoid(-z)
        q = q_ref[0, h]
        q = q * _sigmoid(q)
        s_new = forget * s_ref[0, h] + k * v_ref[0, h]
        s_out_ref[0, h] = s_new
        o = jnp.sum(q * s_new, axis=0, keepdims=True)
        ms = jnp.mean(o * o, axis=-1, keepdims=True)
        g = g_ref[0, h]
        y_ref[0, h] = o * lax.rsqrt(ms + EPS) * gn_ref[h] * (g * _sigmoid(g))


def _hgrn_step(proj, lower_bound, norm_gain, state):
    bsz, heads, dk, dv = state.shape
    d = heads * dk
    q, z, v, g = (proj[:, s * d:(s + 1) * d] for s in range(4))
    col = lambda a: a.reshape(bsz, heads, dk, 1)
    row = lambda a: a.reshape(bsz, heads, 1, dv)
    col_spec = pl.BlockSpec((1, heads, dk, 1), lambda b: (b, 0, 0, 0))
    row_spec = pl.BlockSpec((1, heads, 1, dv), lambda b: (b, 0, 0, 0))
    st_spec = pl.BlockSpec((1, heads, dk, dv), lambda b: (b, 0, 0, 0))
    y, s_new = pl.pallas_call(
        _hgrn_step_body,
        grid=(bsz,),
        in_specs=[col_spec, col_spec, row_spec, row_spec,
                  pl.BlockSpec((heads, dk, 1), lambda b: (0, 0, 0)),
                  pl.BlockSpec((heads, 1, dv), lambda b: (0, 0, 0)),
                  st_spec],
        out_specs=[row_spec, st_spec],
        out_shape=[jax.ShapeDtypeStruct((bsz, heads, 1, dv), F32),
                   jax.ShapeDtypeStruct(state.shape, F32)],
        compiler_params=_cparams("parallel"),
        name="hgrn_step",
    )(col(q), col(z), row(v), row(g), lower_bound.reshape(heads, dk, 1),
      norm_gain.reshape(heads, 1, dv), state)
    return y.reshape(bsz, d), s_new


def _band_block(q, k, v, scale, has_prev):
    n, nkeys = q.shape[0], k.shape[0]
    qi = lax.broadcasted_iota(jnp.int32, (n, nkeys), 0)
    ki = lax.broadcasted_iota(jnp.int32, (n, nkeys), 1)
    s = lax.dot_general(q.astype(BF16), k.astype(BF16), NT_DIMS, preferred_element_type=F32) * scale
    visible = ((ki >= qi) & (ki <= qi + n)) if has_prev else (ki <= qi)
    s = jnp.where(visible, s, NEG_BIG)
    m = jnp.max(s, axis=-1, keepdims=True)
    p = jnp.exp(s - m)
    l = jnp.sum(p, axis=-1, keepdims=True)
    acc = jnp.dot(p.astype(BF16), v.astype(BF16), preferred_element_type=F32)
    return acc / l, m + jnp.log(l)


def _dil_attn_seq_body(*refs, seq_len):
    qkv = refs[:3 * N_GROUPS]
    y_ref, og_ref, lse_ref = refs[3 * N_GROUPS:]
    scale = ATT_DH ** -0.5
    for g, (win, dil) in enumerate(DILATED_GROUPS):
        nk = win // dil
        q_ref, k_ref, v_ref = qkv[3 * g:3 * g + 3]
        for r in range(dil):
            for n in range(seq_len // dil // nk):
                def rows(first_blk, n_blk):
                    start, size = r + dil * nk * first_blk, nk * n_blk
                    return pl.ds(start, size, stride=dil) if dil > 1 else pl.ds(start, size)
                cur = rows(n, 1)
                keys = rows(n - 1, 2) if n else cur
                o, lse = _band_block(q_ref[0, cur, :], k_ref[0, keys, :], v_ref[0, keys, :], scale, n > 0)
                og_ref[g, cur, :] = o
                lse_ref[g, cur, :] = jnp.broadcast_to(lse, (nk, LANES))
    lse = [lse_ref[g] for g in range(N_GROUPS)]
    m = jnp.maximum(jnp.maximum(lse[0], lse[1]), lse[2])
    w = [jnp.exp(x - m) for x in lse]
    tot = w[0] + w[1] + w[2]
    y_ref[0] = ((w[0] * og_ref[0] + w[1] * og_ref[1] + w[2] * og_ref[2]) / tot).astype(BF16)


def _dil_attn_seq(qkv, *, batch, seq_len, heads):
    in_specs = []
    for g in range(N_GROUPS):
        for c in range(3):
            in_specs.append(pl.BlockSpec((1, seq_len, ATT_DH),
                                         lambda b, h, s=(g * 3 + c): (b, 0, s * heads + h)))
    return pl.pallas_call(
        functools.partial(_dil_attn_seq_body, seq_len=seq_len),
        grid=(batch, heads),
        in_specs=in_specs,
        out_specs=pl.BlockSpec((1, seq_len, ATT_DH), lambda b, h: (b, 0, h)),
        out_shape=jax.ShapeDtypeStruct((batch, seq_len, heads * ATT_DH), BF16),
        scratch_shapes=[pltpu.VMEM((N_GROUPS, seq_len, ATT_DH), F32),
                        pltpu.VMEM((N_GROUPS, seq_len, LANES), F32)],
        compiler_params=_cparams("parallel", "parallel"),
        name="dil_attn_seq",
    )(*([qkv] * (3 * N_GROUPS)))


def _bf16_round(x):
    return x.astype(BF16).astype(F32)


def _step_attention(q, k_new, v_new, k_c, v_c, scale):
    qb = _bf16_round(q)
    s_c = jnp.sum(_bf16_round(k_c) * qb[None], axis=-1, keepdims=True) * scale
    m = jnp.max(s_c, axis=0)
    if k_new is not None:
        s_n = jnp.sum(qb * _bf16_round(k_new), axis=-1, keepdims=True) * scale
        m = jnp.maximum(m, s_n)
    p_c = jnp.exp(s_c - m[None])
    l = jnp.sum(p_c, axis=0)
    acc = jnp.sum(_bf16_round(p_c) * _bf16_round(v_c), axis=0)
    if k_new is not None:
        p_n = jnp.exp(s_n - m)
        l = l + p_n
        acc = acc + _bf16_round(p_n) * _bf16_round(v_new)
    return acc / l, m + jnp.log(l)


def _dil_attn_step_body(*refs):
    new = refs[:3 * N_GROUPS]
    caches = refs[3 * N_GROUPS:5 * N_GROUPS]
    y_ref = refs[5 * N_GROUPS]
    scale = ATT_DH ** -0.5
    outs, lses = [], []
    for g in range(N_GROUPS):
        o, lse = _step_attention(new[3 * g][0, 0], new[3 * g + 1][0, 0], new[3 * g + 2][0, 0],
                                 caches[2 * g][...], caches[2 * g + 1][...], scale)
        outs.append(o)
        lses.append(lse)
    m = jnp.maximum(jnp.maximum(lses[0], lses[1]), lses[2])
    w = [jnp.exp(x - m) for x in lses]
    y_ref[0] = (w[0] * outs[0] + w[1] * outs[1] + w[2] * outs[2]) / (w[0] + w[1] + w[2])


def _dil_attn_step(qkv, caches, layer, *, heads):
    bsz = qkv.shape[0]
    new = qkv.reshape(bsz, 3 * N_GROUPS, heads, ATT_DH)
    args, in_specs = [], []
    for s in range(3 * N_GROUPS):
        args.append(new)
        in_specs.append(pl.BlockSpec((1, 1, heads, ATT_DH), lambda b, s=s: (b, s, 0, 0)))
    for g, (win, dil) in enumerate(DILATED_GROUPS):
        nk = win // dil
        for buf in caches[g]:
            assert buf.shape[1:] == (bsz, win, heads, ATT_DH)
            args.append(buf.reshape(buf.shape[0], bsz, nk, dil, heads, ATT_DH))
            in_specs.append(pl.BlockSpec((None, None, nk, None, heads, ATT_DH),
                                         lambda b: (layer, b, 0, 0, 0, 0)))
    y = pl.pallas_call(
        _dil_attn_step_body,
        grid=(bsz,),
        in_specs=in_specs,
        out_specs=pl.BlockSpec((1, heads, ATT_DH), lambda b: (b, 0, 0)),
        out_shape=jax.ShapeDtypeStruct((bsz, heads, ATT_DH), F32),
        compiler_params=_cparams("parallel"),
        name="dil_attn_step",
    )(*args)
    return y.reshape(bsz, heads * ATT_DH)


def _softmax_pv(s, v):
    m = jnp.max(s, axis=-1, keepdims=True)
    p = jnp.exp(s - m)
    l = jnp.sum(p, axis=-1, keepdims=True)
    return jnp.dot(p.astype(BF16), v.astype(BF16), preferred_element_type=F32) / l


def _xattn_block_body(xn_ref, r_ref, xs_ref, gq_ref, wq_ref, wo_ref, kv_ref, ks_ref, vs_ref, gn_ref,
                      o_ref, os_ref, xnn_ref, wqb_ref, wob_ref, att_ref, *, heads, nbat_s):
    scale = XA_DH ** -0.5
    hd = heads * XA_DH

    @pl.when(pl.program_id(0) == 0)
    def _():
        _cast_weight(wq_ref, wqb_ref, 0)
        _cast_weight(wo_ref, wob_ref, 0)
        xs = xs_ref[...]
        ms = jnp.mean(xs * xs, axis=-1, keepdims=True)
        xsn = (xs * lax.rsqrt(ms + EPS) * gq_ref[...]).astype(BF16)
        q_s = jnp.dot(xsn, wqb_ref[0], preferred_element_type=F32)
        att_ref[...] = jnp.zeros_like(att_ref)
        for b in range(nbat_s):
            for h in range(heads):
                cols = slice(h * XA_DH, (h + 1) * XA_DH)
                qb = _bf16_round(q_s[b:b + 1, cols])
                s = jnp.sum(_bf16_round(ks_ref[b, :, h, :]) * qb, axis=-1, keepdims=True) * scale
                p = jnp.exp(s - jnp.max(s, axis=0, keepdims=True))
                acc = jnp.sum(_bf16_round(p) * _bf16_round(vs_ref[b, :, h, :]), axis=0, keepdims=True)
                att_ref[b:b + 1, cols] = acc / jnp.sum(p, axis=0, keepdims=True)
        os_ref[...] = xs + jnp.dot(att_ref[0:xs.shape[0], :].astype(BF16), wob_ref[0], preferred_element_type=F32)

    q = jnp.dot(xn_ref[...], wqb_ref[0], preferred_element_type=F32)
    for h in range(heads):
        cols = slice(h * XA_DH, (h + 1) * XA_DH)
        k = kv_ref[0][:, cols]
        v = kv_ref[0][:, hd + h * XA_DH:hd + (h + 1) * XA_DH]
        s = lax.dot_general(q[:, cols].astype(BF16), k.astype(BF16), NT_DIMS, preferred_element_type=F32) * scale
        att_ref[:, cols] = _softmax_pv(s, v)
    x = r_ref[...] + jnp.dot(att_ref[...].astype(BF16), wob_ref[0], preferred_element_type=F32)
    o_ref[...] = x
    ms = jnp.mean(x * x, axis=-1, keepdims=True)
    xnn_ref[...] = (x * lax.rsqrt(ms + EPS) * gn_ref[...]).astype(BF16)


def _xattn_block(xp, xs, xn, gain, w_q, w_o, layer, mem_kv, mem_k_s, mem_v_s, next_gain, *, nbat, t, nbat_s,
                 tm=512):
    m, d = xp.shape
    rs = xs.shape[0]
    hd = w_q.shape[2]
    heads = hd // XA_DH
    mem_len = mem_kv.shape[1]
    tiles = t // tm
    assert t % tm == 0 and tm >= rs and mem_k_s.shape[1:] == (nbat_s, mem_len, heads, XA_DH)
    once = dict(pipeline_mode=pl.Buffered(1))
    mem_s_spec = pl.BlockSpec((None, nbat_s, mem_len, heads, XA_DH), lambda i: (layer, 0, 0, 0, 0), **once)
    row = lambda width: pl.BlockSpec((tm, width), lambda i: (i, 0))
    vec = pl.BlockSpec((1, d), lambda i: (0, 0))
    return pl.pallas_call(
        functools.partial(_xattn_block_body, heads=heads, nbat_s=nbat_s),
        grid=(m // tm,),
        in_specs=[row(d), row(d), pl.BlockSpec((rs, d), lambda i: (0, 0)), vec,
                  pl.BlockSpec((1, d, hd), lambda i: (layer, 0, 0), **once),
                  pl.BlockSpec((1, hd, d), lambda i: (layer, 0, 0), **once),
                  pl.BlockSpec((1, mem_len, 2 * hd), lambda i: (i // tiles, 0, 0)),
                  mem_s_spec, mem_s_spec, vec],
        out_specs=[row(d), pl.BlockSpec((rs, d), lambda i: (0, 0)), row(d)],
        out_shape=[jax.ShapeDtypeStruct((m, d), F32), jax.ShapeDtypeStruct((rs, d), F32),
                   jax.ShapeDtypeStruct((m, d), BF16)],
        scratch_shapes=[pltpu.VMEM((1, d, hd), BF16), pltpu.VMEM((1, hd, d), BF16), pltpu.VMEM((tm, hd), F32)],
        compiler_params=_cparams("arbitrary"),
        name="xattn_block",
    )(xn, xp, xs, gain.reshape(1, d), w_q, w_o, mem_kv, mem_k_s, mem_v_s, next_gain.reshape(1, d))


TN_COLS = 512


def _rope_tables(pos):
    rows = pos.shape[0]
    half = ROPE_DIM // 2
    inv_freq = ROPE_THETA ** (-jnp.arange(half, dtype=F32) * 2.0 / ROPE_DIM)
    ang = pos.astype(F32)[:, None] * inv_freq[None, :]
    cos, sin = jnp.cos(ang), jnp.sin(ang)
    rest = jnp.zeros((rows, LANES - ROPE_DIM), F32)
    zh = jnp.zeros((rows, half), F32)
    return (jnp.concatenate([cos, cos, rest + 1.0], axis=1), jnp.concatenate([-sin, zh, rest], axis=1),
            jnp.concatenate([zh, sin, rest], axis=1))


def _pad_rows(a, rows):
    return jnp.pad(a, ((0, rows - a.shape[0]), (0, 0)))


ROWS_S = 16
TM_ROWS = 512
TN_WIDE = 1024
TM_GATE_UP = 2048


def _ffn(xp, xs, xn, gain, w_gu, w_down, layer, tag):
    nf = w_down.shape[1] // TN_COLS
    if xn is None:
        xn = _rms_norm(xp, gain, tm=TM_ROWS, out_dtype=BF16, name=tag + "_norm")
    (hid,), (hid_s,) = _proj(xn, xs, gain, w_gu, layer, w_col_blocks=(0, nf), n_col_blocks=nf,
                             epilogue=_ep_swiglu, out_dtypes=(BF16,), name=tag + "_gate_up", tm=TM_GATE_UP)
    return _res_proj(hid, hid_s, w_down, layer, xp, xs, 0.5, name=tag + "_down")


def _conv_mixer(xp, xs, gain, w_in, w_conv, w_out, layer, state, next_gain, *, nbat, t, nbat_s):
    d = xp.shape[1]
    nb = d // TN_COLS
    tm = min(1024, t)
    tiles = t // tm
    xn = _rms_norm(xp, gain, tm=TM_ROWS, out_dtype=BF16, name="conv_norm")
    w_spec = pl.BlockSpec((CONV_W, TN_COLS), lambda j, i: (0, j))
    st_spec = pl.BlockSpec((ROWS_S, TN_COLS), lambda j, i: (0, j))
    tail_out = (jax.ShapeDtypeStruct((nbat * SUBLANES, d), F32),
                pl.BlockSpec((SUBLANES, TN_COLS), lambda j, i: (i // tiles, j)))
    (gated, tail), (gated_s, u_s) = _proj(
        xn, xs, gain, w_in, layer, w_col_blocks=(0, nb, 2 * nb), n_col_blocks=nb, name="conv_in", tm=tm,
        epilogue=functools.partial(_ep_conv_seq, tiles_per_seq=tiles), out_dtypes=(BF16,), more_outs=(tail_out,),
        epilogue_s=_ep_conv_step, out_s_dtypes=(BF16, F32),
        extra=(w_conv,), extra_specs=(w_spec,),
        extra_s=(_pad_rows(state[:, 0], ROWS_S), _pad_rows(state[:, 1], ROWS_S), w_conv),
        extra_s_specs=(st_spec, st_spec, w_spec),
        scratch=(pltpu.VMEM((SUBLANES, TN_COLS), F32), pltpu.VMEM((tm + SUBLANES, TN_COLS), F32)))
    state_p = tail.reshape(nbat, SUBLANES, d)[:, SUBLANES - (CONV_W - 1):]
    state_s = jnp.stack([state[:, 1], u_s[:nbat_s]], axis=1)
    xp, xs, xn = _res_proj_norm(gated, gated_s, w_out, layer, xp, xs, 1.0, next_gain, name="conv_out")
    return xp, xs, xn, state_p, state_s


def _hgrn_mixer(xp, xs, gain, w_in, lower_bound, norm_gain, w_out, layer, state, next_gain, *, nbat, t, nbat_s):
    xn = _rms_norm(xp, gain, tm=TM_ROWS, out_dtype=BF16, name="hgrn_norm")
    (proj,), (proj_s,) = _proj(xn, xs, gain, w_in, layer, w_col_blocks=(0,), n_col_blocks=w_in.shape[2] // TN_WIDE,
                               epilogue=_ep_plain, out_dtypes=(F32,), name="hgrn_in", tn=TN_WIDE)
    y, st = _hgrn_seq(proj, lower_bound, norm_gain, batch=nbat, seq_len=t)
    y_s, state_s = _hgrn_step(proj_s[:nbat_s], lower_bound, norm_gain, state)
    xp, xs, xn = _res_proj_norm(y, _pad_rows(y_s, ROWS_S).astype(BF16), w_out, layer, xp, xs, 1.0, next_gain,
                                name="hgrn_out")
    return xp, xs, xn, jnp.swapaxes(st, -1, -2), state_s


def _attn_mixer(xp, xs, gain, w_qkv, w_out, layer, caches, next_gain, *, nbat, t, nbat_s):
    d = xp.shape[1]
    heads = d // ATT_DH
    n_sec = 3 * N_GROUPS
    sec_blocks = d // TN_WIDE
    tm = min(1024, t)
    tiles = t // tm
    tab_spec = pl.BlockSpec((tm, LANES), lambda j, i: (i % tiles, 0))
    tab_s_spec = pl.BlockSpec((ROWS_S, LANES), lambda j, i: (0, 0))
    xn = _rms_norm(xp, gain, tm=TM_ROWS, out_dtype=BF16, name="attn_norm")
    (qkv,), (qkv_s,) = _proj(
        xn, xs, gain, w_qkv, layer, w_col_blocks=(0,), n_col_blocks=n_sec * sec_blocks,
        epilogue=_ep_rotary, out_dtypes=(F32,), name="attn_qkv", tm=tm, tn=TN_WIDE,
        alt_epilogue=_ep_plain, alt_pred=lambda j: (j // sec_blocks) % 3 == 2,
        extra=_rope_tables(jnp.arange(t)), extra_specs=(tab_spec,) * 3,
        extra_s=_rope_tables(jnp.full((ROWS_S,), PAST_LEN, jnp.int32)), extra_s_specs=(tab_s_spec,) * 3)
    y = _dil_attn_seq(qkv.reshape(nbat, t, n_sec * d), batch=nbat, seq_len=t, heads=heads).reshape(nbat * t, d)
    y_s = _dil_attn_step(qkv_s[:nbat_s], caches, layer, heads=heads)
    q5 = qkv.reshape(nbat, t, N_GROUPS, 3, heads, ATT_DH)
    q5_s = qkv_s[:nbat_s].reshape(nbat_s, 1, N_GROUPS, 3, heads, ATT_DH)
    rows_p, rows_s = [], []
    for g, (win, _) in enumerate(DILATED_GROUPS):
        keep = min(win, t)
        rows_p += [q5[:, t - keep:, g, 1], q5[:, t - keep:, g, 2]]
        rows_s += [q5_s[:, :, g, 1], q5_s[:, :, g, 2]]
    xp, xs, xn = _res_proj_norm(y, _pad_rows(y_s, ROWS_S).astype(BF16), w_out, layer, xp, xs, 1.0, next_gain,
                                name="attn_out")
    return xp, xs, xn, rows_p, rows_s


def kernel(x_prompt, x_sample, state_conv, state_hgrn,
           cache_win_k0, cache_win_v0, cache_win_k1, cache_win_v1, cache_win_k2, cache_win_v2,
           cache_mem_k, cache_mem_v, mem_prompt,
           norm_ffn1, ffn1_w_gu, ffn1_w_down, norm_mix,
           conv_w_in, conv_w, conv_w_out,
           hgrn_w_in, hgrn_lb_logits, hgrn_norm, hgrn_w_out,
           attn_w_qkv, attn_w_out,
           norm_mem, xattn_w_kv, norm_xattn, xattn_w_q, xattn_w_o,
           norm_ffn2, ffn2_w_gu, ffn2_w_down, norm_final):
    batch, seq, d = x_prompt.shape
    dec_batch, dec_seq, _ = x_sample.shape
    assert dec_seq == 1 and dec_batch <= ROWS_S
    depth = norm_ffn1.shape[0]
    mem_len = mem_prompt.shape[1]
    xa_hd = xattn_w_q.shape[2]
    xa_heads = xa_hd // XA_DH
    win_cache = [(cache_win_k0, cache_win_v0), (cache_win_k1, cache_win_v1), (cache_win_k2, cache_win_v2)]
    sizes = dict(nbat=batch, t=seq, nbat_s=dec_batch)

    lb_p = jax.nn.softmax(hgrn_lb_logits.astype(F32), axis=0)
    lower_bounds = jnp.cumsum(lb_p, axis=0) - lb_p[0]

    xp = x_prompt.reshape(batch * seq, d)
    xs = _pad_rows(x_sample.reshape(dec_batch, d), ROWS_S)
    mem = mem_prompt.reshape(batch * mem_len, d)
    p_conv, p_hgrn, p_win, p_mk, p_mv, s_conv, s_hgrn, s_win = ([] for _ in range(8))
    for i in range(depth):
        j, kind = divmod(i, N_MIXERS)
        xp, xs = _ffn(xp, xs, None, norm_ffn1[i], ffn1_w_gu, ffn1_w_down, i, "ffn1")
        if kind == 0:
            xp, xs, xn, st_p, st_s = _conv_mixer(xp, xs, norm_mix[i], conv_w_in, conv_w[j], conv_w_out, j,
                                                 state_conv[j], norm_xattn[i], **sizes)
            p_conv.append(st_p)
            s_conv.append(st_s)
        elif kind == 1:
            xp, xs, xn, st_p, st_s = _hgrn_mixer(xp, xs, norm_mix[i], hgrn_w_in, lower_bounds[i], hgrn_norm[j],
                                                 hgrn_w_out, j, state_hgrn[j], norm_xattn[i], **sizes)
            p_hgrn.append(st_p)
            s_hgrn.append(st_s)
        else:
            xp, xs, xn, rows_p, rows_s = _attn_mixer(xp, xs, norm_mix[i], attn_w_qkv, attn_w_out, j, win_cache,
                                                     norm_xattn[i], **sizes)
            p_win.append(rows_p)
            s_win.append(rows_s)
        mem_n = _rms_norm(mem, norm_mem[i], tm=TM_ROWS, out_dtype=BF16, name="mem_norm")
        (kv,), _ = _proj(mem_n, None, None, xattn_w_kv, i, w_col_blocks=(0,), n_col_blocks=2 * xa_hd // TN_COLS,
                         epilogue=_ep_plain, out_dtypes=(F32,), name="mem_kv")
        kv = kv.reshape(batch, mem_len, 2 * xa_hd)
        p_mk.append(kv[:, :, :xa_hd].reshape(batch, mem_len, xa_heads, XA_DH))
        p_mv.append(kv[:, :, xa_hd:].reshape(batch, mem_len, xa_heads, XA_DH))
        xp, xs, xn = _xattn_block(xp, xs, xn, norm_xattn[i], xattn_w_q, xattn_w_o, i, kv, cache_mem_k, cache_mem_v,
                                      norm_ffn2[i], **sizes)
        xp, xs = _ffn(xp, xs, xn, norm_ffn2[i], ffn2_w_gu, ffn2_w_down, i, "ffn2")
    y_prompt = _rms_norm(xp, norm_final, tm=TM_ROWS, out_dtype=F32, name="final_norm").reshape(batch, seq, d)
    y_sample = _rms_norm(xs, norm_final, tm=ROWS_S, out_dtype=F32, name="final_norm_s")[:dec_batch]
    y_sample = y_sample.reshape(dec_batch, 1, d)

    stack_win = lambda rows: [jnp.stack([r[g] for r in rows]) for g in range(2 * N_GROUPS)]
    return (y_prompt, y_sample,
            jnp.stack(p_conv), jnp.stack(p_hgrn), *stack_win(p_win), jnp.stack(p_mk), jnp.stack(p_mv),
            jnp.stack(s_conv), jnp.stack(s_hgrn), *stack_win(s_win))
```

```python
import functools

import jax
import jax.numpy as jnp
from jax import lax
from jax.experimental import pallas as pl
from jax.experimental.pallas import tpu as pltpu

F32 = jnp.float32
BF16 = jnp.bfloat16
EPS = 1e-6
LANES = 128
SUBLANES = 8
MXU_COLS = 256
VMEM_LIMIT_BYTES = 56 << 20
NEG_BIG = -1e30

N_MIXERS = 3
CONV_W = 3
HG_DK = 128
ATT_DH = 128
DILATED_GROUPS = ((128, 1), (512, 4), (2048, 16))
N_GROUPS = 3
ROPE_DIM = ATT_DH // 4
ROPE_THETA = 500000.0
XA_DH = 128
GLA_BLOCK = 16
PAST_LEN = 16384

NT_DIMS = (((1,), (1,)), ((), ()))
TN_DIMS = (((0,), (0,)), ((), ()))


def _cparams(*semantics):
    return pltpu.CompilerParams(dimension_semantics=semantics, vmem_limit_bytes=VMEM_LIMIT_BYTES)


def _sigmoid(x):
    return 1.0 / (1.0 + jnp.exp(-x))


def _rms_body(x_ref, g_ref, o_ref):
    x = x_ref[...]
    ms = jnp.mean(x * x, axis=-1, keepdims=True)
    o_ref[...] = (x * lax.rsqrt(ms + EPS) * g_ref[...]).astype(o_ref.dtype)


def _rms_norm(x, gain, *, tm, out_dtype, name):
    m, k = x.shape
    return pl.pallas_call(
        _rms_body,
        grid=(m // tm,),
        in_specs=[pl.BlockSpec((tm, k), lambda i: (i, 0)), pl.BlockSpec((1, k), lambda i: (0, 0))],
        out_specs=pl.BlockSpec((tm, k), lambda i: (i, 0)),
        out_shape=jax.ShapeDtypeStruct((m, k), out_dtype),
        compiler_params=_cparams("parallel"),
        name=name,
    )(x, gain.reshape(1, k))


CAST_ROWS = 256


def _cast_weight(w_ref, wb_ref, slot):
    def body(c, carry):
        rows = pl.ds(pl.multiple_of(c * CAST_ROWS, CAST_ROWS), CAST_ROWS)
        wb_ref[slot, rows, :] = w_ref[0, rows, :].astype(BF16)
        return carry
    lax.fori_loop(0, w_ref.shape[1] // CAST_ROWS, body, 0)


def _ep_plain(accs, extra, outs, col0):
    acc = accs[0]
    outs[0][:, col0:col0 + acc.shape[1]] = acc.astype(outs[0].dtype)


def _ep_swiglu(accs, extra, outs, col0):
    gate, up = accs
    outs[0][:, col0:col0 + gate.shape[1]] = (gate * _sigmoid(gate) * up).astype(BF16)


def _ep_conv_seq(accs, extra, outs, col0, scratch, *, tiles_per_seq):
    b_gate, c_gate, z = accs
    (w_ref,), (gated_ref, tail_ref), (carry_ref, ue_ref) = extra, outs, scratch
    tm = z.shape[0]
    u = c_gate * z
    first = (pl.program_id(1) % tiles_per_seq) == 0
    ue_ref[0:SUBLANES, :] = jnp.where(first, 0.0, carry_ref[...])
    ue_ref[SUBLANES:SUBLANES + tm, :] = u
    w = w_ref[...]
    conv = (w[0:1] * ue_ref[pl.ds(SUBLANES - 2, tm), :] + w[1:2] * ue_ref[pl.ds(SUBLANES - 1, tm), :]
            + w[2:3] * u)
    gated_ref[...] = (b_gate * conv).astype(BF16)
    tail = u[tm - SUBLANES:, :]
    carry_ref[...] = tail
    tail_ref[...] = tail


def _ep_conv_step(accs, extra, outs, col0):
    b_gate, c_gate, z = accs
    s0_ref, s1_ref, w_ref = extra
    u = c_gate * z
    w = w_ref[...]
    conv = w[0:1] * s0_ref[...] + w[1:2] * s1_ref[...] + w[2:3] * u
    outs[0][...] = (b_gate * conv).astype(BF16)
    outs[1][...] = u


def _ep_rotary(accs, extra, outs, col0):
    cos, sin_lo, sin_hi = (r[...] for r in extra)
    acc = accs[0]
    half = ROPE_DIM // 2
    for c in range(acc.shape[1] // LANES):
        xc = acc[:, c * LANES:(c + 1) * LANES]
        outs[0][:, col0 + c * LANES:col0 + (c + 1) * LANES] = (
            xc * cos + pltpu.roll(xc, LANES - half, 1) * sin_lo + pltpu.roll(xc, half, 1) * sin_hi)


def _proj_body(*refs, n_w, n_extra, n_extra_s, n_out, n_out_s, has_sample, epilogue, epilogue_s, col_chunk,
               alt_epilogue, alt_pred):
    refs = list(refs)
    take = lambda n: [refs.pop(0) for _ in range(n)]
    (x_ref,) = take(1)
    xs_ref, gs_ref = take(2) if has_sample else (None, None)
    w_refs = take(n_w)
    extra = take(n_extra)
    extra_s = take(n_extra_s)
    outs = take(n_out)
    outs_s = take(n_out_s)
    wb_ref, *scratch = refs

    @pl.when(pl.program_id(1) == 0)
    def _():
        for k, w_ref in enumerate(w_refs):
            _cast_weight(w_ref, wb_ref, k)
        if has_sample:
            xs = xs_ref[...]
            ms = jnp.mean(xs * xs, axis=-1, keepdims=True)
            xsn = (xs * lax.rsqrt(ms + EPS) * gs_ref[...]).astype(BF16)
            accs_s = [jnp.dot(xsn, wb_ref[k], preferred_element_type=F32) for k in range(n_w)]
            if alt_epilogue is None:
                epilogue_s(accs_s, extra_s, outs_s, 0)
            else:
                use_alt = alt_pred(pl.program_id(0))
                pl.when(use_alt)(lambda: alt_epilogue(accs_s, extra_s, outs_s, 0))
                pl.when(jnp.logical_not(use_alt))(lambda: epilogue_s(accs_s, extra_s, outs_s, 0))

    def tile(ep):
        x = x_ref[...]
        for c0 in range(0, wb_ref.shape[2], col_chunk):
            accs = [jnp.dot(x, wb_ref[k, :, c0:c0 + col_chunk], preferred_element_type=F32) for k in range(n_w)]
            ep(accs, extra, outs, c0, *([scratch] if scratch else []))

    if alt_epilogue is None:
        tile(epilogue)
    else:
        use_alt = alt_pred(pl.program_id(0))
        pl.when(use_alt)(lambda: tile(alt_epilogue))
        pl.when(jnp.logical_not(use_alt))(lambda: tile(epilogue))


def _proj(xn, xs, gain, w_stack, layer, *, w_col_blocks, n_col_blocks, epilogue, out_dtypes, name,
          tm=1024, tn=512, col_chunk=None, extra=(), extra_specs=(), extra_s=(), extra_s_specs=(),
          epilogue_s=None, out_s_dtypes=None, more_outs=(), scratch=(), alt_epilogue=None, alt_pred=None):
    m, k = xn.shape
    tm = min(tm, m)
    has_sample = xs is not None
    assert m % tm == 0 and w_stack.shape[1] == k and k % CAST_ROWS == 0
    out_s_dtypes = (out_dtypes if out_s_dtypes is None else out_s_dtypes) if has_sample else ()
    n_total = n_col_blocks * tn
    args = [xn]
    in_specs = [pl.BlockSpec((tm, k), lambda j, i: (i, 0))]
    if has_sample:
        rs = xs.shape[0]
        args += [xs, gain.reshape(1, k)]
        in_specs += [pl.BlockSpec((rs, k), lambda j, i: (0, 0)), pl.BlockSpec((1, k), lambda j, i: (0, 0))]
    for off in w_col_blocks:
        args.append(w_stack)
        in_specs.append(pl.BlockSpec((1, k, tn), lambda j, i, off=off: (layer, 0, j + off)))
    args += list(extra) + list(extra_s)
    in_specs += list(extra_specs) + list(extra_s_specs)
    out_shape = [jax.ShapeDtypeStruct((m, n_total), dt) for dt in out_dtypes] + [o[0] for o in more_outs]
    out_specs = [pl.BlockSpec((tm, tn), lambda j, i: (i, j)) for _ in out_dtypes] + [o[1] for o in more_outs]
    n_out = len(out_shape)
    if has_sample:
        out_shape += [jax.ShapeDtypeStruct((rs, n_total), dt) for dt in out_s_dtypes]
        out_specs += [pl.BlockSpec((rs, tn), lambda j, i: (0, j)) for _ in out_s_dtypes]
    res = pl.pallas_call(
        functools.partial(_proj_body, n_w=len(w_col_blocks), n_extra=len(extra), n_extra_s=len(extra_s),
                          n_out=n_out, n_out_s=len(out_s_dtypes), has_sample=has_sample,
                          epilogue=epilogue, epilogue_s=epilogue_s or epilogue, col_chunk=col_chunk or tn,
                          alt_epilogue=alt_epilogue, alt_pred=alt_pred),
        grid=(n_col_blocks, m // tm),
        in_specs=in_specs,
        out_specs=out_specs,
        out_shape=out_shape,
        scratch_shapes=[pltpu.VMEM((len(w_col_blocks), k, tn), BF16)] + list(scratch),
        compiler_params=_cparams("arbitrary", "arbitrary"),
        name=name,
    )(*args)
    return (res[:n_out], res[n_out:]) if has_sample else (res, None)


def _res_proj_norm_body(a_ref, as_ref, w_ref, r_ref, rs_ref, g_ref, o_ref, os_ref, xn_ref, wb_ref, *, scale):
    @pl.when(pl.program_id(0) == 0)
    def _():
        _cast_weight(w_ref, wb_ref, 0)
        os_ref[...] = rs_ref[...] + scale * jnp.dot(as_ref[...], wb_ref[0], preferred_element_type=F32)

    x = r_ref[...] + scale * jnp.dot(a_ref[...], wb_ref[0], preferred_element_type=F32)
    o_ref[...] = x
    ms = jnp.mean(x * x, axis=-1, keepdims=True)
    xn_ref[...] = (x * lax.rsqrt(ms + EPS) * g_ref[...]).astype(BF16)


def _res_proj_norm(a, a_s, w_stack, layer, res, res_s, scale, next_gain, *, name, tm=512):
    m, k = a.shape
    rs = a_s.shape[0]
    n = w_stack.shape[2]
    assert m % tm == 0 and k % CAST_ROWS == 0 and res.shape == (m, n)
    return pl.pallas_call(
        functools.partial(_res_proj_norm_body, scale=scale),
        grid=(m // tm,),
        in_specs=[pl.BlockSpec((tm, k), lambda i: (i, 0)),
                  pl.BlockSpec((rs, k), lambda i: (0, 0)),
                  pl.BlockSpec((1, k, n), lambda i: (layer, 0, 0), pipeline_mode=pl.Buffered(1)),
                  pl.BlockSpec((tm, n), lambda i: (i, 0)),
                  pl.BlockSpec((rs, n), lambda i: (0, 0)),
                  pl.BlockSpec((1, n), lambda i: (0, 0))],
        out_specs=[pl.BlockSpec((tm, n), lambda i: (i, 0)),
                   pl.BlockSpec((rs, n), lambda i: (0, 0)),
                   pl.BlockSpec((tm, n), lambda i: (i, 0))],
        out_shape=[jax.ShapeDtypeStruct((m, n), F32), jax.ShapeDtypeStruct((rs, n), F32),
                   jax.ShapeDtypeStruct((m, n), BF16)],
        scratch_shapes=[pltpu.VMEM((1, k, n), BF16)],
        compiler_params=_cparams("arbitrary"),
        name=name,
    )(a, a_s, w_stack, res, res_s, next_gain.reshape(1, n))


WEIGHT_CHUNK_ROWS = 512


def _res_proj_norm_big_body(a_ref, as_ref, w_hbm, r_ref, rs_ref, g_ref, o_ref, os_ref, xn_ref,
                            wb_ref, stage_ref, sem, *, scale, layer):
    @pl.when(pl.program_id(0) == 0)
    def _():
        chunk = stage_ref.shape[1]
        n_chunks = wb_ref.shape[0] // chunk

        def copy(c):
            return pltpu.make_async_copy(w_hbm.at[layer, pl.ds(c * chunk, chunk), :], stage_ref.at[c % 2],
                                         sem.at[c % 2])
        copy(0).start()
        for c in range(n_chunks):
            if c + 1 < n_chunks:
                copy(c + 1).start()
            copy(c).wait()
            wb_ref[pl.ds(c * chunk, chunk), :] = stage_ref[c % 2].astype(BF16)
        os_ref[...] = rs_ref[...] + scale * jnp.dot(as_ref[...], wb_ref[...], preferred_element_type=F32)

    x = r_ref[...] + scale * jnp.dot(a_ref[...], wb_ref[...], preferred_element_type=F32)
    o_ref[...] = x
    ms = jnp.mean(x * x, axis=-1, keepdims=True)
    xn_ref[...] = (x * lax.rsqrt(ms + EPS) * g_ref[...]).astype(xn_ref.dtype)


def _res_proj_norm_big(a, a_s, w_stack, layer, res, res_s, scale, next_gain, *, name, norm_dtype=BF16, tm=256):
    m, k = a.shape
    rs = a_s.shape[0]
    n = w_stack.shape[2]
    assert m % tm == 0 and k % WEIGHT_CHUNK_ROWS == 0 and res.shape == (m, n)
    return pl.pallas_call(
        functools.partial(_res_proj_norm_big_body, scale=scale, layer=layer),
        grid=(m // tm,),
        in_specs=[pl.BlockSpec((tm, k), lambda i: (i, 0)),
                  pl.BlockSpec((rs, k), lambda i: (0, 0)),
                  pl.BlockSpec(memory_space=pl.ANY),
                  pl.BlockSpec((tm, n), lambda i: (i, 0)),
                  pl.BlockSpec((rs, n), lambda i: (0, 0)),
                  pl.BlockSpec((1, n), lambda i: (0, 0))],
        out_specs=[pl.BlockSpec((tm, n), lambda i: (i, 0)),
                   pl.BlockSpec((rs, n), lambda i: (0, 0)),
                   pl.BlockSpec((tm, n), lambda i: (i, 0))],
        out_shape=[jax.ShapeDtypeStruct((m, n), F32), jax.ShapeDtypeStruct((rs, n), F32),
                   jax.ShapeDtypeStruct((m, n), norm_dtype)],
        scratch_shapes=[pltpu.VMEM((k, n), BF16), pltpu.VMEM((2, WEIGHT_CHUNK_ROWS, n), F32),
                        pltpu.SemaphoreType.DMA((2,))],
        compiler_params=_cparams("arbitrary"),
        name=name,
    )(a, a_s, w_stack, res, res_s, next_gain.reshape(1, n))


def _hgrn_seq_body(q_ref, z_ref, v_ref, g_ref, lb_ref, gn_ref, tri_ref, y_ref, st_out_ref,
                   st_ref, qs_ref, ks_ref, cum_ref, o_ref, *, hb):
    @pl.when(pl.program_id(2) == 0)
    def _():
        st_ref[...] = jnp.zeros_like(st_ref)

    tt = q_ref.shape[0]
    lb = lb_ref[...]
    z = z_ref[...]
    forget = lb + (1.0 - lb) * _sigmoid(z)
    ks_ref[...] = (1.0 - lb) * _sigmoid(-z)
    q = q_ref[...]
    qs_ref[...] = q * _sigmoid(q)
    cum_ref[...] = jnp.dot(tri_ref[...], jnp.log(forget), preferred_element_type=F32,
                           precision=lax.Precision.HIGHEST)

    t_idx = lax.broadcasted_iota(jnp.int32, (GLA_BLOCK, 1), 0)
    half = GLA_BLOCK // 2

    def block(j, carry):
        rows = pl.ds(pl.multiple_of(j * GLA_BLOCK, GLA_BLOCK), GLA_BLOCK)
        for h in range(hb):
            cols = slice(h * HG_DK, (h + 1) * HG_DK)
            cb = cum_ref[rows, cols]
            qb = qs_ref[rows, cols]
            kb = ks_ref[rows, cols]
            vb = v_ref[rows, cols]
            last = cb[GLA_BLOCK - 1:GLA_BLOCK, :]
            st = st_ref[h]
            o = lax.dot_general((qb * jnp.exp(cb)).astype(BF16), st.astype(BF16), NT_DIMS,
                                preferred_element_type=F32)
            o_lo = jnp.zeros((half, LANES), F32)
            o_hi = jnp.zeros((half, LANES), F32)
            for s in range(GLA_BLOCK):
                cs = cb[s:s + 1, :]
                qk_hi = qb[half:, :] * kb[s:s + 1, :]
                g_hi = jnp.where(t_idx[half:] >= s, jnp.exp(cb[half:, :] - cs), 0.0) * qk_hi
                o_hi = o_hi + jnp.sum(g_hi, axis=-1, keepdims=True) * vb[s:s + 1, :]
                if s < half:
                    qk_lo = qb[:half, :] * kb[s:s + 1, :]
                    g_lo = jnp.where(t_idx[:half] >= s, jnp.exp(cb[:half, :] - cs), 0.0) * qk_lo
                    o_lo = o_lo + jnp.sum(g_lo, axis=-1, keepdims=True) * vb[s:s + 1, :]
            o_ref[rows, cols] = o + jnp.concatenate([o_lo, o_hi], axis=0)
            kd = kb * jnp.exp(last - cb)
            upd = lax.dot_general(vb.astype(BF16), kd.astype(BF16), TN_DIMS, preferred_element_type=F32)
            st_ref[h] = st * jnp.exp(last) + upd
        return carry

    lax.fori_loop(0, tt // GLA_BLOCK, block, 0)

    for h in range(hb):
        cols = slice(h * HG_DK, (h + 1) * HG_DK)
        o = o_ref[:, cols]
        ms = jnp.mean(o * o, axis=-1, keepdims=True)
        g = g_ref[:, cols]
        y_ref[:, cols] = (o * lax.rsqrt(ms + EPS) * gn_ref[0, h:h + 1, :] * (g * _sigmoid(g))).astype(BF16)

    @pl.when(pl.program_id(2) == pl.num_programs(2) - 1)
    def _():
        st_out_ref[0] = st_ref[...]


def _hgrn_seq(proj, lower_bound, norm_gain, *, batch, seq_len, tt=256, hb=8):
    m, n4 = proj.shape
    d = n4 // 4
    heads = d // HG_DK
    hg = heads // hb
    nt = seq_len // tt
    wcols = hb * HG_DK
    tri = (jnp.arange(tt)[:, None] >= jnp.arange(tt)[None, :]) & (
        jnp.arange(tt)[:, None] // GLA_BLOCK == jnp.arange(tt)[None, :] // GLA_BLOCK)
    tri = tri.astype(F32)

    def col(sec):
        return pl.BlockSpec((tt, wcols), lambda b, h, t, sec=sec: (b * nt + t, sec * hg + h))

    y, st = pl.pallas_call(
        functools.partial(_hgrn_seq_body, hb=hb),
        grid=(batch, hg, nt),
        in_specs=[col(0), col(1), col(2), col(3),
                  pl.BlockSpec((1, wcols), lambda b, h, t: (0, h)),
                  pl.BlockSpec((1, hb, HG_DK), lambda b, h, t: (h, 0, 0)),
                  pl.BlockSpec((tt, tt), lambda b, h, t: (0, 0))],
        out_specs=[pl.BlockSpec((tt, wcols), lambda b, h, t: (b * nt + t, h)),
                   pl.BlockSpec((1, hb, HG_DK, HG_DK), lambda b, h, t: (b, h, 0, 0))],
        out_shape=[jax.ShapeDtypeStruct((m, d), BF16),
                   jax.ShapeDtypeStruct((batch, heads, HG_DK, HG_DK), F32)],
        scratch_shapes=[pltpu.VMEM((hb, HG_DK, HG_DK), F32), pltpu.VMEM((tt, wcols), F32),
                        pltpu.VMEM((tt, wcols), F32), pltpu.VMEM((tt, wcols), F32),
                        pltpu.VMEM((tt, wcols), F32)],
        compiler_params=_cparams("parallel", "parallel", "arbitrary"),
        name="hgrn_seq",
    )(proj, proj, proj, proj, lower_bound.reshape(1, d), norm_gain.reshape(hg, hb, HG_DK), tri)
    return y, st


def _hgrn_step_body(q_ref, z_ref, v_ref, g_ref, lb_ref, gn_ref, s_ref, y_ref, s_out_ref):
    heads = s_ref.shape[1]
    for h in range(heads):
        lb = lb_ref[h]
        z = z_ref[0, h]
        forget = lb + (1.0 - lb) * _sigmoid(z)
        k = (1.0 - lb) * _sigmoid(-z)
        q = q_ref[0, h]
        q = q * _sigmoid(q)
        s_new = forget * s_ref[0, h] + k * v_ref[0, h]
        s_out_ref[0, h] = s_new
        o = jnp.sum(q * s_new, axis=0, keepdims=True)
        ms = jnp.mean(o * o, axis=-1, keepdims=True)
        g = g_ref[0, h]
        y_ref[0, h] = o * lax.rsqrt(ms + EPS) * gn_ref[h] * (g * _sigmoid(g))


def _hgrn_step(proj, lower_bound, norm_gain, state):
    bsz, heads, dk, dv = state.shape
    d = heads * dk
    q, z, v, g = (proj[:, s * d:(s + 1) * d] for s in range(4))
    col = lambda a: a.reshape(bsz, heads, dk, 1)
    row = lambda a: a.reshape(bsz, heads, 1, dv)
    col_spec = pl.BlockSpec((1, heads, dk, 1), lambda b: (b, 0, 0, 0))
    row_spec = pl.BlockSpec((1, heads, 1, dv), lambda b: (b, 0, 0, 0))
    st_spec = pl.BlockSpec((1, heads, dk, dv), lambda b: (b, 0, 0, 0))
    y, s_new = pl.pallas_call(
        _hgrn_step_body,
        grid=(bsz,),
        in_specs=[col_spec, col_spec, row_spec, row_spec,
                  pl.BlockSpec((heads, dk, 1), lambda b: (0, 0, 0)),
                  pl.BlockSpec((heads, 1, dv), lambda b: (0, 0, 0)),
                  st_spec],
        out_specs=[row_spec, st_spec],
        out_shape=[jax.ShapeDtypeStruct((bsz, heads, 1, dv), F32),
                   jax.ShapeDtypeStruct(state.shape, F32)],
        compiler_params=_cparams("parallel"),
        name="hgrn_step",
    )(col(q), col(z), row(v), row(g), lower_bound.reshape(heads, dk, 1),
      norm_gain.reshape(heads, 1, dv), state)
    return y.reshape(bsz, d), s_new


def _band_block(q, k, v, scale, has_prev):
    n, nkeys = q.shape[0], k.shape[0]
    qi = lax.broadcasted_iota(jnp.int32, (n, nkeys), 0)
    ki = lax.broadcasted_iota(jnp.int32, (n, nkeys), 1)
    s = lax.dot_general(q.astype(BF16), k.astype(BF16), NT_DIMS, preferred_element_type=F32) * scale
    visible = ((ki >= qi) & (ki <= qi + n)) if has_prev else (ki <= qi)
    s = jnp.where(visible, s, NEG_BIG)
    m = jnp.max(s, axis=-1, keepdims=True)
    p = jnp.exp(s - m)
    l = jnp.sum(p, axis=-1, keepdims=True)
    acc = jnp.dot(p.astype(BF16), v.astype(BF16), preferred_element_type=F32)
    return acc / l, m + jnp.log(l)


def _dil_attn_seq_body(*refs, seq_len):
    qkv = refs[:3 * N_GROUPS]
    y_ref, og_ref, lse_ref = refs[3 * N_GROUPS:]
    scale = ATT_DH ** -0.5
    for g, (win, dil) in enumerate(DILATED_GROUPS):
        nk = win // dil
        q_ref, k_ref, v_ref = qkv[3 * g:3 * g + 3]
        for r in range(dil):
            for n in range(seq_len // dil // nk):
                def rows(first_blk, n_blk):
                    start, size = r + dil * nk * first_blk, nk * n_blk
                    return pl.ds(start, size, stride=dil) if dil > 1 else pl.ds(start, size)
                cur = rows(n, 1)
                keys = rows(n - 1, 2) if n else cur
                o, lse = _band_block(q_ref[0, cur, :], k_ref[0, keys, :], v_ref[0, keys, :], scale, n > 0)
                og_ref[g, cur, :] = o
                lse_ref[g, cur, :] = jnp.broadcast_to(lse, (nk, LANES))
    lse = [lse_ref[g] for g in range(N_GROUPS)]
    m = jnp.maximum(jnp.maximum(lse[0], lse[1]), lse[2])
    w = [jnp.exp(x - m) for x in lse]
    tot = w[0] + w[1] + w[2]
    y_ref[0] = ((w[0] * og_ref[0] + w[1] * og_ref[1] + w[2] * og_ref[2]) / tot).astype(BF16)


def _dil_attn_seq(qkv, *, batch, seq_len, heads):
    in_specs = []
    for g in range(N_GROUPS):
        for c in range(3):
            in_specs.append(pl.BlockSpec((1, seq_len, ATT_DH),
                                         lambda b, h, s=(g * 3 + c): (b, 0, s * heads + h)))
    return pl.pallas_call(
        functools.partial(_dil_attn_seq_body, seq_len=seq_len),
        grid=(batch, heads),
        in_specs=in_specs,
        out_specs=pl.BlockSpec((1, seq_len, ATT_DH), lambda b, h: (b, 0, h)),
        out_shape=jax.ShapeDtypeStruct((batch, seq_len, heads * ATT_DH), BF16),
        scratch_shapes=[pltpu.VMEM((N_GROUPS, seq_len, ATT_DH), F32),
                        pltpu.VMEM((N_GROUPS, seq_len, LANES), F32)],
        compiler_params=_cparams("parallel", "parallel"),
        name="dil_attn_seq",
    )(*([qkv] * (3 * N_GROUPS)))


def _bf16_round(x):
    return x.astype(BF16).astype(F32)


def _step_attention(q, k_new, v_new, k_c, v_c, scale):
    qb = _bf16_round(q)
    s_c = jnp.sum(_bf16_round(k_c) * qb[None], axis=-1, keepdims=True) * scale
    m = jnp.max(s_c, axis=0)
    if k_new is not None:
        s_n = jnp.sum(qb * _bf16_round(k_new), axis=-1, keepdims=True) * scale
        m = jnp.maximum(m, s_n)
    p_c = jnp.exp(s_c - m[None])
    l = jnp.sum(p_c, axis=0)
    acc = jnp.sum(_bf16_round(p_c) * _bf16_round(v_c), axis=0)
    if k_new is not None:
        p_n = jnp.exp(s_n - m)
        l = l + p_n
        acc = acc + _bf16_round(p_n) * _bf16_round(v_new)
    return acc / l, m + jnp.log(l)


def _dil_attn_step_body(*refs):
    new = refs[:3 * N_GROUPS]
    caches = refs[3 * N_GROUPS:5 * N_GROUPS]
    y_ref = refs[5 * N_GROUPS]
    scale = ATT_DH ** -0.5
    outs, lses = [], []
    for g in range(N_GROUPS):
        o, lse = _step_attention(new[3 * g][0, 0], new[3 * g + 1][0, 0], new[3 * g + 2][0, 0],
                                 caches[2 * g][...], caches[2 * g + 1][...], scale)
        outs.append(o)
        lses.append(lse)
    m = jnp.maximum(jnp.maximum(lses[0], lses[1]), lses[2])
    w = [jnp.exp(x - m) for x in lses]
    y_ref[0] = (w[0] * outs[0] + w[1] * outs[1] + w[2] * outs[2]) / (w[0] + w[1] + w[2])


def _dil_attn_step(qkv, caches, layer, *, heads):
    bsz = qkv.shape[0]
    new = qkv.reshape(bsz, 3 * N_GROUPS, heads, ATT_DH)
    args, in_specs = [], []
    for s in range(3 * N_GROUPS):
        args.append(new)
        in_specs.append(pl.BlockSpec((1, 1, heads, ATT_DH), lambda b, s=s: (b, s, 0, 0)))
    for g, (win, dil) in enumerate(DILATED_GROUPS):
        nk = win // dil
        for buf in caches[g]:
            assert buf.shape[1:] == (bsz, win, heads, ATT_DH)
            args.append(buf.reshape(buf.shape[0], bsz, nk, dil, heads, ATT_DH))
            in_specs.append(pl.BlockSpec((None, None, nk, None, heads, ATT_DH),
                                         lambda b: (layer, b, 0, 0, 0, 0)))
    y = pl.pallas_call(
        _dil_attn_step_body,
        grid=(bsz,),
        in_specs=in_specs,
        out_specs=pl.BlockSpec((1, heads, ATT_DH), lambda b: (b, 0, 0)),
        out_shape=jax.ShapeDtypeStruct((bsz, heads, ATT_DH), F32),
        compiler_params=_cparams("parallel"),
        name="dil_attn_step",
    )(*args)
    return y.reshape(bsz, heads * ATT_DH)


def _softmax_pv(s, v):
    m = jnp.max(s, axis=-1, keepdims=True)
    p = jnp.exp(s - m)
    l = jnp.sum(p, axis=-1, keepdims=True)
    return jnp.dot(p.astype(BF16), v.astype(BF16), preferred_element_type=F32) / l


def _xattn_block_body(xn_ref, r_ref, xs_ref, gq_ref, wq_ref, wo_ref, kv_ref, ks_ref, vs_ref, gn_ref,
                      o_ref, os_ref, xnn_ref, wqb_ref, wob_ref, att_ref, *, heads, nbat_s):
    scale = XA_DH ** -0.5
    hd = heads * XA_DH

    @pl.when(pl.program_id(0) == 0)
    def _():
        _cast_weight(wq_ref, wqb_ref, 0)
        _cast_weight(wo_ref, wob_ref, 0)
        xs = xs_ref[...]
        ms = jnp.mean(xs * xs, axis=-1, keepdims=True)
        xsn = (xs * lax.rsqrt(ms + EPS) * gq_ref[...]).astype(BF16)
        q_s = jnp.dot(xsn, wqb_ref[0], preferred_element_type=F32)
        att_ref[...] = jnp.zeros_like(att_ref)
        for b in range(nbat_s):
            for h in range(heads):
                cols = slice(h * XA_DH, (h + 1) * XA_DH)
                qb = _bf16_round(q_s[b:b + 1, cols])
                s = jnp.sum(_bf16_round(ks_ref[b, :, h, :]) * qb, axis=-1, keepdims=True) * scale
                p = jnp.exp(s - jnp.max(s, axis=0, keepdims=True))
                acc = jnp.sum(_bf16_round(p) * _bf16_round(vs_ref[b, :, h, :]), axis=0, keepdims=True)
                att_ref[b:b + 1, cols] = acc / jnp.sum(p, axis=0, keepdims=True)
        os_ref[...] = xs + jnp.dot(att_ref[0:xs.shape[0], :].astype(BF16), wob_ref[0], preferred_element_type=F32)

    q = jnp.dot(xn_ref[...], wqb_ref[0], preferred_element_type=F32)
    for h in range(heads):
        cols = slice(h * XA_DH, (h + 1) * XA_DH)
        k = kv_ref[0][:, cols]
        v = kv_ref[0][:, hd + h * XA_DH:hd + (h + 1) * XA_DH]
        s = lax.dot_general(q[:, cols].astype(BF16), k.astype(BF16), NT_DIMS, preferred_element_type=F32) * scale
        att_ref[:, cols] = _softmax_pv(s, v)
    x = r_ref[...] + jnp.dot(att_ref[...].astype(BF16), wob_ref[0], preferred_element_type=F32)
    o_ref[...] = x
    ms = jnp.mean(x * x, axis=-1, keepdims=True)
    xnn_ref[...] = (x * lax.rsqrt(ms + EPS) * gn_ref[...]).astype(BF16)


def _xattn_block(xp, xs, xn, gain, w_q, w_o, layer, mem_kv, mem_k_s, mem_v_s, next_gain, *, nbat, t, nbat_s,
                 tm=512):
    m, d = xp.shape
    rs = xs.shape[0]
    hd = w_q.shape[2]
    heads = hd // XA_DH
    mem_len = mem_kv.shape[1]
    tiles = t // tm
    assert t % tm == 0 and tm >= rs and mem_k_s.shape[1:] == (nbat_s, mem_len, heads, XA_DH)
    once = dict(pipeline_mode=pl.Buffered(1))
    mem_s_spec = pl.BlockSpec((None, nbat_s, mem_len, heads, XA_DH), lambda i: (layer, 0, 0, 0, 0), **once)
    row = lambda width: pl.BlockSpec((tm, width), lambda i: (i, 0))
    vec = pl.BlockSpec((1, d), lambda i: (0, 0))
    return pl.pallas_call(
        functools.partial(_xattn_block_body, heads=heads, nbat_s=nbat_s),
        grid=(m // tm,),
        in_specs=[row(d), row(d), pl.BlockSpec((rs, d), lambda i: (0, 0)), vec,
                  pl.BlockSpec((1, d, hd), lambda i: (layer, 0, 0), **once),
                  pl.BlockSpec((1, hd, d), lambda i: (layer, 0, 0), **once),
                  pl.BlockSpec((1, mem_len, 2 * hd), lambda i: (i // tiles, 0, 0)),
                  mem_s_spec, mem_s_spec, vec],
        out_specs=[row(d), pl.BlockSpec((rs, d), lambda i: (0, 0)), row(d)],
        out_shape=[jax.ShapeDtypeStruct((m, d), F32), jax.ShapeDtypeStruct((rs, d), F32),
                   jax.ShapeDtypeStruct((m, d), BF16)],
        scratch_shapes=[pltpu.VMEM((1, d, hd), BF16), pltpu.VMEM((1, hd, d), BF16), pltpu.VMEM((tm, hd), F32)],
        compiler_params=_cparams("arbitrary"),
        name="xattn_block",
    )(xn, xp, xs, gain.reshape(1, d), w_q, w_o, mem_kv, mem_k_s, mem_v_s, next_gain.reshape(1, d))


TN_COLS = 512


def _rope_tables(pos):
    rows = pos.shape[0]
    half = ROPE_DIM // 2
    inv_freq = ROPE_THETA ** (-jnp.arange(half, dtype=F32) * 2.0 / ROPE_DIM)
    ang = pos.astype(F32)[:, None] * inv_freq[None, :]
    cos, sin = jnp.cos(ang), jnp.sin(ang)
    rest = jnp.zeros((rows, LANES - ROPE_DIM), F32)
    zh = jnp.zeros((rows, half), F32)
    return (jnp.concatenate([cos, cos, rest + 1.0], axis=1), jnp.concatenate([-sin, zh, rest], axis=1),
            jnp.concatenate([zh, sin, rest], axis=1))


def _pad_rows(a, rows):
    return jnp.pad(a, ((0, rows - a.shape[0]), (0, 0)))


ROWS_S = 16
TM_ROWS = 512
TN_WIDE = 1024
TM_GATE_UP = 2048


def _ffn(xp, xs, xn, gain, w_gu, w_down, layer, tag, next_gain, norm_dtype=BF16):
    nf = w_down.shape[1] // TN_COLS
    (hid,), (hid_s,) = _proj(xn, xs, gain, w_gu, layer, w_col_blocks=(0, nf), n_col_blocks=nf,
                             epilogue=_ep_swiglu, out_dtypes=(BF16,), name=tag + "_gate_up", tm=TM_GATE_UP)
    return _res_proj_norm_big(hid, hid_s, w_down, layer, xp, xs, 0.5, next_gain, name=tag + "_down",
                              norm_dtype=norm_dtype)


def _conv_mixer(xp, xs, xn, gain, w_in, w_conv, w_out, layer, state, next_gain, *, nbat, t, nbat_s):
    d = xp.shape[1]
    nb = d // TN_COLS
    tm = min(1024, t)
    tiles = t // tm
    w_spec = pl.BlockSpec((CONV_W, TN_COLS), lambda j, i: (0, j))
    st_spec = pl.BlockSpec((ROWS_S, TN_COLS), lambda j, i: (0, j))
    tail_out = (jax.ShapeDtypeStruct((nbat * SUBLANES, d), F32),
                pl.BlockSpec((SUBLANES, TN_COLS), lambda j, i: (i // tiles, j)))
    (gated, tail), (gated_s, u_s) = _proj(
        xn, xs, gain, w_in, layer, w_col_blocks=(0, nb, 2 * nb), n_col_blocks=nb, name="conv_in", tm=tm,
        epilogue=functools.partial(_ep_conv_seq, tiles_per_seq=tiles), out_dtypes=(BF16,), more_outs=(tail_out,),
        epilogue_s=_ep_conv_step, out_s_dtypes=(BF16, F32),
        extra=(w_conv,), extra_specs=(w_spec,),
        extra_s=(_pad_rows(state[:, 0], ROWS_S), _pad_rows(state[:, 1], ROWS_S), w_conv),
        extra_s_specs=(st_spec, st_spec, w_spec),
        scratch=(pltpu.VMEM((SUBLANES, TN_COLS), F32), pltpu.VMEM((tm + SUBLANES, TN_COLS), F32)))
    state_p = tail.reshape(nbat, SUBLANES, d)[:, SUBLANES - (CONV_W - 1):]
    state_s = jnp.stack([state[:, 1], u_s[:nbat_s]], axis=1)
    xp, xs, xn = _res_proj_norm(gated, gated_s, w_out, layer, xp, xs, 1.0, next_gain, name="conv_out")
    return xp, xs, xn, state_p, state_s


def _hgrn_mixer(xp, xs, xn, gain, w_in, lower_bound, norm_gain, w_out, layer, state, next_gain, *, nbat, t, nbat_s):
    (proj,), (proj_s,) = _proj(xn, xs, gain, w_in, layer, w_col_blocks=(0,), n_col_blocks=w_in.shape[2] // TN_WIDE,
                               epilogue=_ep_plain, out_dtypes=(F32,), name="hgrn_in", tn=TN_WIDE)
    y, st = _hgrn_seq(proj, lower_bound, norm_gain, batch=nbat, seq_len=t)
    y_s, state_s = _hgrn_step(proj_s[:nbat_s], lower_bound, norm_gain, state)
    xp, xs, xn = _res_proj_norm(y, _pad_rows(y_s, ROWS_S).astype(BF16), w_out, layer, xp, xs, 1.0, next_gain,
                                name="hgrn_out")
    return xp, xs, xn, jnp.swapaxes(st, -1, -2), state_s


def _attn_mixer(xp, xs, xn, gain, w_qkv, w_out, layer, caches, next_gain, *, nbat, t, nbat_s):
    d = xp.shape[1]
    heads = d // ATT_DH
    n_sec = 3 * N_GROUPS
    sec_blocks = d // TN_WIDE
    tm = min(1024, t)
    tiles = t // tm
    tab_spec = pl.BlockSpec((tm, LANES), lambda j, i: (i % tiles, 0))
    tab_s_spec = pl.BlockSpec((ROWS_S, LANES), lambda j, i: (0, 0))
    (qkv,), (qkv_s,) = _proj(
        xn, xs, gain, w_qkv, layer, w_col_blocks=(0,), n_col_blocks=n_sec * sec_blocks,
        epilogue=_ep_rotary, out_dtypes=(F32,), name="attn_qkv", tm=tm, tn=TN_WIDE,
        alt_epilogue=_ep_plain, alt_pred=lambda j: (j // sec_blocks) % 3 == 2,
        extra=_rope_tables(jnp.arange(t)), extra_specs=(tab_spec,) * 3,
        extra_s=_rope_tables(jnp.full((ROWS_S,), PAST_LEN, jnp.int32)), extra_s_specs=(tab_s_spec,) * 3)
    y = _dil_attn_seq(qkv.reshape(nbat, t, n_sec * d), batch=nbat, seq_len=t, heads=heads).reshape(nbat * t, d)
    y_s = _dil_attn_step(qkv_s[:nbat_s], caches, layer, heads=heads)
    q5 = qkv.reshape(nbat, t, N_GROUPS, 3, heads, ATT_DH)
    q5_s = qkv_s[:nbat_s].reshape(nbat_s, 1, N_GROUPS, 3, heads, ATT_DH)
    rows_p, rows_s = [], []
    for g, (win, _) in enumerate(DILATED_GROUPS):
        keep = min(win, t)
        rows_p += [q5[:, t - keep:, g, 1], q5[:, t - keep:, g, 2]]
        rows_s += [q5_s[:, :, g, 1], q5_s[:, :, g, 2]]
    xp, xs, xn = _res_proj_norm(y, _pad_rows(y_s, ROWS_S).astype(BF16), w_out, layer, xp, xs, 1.0, next_gain,
                                name="attn_out")
    return xp, xs, xn, rows_p, rows_s


def kernel(x_prompt, x_sample, state_conv, state_hgrn,
           cache_win_k0, cache_win_v0, cache_win_k1, cache_win_v1, cache_win_k2, cache_win_v2,
           cache_mem_k, cache_mem_v, mem_prompt,
           norm_ffn1, ffn1_w_gu, ffn1_w_down, norm_mix,
           conv_w_in, conv_w, conv_w_out,
           hgrn_w_in, hgrn_lb_logits, hgrn_norm, hgrn_w_out,
           attn_w_qkv, attn_w_out,
           norm_mem, xattn_w_kv, norm_xattn, xattn_w_q, xattn_w_o,
           norm_ffn2, ffn2_w_gu, ffn2_w_down, norm_final):
    batch, seq, d = x_prompt.shape
    dec_batch, dec_seq, _ = x_sample.shape
    assert dec_seq == 1 and dec_batch <= ROWS_S
    depth = norm_ffn1.shape[0]
    mem_len = mem_prompt.shape[1]
    xa_hd = xattn_w_q.shape[2]
    xa_heads = xa_hd // XA_DH
    win_cache = [(cache_win_k0, cache_win_v0), (cache_win_k1, cache_win_v1), (cache_win_k2, cache_win_v2)]
    sizes = dict(nbat=batch, t=seq, nbat_s=dec_batch)

    lb_p = jax.nn.softmax(hgrn_lb_logits.astype(F32), axis=0)
    lower_bounds = jnp.cumsum(lb_p, axis=0) - lb_p[0]

    xp = x_prompt.reshape(batch * seq, d)
    xs = _pad_rows(x_sample.reshape(dec_batch, d), ROWS_S)
    mem = mem_prompt.reshape(batch * mem_len, d)
    p_conv, p_hgrn, p_win, p_mk, p_mv, s_conv, s_hgrn, s_win = ([] for _ in range(8))
    xn = _rms_norm(xp, norm_ffn1[0], tm=TM_ROWS, out_dtype=BF16, name="first_norm")
    for i in range(depth):
        j, kind = divmod(i, N_MIXERS)
        xp, xs, xn = _ffn(xp, xs, xn, norm_ffn1[i], ffn1_w_gu, ffn1_w_down, i, "ffn1", norm_mix[i])
        if kind == 0:
            xp, xs, xn, st_p, st_s = _conv_mixer(xp, xs, xn, norm_mix[i], conv_w_in, conv_w[j], conv_w_out, j,
                                                 state_conv[j], norm_xattn[i], **sizes)
            p_conv.append(st_p)
            s_conv.append(st_s)
        elif kind == 1:
            xp, xs, xn, st_p, st_s = _hgrn_mixer(xp, xs, xn, norm_mix[i], hgrn_w_in, lower_bounds[i], hgrn_norm[j],
                                                 hgrn_w_out, j, state_hgrn[j], norm_xattn[i], **sizes)
            p_hgrn.append(st_p)
            s_hgrn.append(st_s)
        else:
            xp, xs, xn, rows_p, rows_s = _attn_mixer(xp, xs, xn, norm_mix[i], attn_w_qkv, attn_w_out, j, win_cache,
                                                     norm_xattn[i], **sizes)
            p_win.append(rows_p)
            s_win.append(rows_s)
        mem_n = _rms_norm(mem, norm_mem[i], tm=TM_ROWS, out_dtype=BF16, name="mem_norm")
        (kv,), _ = _proj(mem_n, None, None, xattn_w_kv, i, w_col_blocks=(0,), n_col_blocks=2 * xa_hd // TN_COLS,
                         epilogue=_ep_plain, out_dtypes=(F32,), name="mem_kv")
        kv = kv.reshape(batch, mem_len, 2 * xa_hd)
        p_mk.append(kv[:, :, :xa_hd].reshape(batch, mem_len, xa_heads, XA_DH))
        p_mv.append(kv[:, :, xa_hd:].reshape(batch, mem_len, xa_heads, XA_DH))
        xp, xs, xn = _xattn_block(xp, xs, xn, norm_xattn[i], xattn_w_q, xattn_w_o, i, kv, cache_mem_k, cache_mem_v,
                                      norm_ffn2[i], **sizes)
        last = i == depth - 1
        xp, xs, xn = _ffn(xp, xs, xn, norm_ffn2[i], ffn2_w_gu, ffn2_w_down, i, "ffn2",
                          norm_final if last else norm_ffn1[i + 1], F32 if last else BF16)
    y_prompt = xn.reshape(batch, seq, d)
    y_sample = _rms_norm(xs, norm_final, tm=ROWS_S, out_dtype=F32, name="final_norm_s")[:dec_batch]
    y_sample = y_sample.reshape(dec_batch, 1, d)

    stack_win = lambda rows: [jnp.stack([r[g] for r in rows]) for g in range(2 * N_GROUPS)]
    return (y_prompt, y_sample,
            jnp.stack(p_conv), jnp.stack(p_hgrn), *stack_win(p_win), jnp.stack(p_mk), jnp.stack(p_mv),
            jnp.stack(s_conv), jnp.stack(s_hgrn), *stack_win(s_win))
```

```python
import functools

import jax
import jax.numpy as jnp
from jax import lax
from jax.experimental import pallas as pl
from jax.experimental.pallas import tpu as pltpu

F32 = jnp.float32
BF16 = jnp.bfloat16
EPS = 1e-6
LANES = 128
SUBLANES = 8
MXU_COLS = 256
VMEM_LIMIT_BYTES = 56 << 20
NEG_BIG = -1e30

N_MIXERS = 3
CONV_W = 3
HG_DK = 128
ATT_DH = 128
DILATED_GROUPS = ((128, 1), (512, 4), (2048, 16))
N_GROUPS = 3
ROPE_DIM = ATT_DH // 4
ROPE_THETA = 500000.0
XA_DH = 128
GLA_BLOCK = 16
PAST_LEN = 16384

NT_DIMS = (((1,), (1,)), ((), ()))
TN_DIMS = (((0,), (0,)), ((), ()))


def _cparams(*semantics):
    return pltpu.CompilerParams(dimension_semantics=semantics, vmem_limit_bytes=VMEM_LIMIT_BYTES)


def _sigmoid(x):
    return 1.0 / (1.0 + jnp.exp(-x))


def _rms_body(x_ref, g_ref, o_ref):
    x = x_ref[...]
    ms = jnp.mean(x * x, axis=-1, keepdims=True)
    o_ref[...] = (x * lax.rsqrt(ms + EPS) * g_ref[...]).astype(o_ref.dtype)


def _rms_norm(x, gain, *, tm, out_dtype, name):
    m, k = x.shape
    return pl.pallas_call(
        _rms_body,
        grid=(m // tm,),
        in_specs=[pl.BlockSpec((tm, k), lambda i: (i, 0)), pl.BlockSpec((1, k), lambda i: (0, 0))],
        out_specs=pl.BlockSpec((tm, k), lambda i: (i, 0)),
        out_shape=jax.ShapeDtypeStruct((m, k), out_dtype),
        compiler_params=_cparams("parallel"),
        name=name,
    )(x, gain.reshape(1, k))


CAST_ROWS = 256


def _cast_weight(w_ref, wb_ref, slot):
    def body(c, carry):
        rows = pl.ds(pl.multiple_of(c * CAST_ROWS, CAST_ROWS), CAST_ROWS)
        wb_ref[slot, rows, :] = w_ref[0, rows, :].astype(BF16)
        return carry
    lax.fori_loop(0, w_ref.shape[1] // CAST_ROWS, body, 0)


def _ep_plain(accs, extra, outs, col0):
    acc = accs[0]
    outs[0][:, col0:col0 + acc.shape[1]] = acc.astype(outs[0].dtype)


def _ep_swiglu(accs, extra, outs, col0):
    gate, up = accs
    outs[0][:, col0:col0 + gate.shape[1]] = (gate * _sigmoid(gate) * up).astype(BF16)


def _ep_conv_seq(accs, extra, outs, col0, scratch, *, tiles_per_seq):
    b_gate, c_gate, z = accs
    (w_ref,), (gated_ref, tail_ref), (carry_ref, ue_ref) = extra, outs, scratch
    tm, width = z.shape
    cols = slice(col0, col0 + width)
    ue = ue_ref.at[col0 // width]
    u = c_gate * z
    first = (pl.program_id(1) % tiles_per_seq) == 0
    ue[0:SUBLANES, :] = jnp.where(first, 0.0, carry_ref[:, cols])
    ue[SUBLANES:SUBLANES + tm, :] = u
    w = w_ref[:, cols]
    conv = (w[0:1] * ue[pl.ds(SUBLANES - 2, tm), :] + w[1:2] * ue[pl.ds(SUBLANES - 1, tm), :]
            + w[2:3] * u)
    gated_ref[:, cols] = (b_gate * conv).astype(BF16)
    tail = u[tm - SUBLANES:, :]
    carry_ref[:, cols] = tail
    tail_ref[:, cols] = tail


def _ep_conv_step(accs, extra, outs, col0):
    b_gate, c_gate, z = accs
    s0_ref, s1_ref, w_ref = extra
    u = c_gate * z
    w = w_ref[...]
    conv = w[0:1] * s0_ref[...] + w[1:2] * s1_ref[...] + w[2:3] * u
    outs[0][...] = (b_gate * conv).astype(BF16)
    outs[1][...] = u


def _ep_rotary(accs, extra, outs, col0):
    cos, sin_lo, sin_hi = (r[...] for r in extra)
    acc = accs[0]
    half = ROPE_DIM // 2
    for c in range(acc.shape[1] // LANES):
        xc = acc[:, c * LANES:(c + 1) * LANES]
        outs[0][:, col0 + c * LANES:col0 + (c + 1) * LANES] = (
            xc * cos + pltpu.roll(xc, LANES - half, 1) * sin_lo + pltpu.roll(xc, half, 1) * sin_hi)


def _proj_body(*refs, n_w, n_extra, n_extra_s, n_out, n_out_s, has_sample, epilogue, epilogue_s, col_chunk,
               alt_epilogue, alt_pred):
    refs = list(refs)
    take = lambda n: [refs.pop(0) for _ in range(n)]
    (x_ref,) = take(1)
    xs_ref, gs_ref = take(2) if has_sample else (None, None)
    w_refs = take(n_w)
    extra = take(n_extra)
    extra_s = take(n_extra_s)
    outs = take(n_out)
    outs_s = take(n_out_s)
    wb_ref, *scratch = refs

    @pl.when(pl.program_id(1) == 0)
    def _():
        for k, w_ref in enumerate(w_refs):
            _cast_weight(w_ref, wb_ref, k)
        if has_sample:
            xs = xs_ref[...]
            ms = jnp.mean(xs * xs, axis=-1, keepdims=True)
            xsn = (xs * lax.rsqrt(ms + EPS) * gs_ref[...]).astype(BF16)
            accs_s = [jnp.dot(xsn, wb_ref[k], preferred_element_type=F32) for k in range(n_w)]
            if alt_epilogue is None:
                epilogue_s(accs_s, extra_s, outs_s, 0)
            else:
                use_alt = alt_pred(pl.program_id(0))
                pl.when(use_alt)(lambda: alt_epilogue(accs_s, extra_s, outs_s, 0))
                pl.when(jnp.logical_not(use_alt))(lambda: epilogue_s(accs_s, extra_s, outs_s, 0))

    def tile(ep):
        x = x_ref[...]
        for c0 in range(0, wb_ref.shape[2], col_chunk):
            accs = [jnp.dot(x, wb_ref[k, :, c0:c0 + col_chunk], preferred_element_type=F32) for k in range(n_w)]
            ep(accs, extra, outs, c0, *([scratch] if scratch else []))

    if alt_epilogue is None:
        tile(epilogue)
    else:
        use_alt = alt_pred(pl.program_id(0))
        pl.when(use_alt)(lambda: tile(alt_epilogue))
        pl.when(jnp.logical_not(use_alt))(lambda: tile(epilogue))


def _proj(xn, xs, gain, w_stack, layer, *, w_col_blocks, n_col_blocks, epilogue, out_dtypes, name,
          tm=1024, tn=512, col_chunk=None, extra=(), extra_specs=(), extra_s=(), extra_s_specs=(),
          epilogue_s=None, out_s_dtypes=None, more_outs=(), scratch=(), alt_epilogue=None, alt_pred=None):
    m, k = xn.shape
    tm = min(tm, m)
    has_sample = xs is not None
    assert m % tm == 0 and w_stack.shape[1] == k and k % CAST_ROWS == 0
    out_s_dtypes = (out_dtypes if out_s_dtypes is None else out_s_dtypes) if has_sample else ()
    n_total = n_col_blocks * tn
    args = [xn]
    in_specs = [pl.BlockSpec((tm, k), lambda j, i: (i, 0))]
    if has_sample:
        rs = xs.shape[0]
        args += [xs, gain.reshape(1, k)]
        in_specs += [pl.BlockSpec((rs, k), lambda j, i: (0, 0)), pl.BlockSpec((1, k), lambda j, i: (0, 0))]
    for off in w_col_blocks:
        args.append(w_stack)
        in_specs.append(pl.BlockSpec((1, k, tn), lambda j, i, off=off: (layer, 0, j + off)))
    args += list(extra) + list(extra_s)
    in_specs += list(extra_specs) + list(extra_s_specs)
    out_shape = [jax.ShapeDtypeStruct((m, n_total), dt) for dt in out_dtypes] + [o[0] for o in more_outs]
    out_specs = [pl.BlockSpec((tm, tn), lambda j, i: (i, j)) for _ in out_dtypes] + [o[1] for o in more_outs]
    n_out = len(out_shape)
    if has_sample:
        out_shape += [jax.ShapeDtypeStruct((rs, n_total), dt) for dt in out_s_dtypes]
        out_specs += [pl.BlockSpec((rs, tn), lambda j, i: (0, j)) for _ in out_s_dtypes]
    res = pl.pallas_call(
        functools.partial(_proj_body, n_w=len(w_col_blocks), n_extra=len(extra), n_extra_s=len(extra_s),
                          n_out=n_out, n_out_s=len(out_s_dtypes), has_sample=has_sample,
                          epilogue=epilogue, epilogue_s=epilogue_s or epilogue, col_chunk=col_chunk or tn,
                          alt_epilogue=alt_epilogue, alt_pred=alt_pred),
        grid=(n_col_blocks, m // tm),
        in_specs=in_specs,
        out_specs=out_specs,
        out_shape=out_shape,
        scratch_shapes=[pltpu.VMEM((len(w_col_blocks), k, tn), BF16)] + list(scratch),
        compiler_params=_cparams("arbitrary", "arbitrary"),
        name=name,
    )(*args)
    return (res[:n_out], res[n_out:]) if has_sample else (res, None)


def _res_proj_norm_body(a_ref, as_ref, w_ref, r_ref, rs_ref, g_ref, o_ref, os_ref, xn_ref, wb_ref, *, scale):
    @pl.when(pl.program_id(0) == 0)
    def _():
        _cast_weight(w_ref, wb_ref, 0)
        os_ref[...] = rs_ref[...] + scale * jnp.dot(as_ref[...], wb_ref[0], preferred_element_type=F32)

    x = r_ref[...] + scale * jnp.dot(a_ref[...], wb_ref[0], preferred_element_type=F32)
    o_ref[...] = x
    ms = jnp.mean(x * x, axis=-1, keepdims=True)
    xn_ref[...] = (x * lax.rsqrt(ms + EPS) * g_ref[...]).astype(BF16)


def _res_proj_norm(a, a_s, w_stack, layer, res, res_s, scale, next_gain, *, name, tm=512):
    m, k = a.shape
    rs = a_s.shape[0]
    n = w_stack.shape[2]
    assert m % tm == 0 and k % CAST_ROWS == 0 and res.shape == (m, n)
    return pl.pallas_call(
        functools.partial(_res_proj_norm_body, scale=scale),
        grid=(m // tm,),
        in_specs=[pl.BlockSpec((tm, k), lambda i: (i, 0)),
                  pl.BlockSpec((rs, k), lambda i: (0, 0)),
                  pl.BlockSpec((1, k, n), lambda i: (layer, 0, 0), pipeline_mode=pl.Buffered(1)),
                  pl.BlockSpec((tm, n), lambda i: (i, 0)),
                  pl.BlockSpec((rs, n), lambda i: (0, 0)),
                  pl.BlockSpec((1, n), lambda i: (0, 0))],
        out_specs=[pl.BlockSpec((tm, n), lambda i: (i, 0)),
                   pl.BlockSpec((rs, n), lambda i: (0, 0)),
                   pl.BlockSpec((tm, n), lambda i: (i, 0))],
        out_shape=[jax.ShapeDtypeStruct((m, n), F32), jax.ShapeDtypeStruct((rs, n), F32),
                   jax.ShapeDtypeStruct((m, n), BF16)],
        scratch_shapes=[pltpu.VMEM((1, k, n), BF16)],
        compiler_params=_cparams("arbitrary"),
        name=name,
    )(a, a_s, w_stack, res, res_s, next_gain.reshape(1, n))


WEIGHT_CHUNK_ROWS = 512


def _res_proj_norm_big_body(a_ref, as_ref, w_hbm, r_ref, rs_ref, g_ref, o_ref, os_ref, xn_ref,
                            wb_ref, stage_ref, sem, *, scale, layer):
    @pl.when(pl.program_id(0) == 0)
    def _():
        chunk = stage_ref.shape[1]
        n_chunks = wb_ref.shape[0] // chunk

        def copy(c):
            return pltpu.make_async_copy(w_hbm.at[layer, pl.ds(c * chunk, chunk), :], stage_ref.at[c % 2],
                                         sem.at[c % 2])
        copy(0).start()
        for c in range(n_chunks):
            if c + 1 < n_chunks:
                copy(c + 1).start()
            copy(c).wait()
            wb_ref[pl.ds(c * chunk, chunk), :] = stage_ref[c % 2].astype(BF16)
        os_ref[...] = rs_ref[...] + scale * jnp.dot(as_ref[...], wb_ref[...], preferred_element_type=F32)

    x = r_ref[...] + scale * jnp.dot(a_ref[...], wb_ref[...], preferred_element_type=F32)
    o_ref[...] = x
    ms = jnp.mean(x * x, axis=-1, keepdims=True)
    xn_ref[...] = (x * lax.rsqrt(ms + EPS) * g_ref[...]).astype(xn_ref.dtype)


def _res_proj_norm_big(a, a_s, w_stack, layer, res, res_s, scale, next_gain, *, name, norm_dtype=BF16, tm=256):
    m, k = a.shape
    rs = a_s.shape[0]
    n = w_stack.shape[2]
    assert m % tm == 0 and k % WEIGHT_CHUNK_ROWS == 0 and res.shape == (m, n)
    return pl.pallas_call(
        functools.partial(_res_proj_norm_big_body, scale=scale, layer=layer),
        grid=(m // tm,),
        in_specs=[pl.BlockSpec((tm, k), lambda i: (i, 0)),
                  pl.BlockSpec((rs, k), lambda i: (0, 0)),
                  pl.BlockSpec(memory_space=pl.ANY),
                  pl.BlockSpec((tm, n), lambda i: (i, 0)),
                  pl.BlockSpec((rs, n), lambda i: (0, 0)),
                  pl.BlockSpec((1, n), lambda i: (0, 0))],
        out_specs=[pl.BlockSpec((tm, n), lambda i: (i, 0)),
                   pl.BlockSpec((rs, n), lambda i: (0, 0)),
                   pl.BlockSpec((tm, n), lambda i: (i, 0))],
        out_shape=[jax.ShapeDtypeStruct((m, n), F32), jax.ShapeDtypeStruct((rs, n), F32),
                   jax.ShapeDtypeStruct((m, n), norm_dtype)],
        scratch_shapes=[pltpu.VMEM((k, n), BF16), pltpu.VMEM((2, WEIGHT_CHUNK_ROWS, n), F32),
                        pltpu.SemaphoreType.DMA((2,))],
        compiler_params=_cparams("arbitrary"),
        name=name,
    )(a, a_s, w_stack, res, res_s, next_gain.reshape(1, n))


def _hgrn_seq_body(q_ref, z_ref, v_ref, g_ref, lb_ref, gn_ref, tri_ref, y_ref, st_out_ref,
                   st_ref, qs_ref, ks_ref, cum_ref, o_ref, *, hb):
    @pl.when(pl.program_id(2) == 0)
    def _():
        st_ref[...] = jnp.zeros_like(st_ref)

    tt = q_ref.shape[0]
    lb = lb_ref[...]
    z = z_ref[...]
    forget = lb + (1.0 - lb) * _sigmoid(z)
    ks_ref[...] = (1.0 - lb) * _sigmoid(-z)
    q = q_ref[...]
    qs_ref[...] = q * _sigmoid(q)
    cum_ref[...] = jnp.dot(tri_ref[...], jnp.log(forget), preferred_element_type=F32,
                           precision=lax.Precision.HIGHEST)

    t_idx = lax.broadcasted_iota(jnp.int32, (GLA_BLOCK, 1), 0)
    half = GLA_BLOCK // 2

    def block(j, carry):
        rows = pl.ds(pl.multiple_of(j * GLA_BLOCK, GLA_BLOCK), GLA_BLOCK)
        for h in range(hb):
            cols = slice(h * HG_DK, (h + 1) * HG_DK)
            cb = cum_ref[rows, cols]
            qb = qs_ref[rows, cols]
            kb = ks_ref[rows, cols]
            vb = v_ref[rows, cols]
            last = cb[GLA_BLOCK - 1:GLA_BLOCK, :]
            st = st_ref[h]
            o = lax.dot_general((qb * jnp.exp(cb)).astype(BF16), st.astype(BF16), NT_DIMS,
                                preferred_element_type=F32)
            o_lo = jnp.zeros((half, LANES), F32)
            o_hi = jnp.zeros((half, LANES), F32)
            for s in range(GLA_BLOCK):
                cs = cb[s:s + 1, :]
                qk_hi = qb[half:, :] * kb[s:s + 1, :]
                e_hi = jnp.exp(cb[half:, :] - cs)
                if s >= half:
                    e_hi = jnp.where(t_idx[half:] >= s, e_hi, 0.0)
                g_hi = e_hi * qk_hi
                o_hi = o_hi + jnp.sum(g_hi, axis=-1, keepdims=True) * vb[s:s + 1, :]
                if s < half:
                    qk_lo = qb[:half, :] * kb[s:s + 1, :]
                    g_lo = jnp.where(t_idx[:half] >= s, jnp.exp(cb[:half, :] - cs), 0.0) * qk_lo
                    o_lo = o_lo + jnp.sum(g_lo, axis=-1, keepdims=True) * vb[s:s + 1, :]
            o_ref[rows, cols] = o + jnp.concatenate([o_lo, o_hi], axis=0)
            kd = kb * jnp.exp(last - cb)
            upd = lax.dot_general(vb.astype(BF16), kd.astype(BF16), TN_DIMS, preferred_element_type=F32)
            st_ref[h] = st * jnp.exp(last) + upd
        return carry

    lax.fori_loop(0, tt // GLA_BLOCK, block, 0)

    for h in range(hb):
        cols = slice(h * HG_DK, (h + 1) * HG_DK)
        o = o_ref[:, cols]
        ms = jnp.mean(o * o, axis=-1, keepdims=True)
        g = g_ref[:, cols]
        y_ref[:, cols] = (o * lax.rsqrt(ms + EPS) * gn_ref[0, h:h + 1, :] * (g * _sigmoid(g))).astype(BF16)

    @pl.when(pl.program_id(2) == pl.num_programs(2) - 1)
    def _():
        st_out_ref[0] = st_ref[...]


def _hgrn_seq(proj, lower_bound, norm_gain, *, batch, seq_len, tt=256, hb=8):
    m, n4 = proj.shape
    d = n4 // 4
    heads = d // HG_DK
    hg = heads // hb
    nt = seq_len // tt
    wcols = hb * HG_DK
    tri = (jnp.arange(tt)[:, None] >= jnp.arange(tt)[None, :]) & (
        jnp.arange(tt)[:, None] // GLA_BLOCK == jnp.arange(tt)[None, :] // GLA_BLOCK)
    tri = tri.astype(F32)

    def col(sec):
        return pl.BlockSpec((tt, wcols), lambda b, h, t, sec=sec: (b * nt + t, sec * hg + h))

    y, st = pl.pallas_call(
        functools.partial(_hgrn_seq_body, hb=hb),
        grid=(batch, hg, nt),
        in_specs=[col(0), col(1), col(2), col(3),
                  pl.BlockSpec((1, wcols), lambda b, h, t: (0, h)),
                  pl.BlockSpec((1, hb, HG_DK), lambda b, h, t: (h, 0, 0)),
                  pl.BlockSpec((tt, tt), lambda b, h, t: (0, 0))],
        out_specs=[pl.BlockSpec((tt, wcols), lambda b, h, t: (b * nt + t, h)),
                   pl.BlockSpec((1, hb, HG_DK, HG_DK), lambda b, h, t: (b, h, 0, 0))],
        out_shape=[jax.ShapeDtypeStruct((m, d), BF16),
                   jax.ShapeDtypeStruct((batch, heads, HG_DK, HG_DK), F32)],
        scratch_shapes=[pltpu.VMEM((hb, HG_DK, HG_DK), F32), pltpu.VMEM((tt, wcols), F32),
                        pltpu.VMEM((tt, wcols), F32), pltpu.VMEM((tt, wcols), F32),
                        pltpu.VMEM((tt, wcols), F32)],
        compiler_params=_cparams("parallel", "parallel", "arbitrary"),
        name="hgrn_seq",
    )(proj, proj, proj, proj, lower_bound.reshape(1, d), norm_gain.reshape(hg, hb, HG_DK), tri)
    return y, st


def _hgrn_step_body(q_ref, z_ref, v_ref, g_ref, lb_ref, gn_ref, s_ref, y_ref, s_out_ref):
    heads = s_ref.shape[1]
    for h in range(heads):
        lb = lb_ref[h]
        z = z_ref[0, h]
        forget = lb + (1.0 - lb) * _sigmoid(z)
        k = (1.0 - lb) * _sigmoid(-z)
        q = q_ref[0, h]
        q = q * _sigmoid(q)
        s_new = forget * s_ref[0, h] + k * v_ref[0, h]
        s_out_ref[0, h] = s_new
        o = jnp.sum(q * s_new, axis=0, keepdims=True)
        ms = jnp.mean(o * o, axis=-1, keepdims=True)
        g = g_ref[0, h]
        y_ref[0, h] = o * lax.rsqrt(ms + EPS) * gn_ref[h] * (g * _sigmoid(g))


def _hgrn_step(proj, lower_bound, norm_gain, state):
    bsz, heads, dk, dv = state.shape
    d = heads * dk
    q, z, v, g = (proj[:, s * d:(s + 1) * d] for s in range(4))
    col = lambda a: a.reshape(bsz, heads, dk, 1)
    row = lambda a: a.reshape(bsz, heads, 1, dv)
    col_spec = pl.BlockSpec((1, heads, dk, 1), lambda b: (b, 0, 0, 0))
    row_spec = pl.BlockSpec((1, heads, 1, dv), lambda b: (b, 0, 0, 0))
    st_spec = pl.BlockSpec((1, heads, dk, dv), lambda b: (b, 0, 0, 0))
    y, s_new = pl.pallas_call(
        _hgrn_step_body,
        grid=(bsz,),
        in_specs=[col_spec, col_spec, row_spec, row_spec,
                  pl.BlockSpec((heads, dk, 1), lambda b: (0, 0, 0)),
                  pl.BlockSpec((heads, 1, dv), lambda b: (0, 0, 0)),
                  st_spec],
        out_specs=[row_spec, st_spec],
        out_shape=[jax.ShapeDtypeStruct((bsz, heads, 1, dv), F32),
                   jax.ShapeDtypeStruct(state.shape, F32)],
        compiler_params=_cparams("parallel"),
        name="hgrn_step",
    )(col(q), col(z), row(v), row(g), lower_bound.reshape(heads, dk, 1),
      norm_gain.reshape(heads, 1, dv), state)
    return y.reshape(bsz, d), s_new


def _band_block(q, k, v, scale, has_prev):
    n, nkeys = q.shape[0], k.shape[0]
    qi = lax.broadcasted_iota(jnp.int32, (n, nkeys), 0)
    ki = lax.broadcasted_iota(jnp.int32, (n, nkeys), 1)
    s = lax.dot_general(q.astype(BF16), k.astype(BF16), NT_DIMS, preferred_element_type=F32) * scale
    visible = ((ki >= qi) & (ki <= qi + n)) if has_prev else (ki <= qi)
    s = jnp.where(visible, s, NEG_BIG)
    m = jnp.max(s, axis=-1, keepdims=True)
    p = jnp.exp(s - m)
    l = jnp.sum(p, axis=-1, keepdims=True)
    acc = jnp.dot(p.astype(BF16), v.astype(BF16), preferred_element_type=F32)
    return acc / l, m + jnp.log(l)


def _dil_attn_seq_body(*refs, seq_len, hb):
    n_in = 3 * N_GROUPS
    y_ref, og_ref, lse_ref = refs[hb * n_in:]
    scale = ATT_DH ** -0.5
    for g, (win, dil) in enumerate(DILATED_GROUPS):
        nk = win // dil
        for r in range(dil):
            for n in range(seq_len // dil // nk):
                def rows(first_blk, n_blk):
                    start, size = r + dil * nk * first_blk, nk * n_blk
                    return pl.ds(start, size, stride=dil) if dil > 1 else pl.ds(start, size)
                cur = rows(n, 1)
                keys = rows(n - 1, 2) if n else cur
                for h in range(hb):
                    q_ref, k_ref, v_ref = refs[h * n_in + 3 * g:h * n_in + 3 * g + 3]
                    o, lse = _band_block(q_ref[0, cur, :], k_ref[0, keys, :], v_ref[0, keys, :], scale, n > 0)
                    og_ref[h * N_GROUPS + g, cur, :] = o
                    lse_ref[h * N_GROUPS + g, cur, :] = jnp.broadcast_to(lse, (nk, LANES))
    for h in range(hb):
        lse = [lse_ref[h * N_GROUPS + g] for g in range(N_GROUPS)]
        og = [og_ref[h * N_GROUPS + g] for g in range(N_GROUPS)]
        m = jnp.maximum(jnp.maximum(lse[0], lse[1]), lse[2])
        w = [jnp.exp(x - m) for x in lse]
        y = (w[0] * og[0] + w[1] * og[1] + w[2] * og[2]) / (w[0] + w[1] + w[2])
        y_ref[0, :, h * ATT_DH:(h + 1) * ATT_DH] = y.astype(BF16)


def _dil_attn_seq(qkv, *, batch, seq_len, heads, hb=1):
    in_specs = []
    for h in range(hb):
        for s in range(3 * N_GROUPS):
            in_specs.append(pl.BlockSpec((1, seq_len, ATT_DH),
                                         lambda b, hg, s=s, h=h: (b, 0, s * heads + hg * hb + h)))
    return pl.pallas_call(
        functools.partial(_dil_attn_seq_body, seq_len=seq_len, hb=hb),
        grid=(batch, heads // hb),
        in_specs=in_specs,
        out_specs=pl.BlockSpec((1, seq_len, hb * ATT_DH), lambda b, hg: (b, 0, hg)),
        out_shape=jax.ShapeDtypeStruct((batch, seq_len, heads * ATT_DH), BF16),
        scratch_shapes=[pltpu.VMEM((hb * N_GROUPS, seq_len, ATT_DH), F32),
                        pltpu.VMEM((hb * N_GROUPS, seq_len, LANES), F32)],
        compiler_params=_cparams("parallel", "parallel"),
        name="dil_attn_seq",
    )(*([qkv] * (hb * 3 * N_GROUPS)))


def _bf16_round(x):
    return x.astype(BF16).astype(F32)


def _step_attention(q, k_new, v_new, k_c, v_c, scale):
    qb = _bf16_round(q)
    s_c = jnp.sum(_bf16_round(k_c) * qb[None], axis=-1, keepdims=True) * scale
    m = jnp.max(s_c, axis=0)
    if k_new is not None:
        s_n = jnp.sum(qb * _bf16_round(k_new), axis=-1, keepdims=True) * scale
        m = jnp.maximum(m, s_n)
    p_c = jnp.exp(s_c - m[None])
    l = jnp.sum(p_c, axis=0)
    acc = jnp.sum(_bf16_round(p_c) * _bf16_round(v_c), axis=0)
    if k_new is not None:
        p_n = jnp.exp(s_n - m)
        l = l + p_n
        acc = acc + _bf16_round(p_n) * _bf16_round(v_new)
    return acc / l, m + jnp.log(l)


def _dil_attn_step_body(*refs):
    new = refs[:3 * N_GROUPS]
    caches = refs[3 * N_GROUPS:5 * N_GROUPS]
    y_ref = refs[5 * N_GROUPS]
    scale = ATT_DH ** -0.5
    outs, lses = [], []
    for g in range(N_GROUPS):
        o, lse = _step_attention(new[3 * g][0, 0], new[3 * g + 1][0, 0], new[3 * g + 2][0, 0],
                                 caches[2 * g][...], caches[2 * g + 1][...], scale)
        outs.append(o)
        lses.append(lse)
    m = jnp.maximum(jnp.maximum(lses[0], lses[1]), lses[2])
    w = [jnp.exp(x - m) for x in lses]
    y_ref[0] = (w[0] * outs[0] + w[1] * outs[1] + w[2] * outs[2]) / (w[0] + w[1] + w[2])


def _dil_attn_step(qkv, caches, layer, *, heads):
    bsz = qkv.shape[0]
    new = qkv.reshape(bsz, 3 * N_GROUPS, heads, ATT_DH)
    args, in_specs = [], []
    for s in range(3 * N_GROUPS):
        args.append(new)
        in_specs.append(pl.BlockSpec((1, 1, heads, ATT_DH), lambda b, s=s: (b, s, 0, 0)))
    for g, (win, dil) in enumerate(DILATED_GROUPS):
        nk = win // dil
        for buf in caches[g]:
            assert buf.shape[1:] == (bsz, win, heads, ATT_DH)
            args.append(buf.reshape(buf.shape[0], bsz, nk, dil, heads, ATT_DH))
            in_specs.append(pl.BlockSpec((None, None, nk, None, heads, ATT_DH),
                                         lambda b: (layer, b, 0, 0, 0, 0)))
    y = pl.pallas_call(
        _dil_attn_step_body,
        grid=(bsz,),
        in_specs=in_specs,
        out_specs=pl.BlockSpec((1, heads, ATT_DH), lambda b: (b, 0, 0)),
        out_shape=jax.ShapeDtypeStruct((bsz, heads, ATT_DH), F32),
        compiler_params=_cparams("parallel"),
        name="dil_attn_step",
    )(*args)
    return y.reshape(bsz, heads * ATT_DH)


def _softmax_pv(s, v):
    m = jnp.max(s, axis=-1, keepdims=True)
    p = jnp.exp(s - m)
    l = jnp.sum(p, axis=-1, keepdims=True)
    return jnp.dot(p.astype(BF16), v.astype(BF16), preferred_element_type=F32) / l


def _memory_kv_body(x_ref, g_ref, w_ref, o_ref, wb_ref, xn_ref):
    _cast_weight(w_ref, wb_ref, 0)

    def body(c, carry):
        rows = pl.ds(pl.multiple_of(c * CAST_ROWS, CAST_ROWS), CAST_ROWS)
        x = x_ref[rows, :]
        ms = jnp.mean(x * x, axis=-1, keepdims=True)
        xn_ref[rows, :] = (x * lax.rsqrt(ms + EPS) * g_ref[0]).astype(BF16)
        return carry
    lax.fori_loop(0, x_ref.shape[0] // CAST_ROWS, body, 0)
    o_ref[0] = jnp.dot(xn_ref[...], wb_ref[0], preferred_element_type=F32)


def _memory_kv(mem, gains, w_kv, *, tn=512):
    rows, d = mem.shape
    layers, _, n = w_kv.shape
    assert rows % CAST_ROWS == 0 and d % CAST_ROWS == 0 and n % tn == 0
    return pl.pallas_call(
        _memory_kv_body,
        grid=(layers, n // tn),
        in_specs=[pl.BlockSpec((rows, d), lambda l, j: (0, 0)),
                  pl.BlockSpec((1, 1, d), lambda l, j: (l, 0, 0)),
                  pl.BlockSpec((1, d, tn), lambda l, j: (l, 0, j))],
        out_specs=pl.BlockSpec((1, rows, tn), lambda l, j: (l, 0, j)),
        out_shape=jax.ShapeDtypeStruct((layers, rows, n), F32),
        scratch_shapes=[pltpu.VMEM((1, d, tn), BF16), pltpu.VMEM((rows, d), BF16)],
        compiler_params=_cparams("parallel", "parallel"),
        name="memory_kv",
    )(mem, gains.reshape(layers, 1, d), w_kv)


def _xattn_block_body(xn_ref, r_ref, xs_ref, gq_ref, wq_ref, wo_ref, kv_ref, ks_ref, vs_ref, gn_ref,
                      o_ref, os_ref, xnn_ref, wqb_ref, wob_ref, att_ref, *, heads, nbat_s):
    scale = XA_DH ** -0.5
    hd = heads * XA_DH

    @pl.when(pl.program_id(0) == 0)
    def _():
        _cast_weight(wq_ref, wqb_ref, 0)
        _cast_weight(wo_ref, wob_ref, 0)
        xs = xs_ref[...]
        ms = jnp.mean(xs * xs, axis=-1, keepdims=True)
        xsn = (xs * lax.rsqrt(ms + EPS) * gq_ref[...]).astype(BF16)
        q_s = jnp.dot(xsn, wqb_ref[0], preferred_element_type=F32)
        att_ref[...] = jnp.zeros_like(att_ref)
        for b in range(nbat_s):
            for h in range(heads):
                cols = slice(h * XA_DH, (h + 1) * XA_DH)
                qb = _bf16_round(q_s[b:b + 1, cols])
                s = jnp.sum(_bf16_round(ks_ref[b, :, h, :]) * qb, axis=-1, keepdims=True) * scale
                p = jnp.exp(s - jnp.max(s, axis=0, keepdims=True))
                acc = jnp.sum(_bf16_round(p) * _bf16_round(vs_ref[b, :, h, :]), axis=0, keepdims=True)
                att_ref[b:b + 1, cols] = acc / jnp.sum(p, axis=0, keepdims=True)
        os_ref[...] = xs + jnp.dot(att_ref[0:xs.shape[0], :].astype(BF16), wob_ref[0], preferred_element_type=F32)

    q = jnp.dot(xn_ref[...], wqb_ref[0], preferred_element_type=F32)
    for h in range(heads):
        cols = slice(h * XA_DH, (h + 1) * XA_DH)
        k = kv_ref[0][:, cols]
        v = kv_ref[0][:, hd + h * XA_DH:hd + (h + 1) * XA_DH]
        s = lax.dot_general(q[:, cols].astype(BF16), k.astype(BF16), NT_DIMS, preferred_element_type=F32) * scale
        att_ref[:, cols] = _softmax_pv(s, v)
    x = r_ref[...] + jnp.dot(att_ref[...].astype(BF16), wob_ref[0], preferred_element_type=F32)
    o_ref[...] = x
    ms = jnp.mean(x * x, axis=-1, keepdims=True)
    xnn_ref[...] = (x * lax.rsqrt(ms + EPS) * gn_ref[...]).astype(BF16)


def _xattn_block(xp, xs, xn, gain, w_q, w_o, layer, mem_kv, mem_k_s, mem_v_s, next_gain, *, nbat, t, nbat_s,
                 tm=512):
    m, d = xp.shape
    rs = xs.shape[0]
    hd = w_q.shape[2]
    heads = hd // XA_DH
    mem_len = mem_kv.shape[2]
    tiles = t // tm
    assert t % tm == 0 and tm >= rs and mem_k_s.shape[1:] == (nbat_s, mem_len, heads, XA_DH)
    once = dict(pipeline_mode=pl.Buffered(1))
    mem_s_spec = pl.BlockSpec((None, nbat_s, mem_len, heads, XA_DH), lambda i: (layer, 0, 0, 0, 0), **once)
    row = lambda width: pl.BlockSpec((tm, width), lambda i: (i, 0))
    vec = pl.BlockSpec((1, d), lambda i: (0, 0))
    return pl.pallas_call(
        functools.partial(_xattn_block_body, heads=heads, nbat_s=nbat_s),
        grid=(m // tm,),
        in_specs=[row(d), row(d), pl.BlockSpec((rs, d), lambda i: (0, 0)), vec,
                  pl.BlockSpec((1, d, hd), lambda i: (layer, 0, 0), **once),
                  pl.BlockSpec((1, hd, d), lambda i: (layer, 0, 0), **once),
                  pl.BlockSpec((None, 1, mem_len, 2 * hd), lambda i: (layer, i // tiles, 0, 0)),
                  mem_s_spec, mem_s_spec, vec],
        out_specs=[row(d), pl.BlockSpec((rs, d), lambda i: (0, 0)), row(d)],
        out_shape=[jax.ShapeDtypeStruct((m, d), F32), jax.ShapeDtypeStruct((rs, d), F32),
                   jax.ShapeDtypeStruct((m, d), BF16)],
        scratch_shapes=[pltpu.VMEM((1, d, hd), BF16), pltpu.VMEM((1, hd, d), BF16), pltpu.VMEM((tm, hd), F32)],
        compiler_params=_cparams("arbitrary"),
        name="xattn_block",
    )(xn, xp, xs, gain.reshape(1, d), w_q, w_o, mem_kv, mem_k_s, mem_v_s, next_gain.reshape(1, d))


TN_COLS = 512


def _rope_tables(pos):
    rows = pos.shape[0]
    half = ROPE_DIM // 2
    inv_freq = ROPE_THETA ** (-jnp.arange(half, dtype=F32) * 2.0 / ROPE_DIM)
    ang = pos.astype(F32)[:, None] * inv_freq[None, :]
    cos, sin = jnp.cos(ang), jnp.sin(ang)
    rest = jnp.zeros((rows, LANES - ROPE_DIM), F32)
    zh = jnp.zeros((rows, half), F32)
    return (jnp.concatenate([cos, cos, rest + 1.0], axis=1), jnp.concatenate([-sin, zh, rest], axis=1),
            jnp.concatenate([zh, sin, rest], axis=1))


def _pad_rows(a, rows):
    return jnp.pad(a, ((0, rows - a.shape[0]), (0, 0)))


ROWS_S = 16
TM_ROWS = 512
TN_WIDE = 1024
TM_GATE_UP = 2048


def _ffn(xp, xs, xn, gain, w_gu, w_down, layer, tag, next_gain, norm_dtype=BF16):
    nf = w_down.shape[1] // TN_COLS
    (hid,), (hid_s,) = _proj(xn, xs, gain, w_gu, layer, w_col_blocks=(0, nf), n_col_blocks=nf,
                             epilogue=_ep_swiglu, out_dtypes=(BF16,), name=tag + "_gate_up", tm=TM_GATE_UP)
    return _res_proj_norm_big(hid, hid_s, w_down, layer, xp, xs, 0.5, next_gain, name=tag + "_down",
                              norm_dtype=norm_dtype)


def _conv_mixer(xp, xs, xn, gain, w_in, w_conv, w_out, layer, state, next_gain, *, nbat, t, nbat_s):
    d = xp.shape[1]
    nb = d // TN_COLS
    tm = min(1024, t)
    tiles = t // tm
    w_spec = pl.BlockSpec((CONV_W, TN_COLS), lambda j, i: (0, j))
    st_spec = pl.BlockSpec((ROWS_S, TN_COLS), lambda j, i: (0, j))
    tail_out = (jax.ShapeDtypeStruct((nbat * SUBLANES, d), F32),
                pl.BlockSpec((SUBLANES, TN_COLS), lambda j, i: (i // tiles, j)))
    (gated, tail), (gated_s, u_s) = _proj(
        xn, xs, gain, w_in, layer, w_col_blocks=(0, nb, 2 * nb), n_col_blocks=nb, name="conv_in", tm=tm,
        epilogue=functools.partial(_ep_conv_seq, tiles_per_seq=tiles), out_dtypes=(BF16,), more_outs=(tail_out,),
        epilogue_s=_ep_conv_step, out_s_dtypes=(BF16, F32),
        extra=(w_conv,), extra_specs=(w_spec,),
        extra_s=(_pad_rows(state[:, 0], ROWS_S), _pad_rows(state[:, 1], ROWS_S), w_conv),
        extra_s_specs=(st_spec, st_spec, w_spec),
        col_chunk=MXU_COLS,
        scratch=(pltpu.VMEM((SUBLANES, TN_COLS), F32),
                 pltpu.VMEM((TN_COLS // MXU_COLS, tm + SUBLANES, MXU_COLS), F32)))
    state_p = tail.reshape(nbat, SUBLANES, d)[:, SUBLANES - (CONV_W - 1):]
    state_s = jnp.stack([state[:, 1], u_s[:nbat_s]], axis=1)
    xp, xs, xn = _res_proj_norm(gated, gated_s, w_out, layer, xp, xs, 1.0, next_gain, name="conv_out")
    return xp, xs, xn, state_p, state_s


def _hgrn_mixer(xp, xs, xn, gain, w_in, lower_bound, norm_gain, w_out, layer, state, next_gain, *, nbat, t, nbat_s):
    (proj,), (proj_s,) = _proj(xn, xs, gain, w_in, layer, w_col_blocks=(0,), n_col_blocks=w_in.shape[2] // TN_WIDE,
                               epilogue=_ep_plain, out_dtypes=(F32,), name="hgrn_in", tn=TN_WIDE)
    y, st = _hgrn_seq(proj, lower_bound, norm_gain, batch=nbat, seq_len=t)
    y_s, state_s = _hgrn_step(proj_s[:nbat_s], lower_bound, norm_gain, state)
    xp, xs, xn = _res_proj_norm(y, _pad_rows(y_s, ROWS_S).astype(BF16), w_out, layer, xp, xs, 1.0, next_gain,
                                name="hgrn_out")
    return xp, xs, xn, jnp.swapaxes(st, -1, -2), state_s


def _attn_mixer(xp, xs, xn, gain, w_qkv, w_out, layer, caches, next_gain, *, nbat, t, nbat_s):
    d = xp.shape[1]
    heads = d // ATT_DH
    n_sec = 3 * N_GROUPS
    sec_blocks = d // TN_WIDE
    tm = min(1024, t)
    tiles = t // tm
    tab_spec = pl.BlockSpec((tm, LANES), lambda j, i: (i % tiles, 0))
    tab_s_spec = pl.BlockSpec((ROWS_S, LANES), lambda j, i: (0, 0))
    (qkv,), (qkv_s,) = _proj(
        xn, xs, gain, w_qkv, layer, w_col_blocks=(0,), n_col_blocks=n_sec * sec_blocks,
        epilogue=_ep_rotary, out_dtypes=(F32,), name="attn_qkv", tm=tm, tn=TN_WIDE,
        alt_epilogue=_ep_plain, alt_pred=lambda j: (j // sec_blocks) % 3 == 2,
        extra=_rope_tables(jnp.arange(t)), extra_specs=(tab_spec,) * 3,
        extra_s=_rope_tables(jnp.full((ROWS_S,), PAST_LEN, jnp.int32)), extra_s_specs=(tab_s_spec,) * 3)
    y = _dil_attn_seq(qkv.reshape(nbat, t, n_sec * d), batch=nbat, seq_len=t, heads=heads).reshape(nbat * t, d)
    y_s = _dil_attn_step(qkv_s[:nbat_s], caches, layer, heads=heads)
    q5 = qkv.reshape(nbat, t, N_GROUPS, 3, heads, ATT_DH)
    q5_s = qkv_s[:nbat_s].reshape(nbat_s, 1, N_GROUPS, 3, heads, ATT_DH)
    rows_p, rows_s = [], []
    for g, (win, _) in enumerate(DILATED_GROUPS):
        keep = min(win, t)
        rows_p += [q5[:, t - keep:, g, 1], q5[:, t - keep:, g, 2]]
        rows_s += [q5_s[:, :, g, 1], q5_s[:, :, g, 2]]
    xp, xs, xn = _res_proj_norm(y, _pad_rows(y_s, ROWS_S).astype(BF16), w_out, layer, xp, xs, 1.0, next_gain,
                                name="attn_out")
    return xp, xs, xn, rows_p, rows_s


def kernel(x_prompt, x_sample, state_conv, state_hgrn,
           cache_win_k0, cache_win_v0, cache_win_k1, cache_win_v1, cache_win_k2, cache_win_v2,
           cache_mem_k, cache_mem_v, mem_prompt,
           norm_ffn1, ffn1_w_gu, ffn1_w_down, norm_mix,
           conv_w_in, conv_w, conv_w_out,
           hgrn_w_in, hgrn_lb_logits, hgrn_norm, hgrn_w_out,
           attn_w_qkv, attn_w_out,
           norm_mem, xattn_w_kv, norm_xattn, xattn_w_q, xattn_w_o,
           norm_ffn2, ffn2_w_gu, ffn2_w_down, norm_final):
    batch, seq, d = x_prompt.shape
    dec_batch, dec_seq, _ = x_sample.shape
    assert dec_seq == 1 and dec_batch <= ROWS_S
    depth = norm_ffn1.shape[0]
    mem_len = mem_prompt.shape[1]
    xa_hd = xattn_w_q.shape[2]
    xa_heads = xa_hd // XA_DH
    win_cache = [(cache_win_k0, cache_win_v0), (cache_win_k1, cache_win_v1), (cache_win_k2, cache_win_v2)]
    sizes = dict(nbat=batch, t=seq, nbat_s=dec_batch)

    lb_p = jax.nn.softmax(hgrn_lb_logits.astype(F32), axis=0)
    lower_bounds = jnp.cumsum(lb_p, axis=0) - lb_p[0]

    xp = x_prompt.reshape(batch * seq, d)
    xs = _pad_rows(x_sample.reshape(dec_batch, d), ROWS_S)
    mem_kv = _memory_kv(mem_prompt.reshape(batch * mem_len, d), norm_mem, xattn_w_kv)
    mem_kv = mem_kv.reshape(depth, batch, mem_len, 2 * xa_hd)
    p_mem_k = mem_kv[..., :xa_hd].reshape(depth, batch, mem_len, xa_heads, XA_DH)
    p_mem_v = mem_kv[..., xa_hd:].reshape(depth, batch, mem_len, xa_heads, XA_DH)
    p_conv, p_hgrn, p_win, s_conv, s_hgrn, s_win = ([] for _ in range(6))
    xn = _rms_norm(xp, norm_ffn1[0], tm=TM_ROWS, out_dtype=BF16, name="first_norm")
    for i in range(depth):
        j, kind = divmod(i, N_MIXERS)
        xp, xs, xn = _ffn(xp, xs, xn, norm_ffn1[i], ffn1_w_gu, ffn1_w_down, i, "ffn1", norm_mix[i])
        if kind == 0:
            xp, xs, xn, st_p, st_s = _conv_mixer(xp, xs, xn, norm_mix[i], conv_w_in, conv_w[j], conv_w_out, j,
                                                 state_conv[j], norm_xattn[i], **sizes)
            p_conv.append(st_p)
            s_conv.append(st_s)
        elif kind == 1:
            xp, xs, xn, st_p, st_s = _hgrn_mixer(xp, xs, xn, norm_mix[i], hgrn_w_in, lower_bounds[i], hgrn_norm[j],
                                                 hgrn_w_out, j, state_hgrn[j], norm_xattn[i], **sizes)
            p_hgrn.append(st_p)
            s_hgrn.append(st_s)
        else:
            xp, xs, xn, rows_p, rows_s = _attn_mixer(xp, xs, xn, norm_mix[i], attn_w_qkv, attn_w_out, j, win_cache,
                                                     norm_xattn[i], **sizes)
            p_win.append(rows_p)
            s_win.append(rows_s)
        xp, xs, xn = _xattn_block(xp, xs, xn, norm_xattn[i], xattn_w_q, xattn_w_o, i, mem_kv, cache_mem_k,
                                  cache_mem_v, norm_ffn2[i], **sizes)
        last = i == depth - 1
        xp, xs, xn = _ffn(xp, xs, xn, norm_ffn2[i], ffn2_w_gu, ffn2_w_down, i, "ffn2",
                          norm_final if last else norm_ffn1[i + 1], F32 if last else BF16)
    y_prompt = xn.reshape(batch, seq, d)
    y_sample = _rms_norm(xs, norm_final, tm=ROWS_S, out_dtype=F32, name="final_norm_s")[:dec_batch]
    y_sample = y_sample.reshape(dec_batch, 1, d)

    stack_win = lambda rows: [jnp.stack([r[g] for r in rows]) for g in range(2 * N_GROUPS)]
    return (y_prompt, y_sample,
            jnp.stack(p_conv), jnp.stack(p_hgrn), *stack_win(p_win), p_mem_k, p_mem_v,
            jnp.stack(s_conv), jnp.stack(s_hgrn), *stack_win(s_win))
```

```python
import functools

import jax
import jax.numpy as jnp
from jax import lax
from jax.experimental import pallas as pl
from jax.experimental.pallas import tpu as pltpu

F32 = jnp.float32
BF16 = jnp.bfloat16
EPS = 1e-6
LANES = 128
SUBLANES = 8
MXU_COLS = 256
VMEM_LIMIT_BYTES = 56 << 20
NEG_BIG = -1e30

N_MIXERS = 3
CONV_W = 3
HG_DK = 128
ATT_DH = 128
DILATED_GROUPS = ((128, 1), (512, 4), (2048, 16))
N_GROUPS = 3
ROPE_DIM = ATT_DH // 4
ROPE_THETA = 500000.0
XA_DH = 128
GLA_BLOCK = 16
PAST_LEN = 16384

NT_DIMS = (((1,), (1,)), ((), ()))
TN_DIMS = (((0,), (0,)), ((), ()))


def _cparams(*semantics):
    return pltpu.CompilerParams(dimension_semantics=semantics, vmem_limit_bytes=VMEM_LIMIT_BYTES)


def _sigmoid(x):
    return 1.0 / (1.0 + jnp.exp(-x))


def _rms_body(x_ref, g_ref, o_ref):
    x = x_ref[...]
    ms = jnp.mean(x * x, axis=-1, keepdims=True)
    o_ref[...] = (x * lax.rsqrt(ms + EPS) * g_ref[...]).astype(o_ref.dtype)


def _rms_norm(x, gain, *, tm, out_dtype, name):
    m, k = x.shape
    return pl.pallas_call(
        _rms_body,
        grid=(m // tm,),
        in_specs=[pl.BlockSpec((tm, k), lambda i: (i, 0)), pl.BlockSpec((1, k), lambda i: (0, 0))],
        out_specs=pl.BlockSpec((tm, k), lambda i: (i, 0)),
        out_shape=jax.ShapeDtypeStruct((m, k), out_dtype),
        compiler_params=_cparams("parallel"),
        name=name,
    )(x, gain.reshape(1, k))


CAST_ROWS = 256


def _cast_weight(w_ref, wb_ref, slot):
    def body(c, carry):
        rows = pl.ds(pl.multiple_of(c * CAST_ROWS, CAST_ROWS), CAST_ROWS)
        wb_ref[slot, rows, :] = w_ref[0, rows, :].astype(BF16)
        return carry
    lax.fori_loop(0, w_ref.shape[1] // CAST_ROWS, body, 0)


def _ep_plain(accs, extra, outs, col0):
    acc = accs[0]
    outs[0][:, col0:col0 + acc.shape[1]] = acc.astype(outs[0].dtype)


def _ep_swiglu(accs, extra, outs, col0):
    gate, up = accs
    outs[0][:, col0:col0 + gate.shape[1]] = (gate * _sigmoid(gate) * up).astype(BF16)


def _ep_conv_seq(accs, extra, outs, col0, scratch, *, tiles_per_seq):
    b_gate, c_gate, z = accs
    (w_ref,), (gated_ref, tail_ref), (carry_ref, ue_ref) = extra, outs, scratch
    tm, width = z.shape
    cols = slice(col0, col0 + width)
    ue = ue_ref.at[col0 // width]
    u = c_gate * z
    first = (pl.program_id(1) % tiles_per_seq) == 0
    ue[0:SUBLANES, :] = jnp.where(first, 0.0, carry_ref[:, cols])
    ue[SUBLANES:SUBLANES + tm, :] = u
    w = w_ref[:, cols]
    conv = (w[0:1] * ue[pl.ds(SUBLANES - 2, tm), :] + w[1:2] * ue[pl.ds(SUBLANES - 1, tm), :]
            + w[2:3] * u)
    gated_ref[:, cols] = (b_gate * conv).astype(BF16)
    tail = u[tm - SUBLANES:, :]
    carry_ref[:, cols] = tail
    tail_ref[:, cols] = tail


def _ep_conv_step(accs, extra, outs, col0):
    b_gate, c_gate, z = accs
    s0_ref, s1_ref, w_ref = extra
    u = c_gate * z
    w = w_ref[...]
    conv = w[0:1] * s0_ref[...] + w[1:2] * s1_ref[...] + w[2:3] * u
    outs[0][...] = (b_gate * conv).astype(BF16)
    outs[1][...] = u


def _ep_rotary(accs, extra, outs, col0):
    cos, sin_lo, sin_hi = (r[...] for r in extra)
    acc = accs[0]
    half = ROPE_DIM // 2
    for c in range(acc.shape[1] // LANES):
        xc = acc[:, c * LANES:(c + 1) * LANES]
        outs[0][:, col0 + c * LANES:col0 + (c + 1) * LANES] = (
            xc * cos + pltpu.roll(xc, LANES - half, 1) * sin_lo + pltpu.roll(xc, half, 1) * sin_hi)


def _proj_body(*refs, n_w, n_extra, n_extra_s, n_out, n_out_s, has_sample, epilogue, epilogue_s, col_chunk,
               alt_epilogue, alt_pred):
    refs = list(refs)
    take = lambda n: [refs.pop(0) for _ in range(n)]
    (x_ref,) = take(1)
    xs_ref, gs_ref = take(2) if has_sample else (None, None)
    w_refs = take(n_w)
    extra = take(n_extra)
    extra_s = take(n_extra_s)
    outs = take(n_out)
    outs_s = take(n_out_s)
    wb_ref, *scratch = refs

    @pl.when(pl.program_id(1) == 0)
    def _():
        for k, w_ref in enumerate(w_refs):
            _cast_weight(w_ref, wb_ref, k)
        if has_sample:
            xs = xs_ref[...]
            ms = jnp.mean(xs * xs, axis=-1, keepdims=True)
            xsn = (xs * lax.rsqrt(ms + EPS) * gs_ref[...]).astype(BF16)
            accs_s = [jnp.dot(xsn, wb_ref[k], preferred_element_type=F32) for k in range(n_w)]
            if alt_epilogue is None:
                epilogue_s(accs_s, extra_s, outs_s, 0)
            else:
                use_alt = alt_pred(pl.program_id(0))
                pl.when(use_alt)(lambda: alt_epilogue(accs_s, extra_s, outs_s, 0))
                pl.when(jnp.logical_not(use_alt))(lambda: epilogue_s(accs_s, extra_s, outs_s, 0))

    def tile(ep):
        x = x_ref[...]
        for c0 in range(0, wb_ref.shape[2], col_chunk):
            accs = [jnp.dot(x, wb_ref[k, :, c0:c0 + col_chunk], preferred_element_type=F32) for k in range(n_w)]
            ep(accs, extra, outs, c0, *([scratch] if scratch else []))

    if alt_epilogue is None:
        tile(epilogue)
    else:
        use_alt = alt_pred(pl.program_id(0))
        pl.when(use_alt)(lambda: tile(alt_epilogue))
        pl.when(jnp.logical_not(use_alt))(lambda: tile(epilogue))


def _proj(xn, xs, gain, w_stack, layer, *, w_col_blocks, n_col_blocks, epilogue, out_dtypes, name,
          tm=1024, tn=512, col_chunk=None, extra=(), extra_specs=(), extra_s=(), extra_s_specs=(),
          epilogue_s=None, out_s_dtypes=None, more_outs=(), scratch=(), alt_epilogue=None, alt_pred=None):
    m, k = xn.shape
    tm = min(tm, m)
    has_sample = xs is not None
    assert m % tm == 0 and w_stack.shape[1] == k and k % CAST_ROWS == 0
    out_s_dtypes = (out_dtypes if out_s_dtypes is None else out_s_dtypes) if has_sample else ()
    n_total = n_col_blocks * tn
    args = [xn]
    in_specs = [pl.BlockSpec((tm, k), lambda j, i: (i, 0))]
    if has_sample:
        rs = xs.shape[0]
        args += [xs, gain.reshape(1, k)]
        in_specs += [pl.BlockSpec((rs, k), lambda j, i: (0, 0)), pl.BlockSpec((1, k), lambda j, i: (0, 0))]
    for off in w_col_blocks:
        args.append(w_stack)
        in_specs.append(pl.BlockSpec((1, k, tn), lambda j, i, off=off: (layer, 0, j + off)))
    args += list(extra) + list(extra_s)
    in_specs += list(extra_specs) + list(extra_s_specs)
    out_shape = [jax.ShapeDtypeStruct((m, n_total), dt) for dt in out_dtypes] + [o[0] for o in more_outs]
    out_specs = [pl.BlockSpec((tm, tn), lambda j, i: (i, j)) for _ in out_dtypes] + [o[1] for o in more_outs]
    n_out = len(out_shape)
    if has_sample:
        out_shape += [jax.ShapeDtypeStruct((rs, n_total), dt) for dt in out_s_dtypes]
        out_specs += [pl.BlockSpec((rs, tn), lambda j, i: (0, j)) for _ in out_s_dtypes]
    res = pl.pallas_call(
        functools.partial(_proj_body, n_w=len(w_col_blocks), n_extra=len(extra), n_extra_s=len(extra_s),
                          n_out=n_out, n_out_s=len(out_s_dtypes), has_sample=has_sample,
                          epilogue=epilogue, epilogue_s=epilogue_s or epilogue, col_chunk=col_chunk or tn,
                          alt_epilogue=alt_epilogue, alt_pred=alt_pred),
        grid=(n_col_blocks, m // tm),
        in_specs=in_specs,
        out_specs=out_specs,
        out_shape=out_shape,
        scratch_shapes=[pltpu.VMEM((len(w_col_blocks), k, tn), BF16)] + list(scratch),
        compiler_params=_cparams("arbitrary", "arbitrary"),
        name=name,
    )(*args)
    return (res[:n_out], res[n_out:]) if has_sample else (res, None)


WEIGHT_CHUNK_ROWS = 512


def _res_proj_norm_big_body(a_ref, as_ref, w_hbm, r_ref, rs_ref, g_ref, o_ref, os_ref, xn_ref,
                            wb_ref, stage_ref, sem, *, scale, layer):
    @pl.when(pl.program_id(0) == 0)
    def _():
        chunk = stage_ref.shape[1]
        n_chunks = wb_ref.shape[0] // chunk

        def copy(c):
            return pltpu.make_async_copy(w_hbm.at[layer, pl.ds(c * chunk, chunk), :], stage_ref.at[c % 2],
                                         sem.at[c % 2])
        copy(0).start()
        for c in range(n_chunks):
            if c + 1 < n_chunks:
                copy(c + 1).start()
            copy(c).wait()
            wb_ref[pl.ds(c * chunk, chunk), :] = stage_ref[c % 2].astype(BF16)
        os_ref[...] = rs_ref[...] + scale * jnp.dot(as_ref[...], wb_ref[...], preferred_element_type=F32)

    x = r_ref[...] + scale * jnp.dot(a_ref[...], wb_ref[...], preferred_element_type=F32)
    o_ref[...] = x
    ms = jnp.mean(x * x, axis=-1, keepdims=True)
    xn_ref[...] = (x * lax.rsqrt(ms + EPS) * g_ref[...]).astype(xn_ref.dtype)


def _res_proj_norm_big(a, a_s, w_stack, layer, res, res_s, scale, next_gain, *, name, norm_dtype=BF16, tm=256):
    m, k = a.shape
    rs = a_s.shape[0]
    n = w_stack.shape[2]
    assert m % tm == 0 and k % WEIGHT_CHUNK_ROWS == 0 and res.shape == (m, n)
    return pl.pallas_call(
        functools.partial(_res_proj_norm_big_body, scale=scale, layer=layer),
        grid=(m // tm,),
        in_specs=[pl.BlockSpec((tm, k), lambda i: (i, 0)),
                  pl.BlockSpec((rs, k), lambda i: (0, 0)),
                  pl.BlockSpec(memory_space=pl.ANY),
                  pl.BlockSpec((tm, n), lambda i: (i, 0)),
                  pl.BlockSpec((rs, n), lambda i: (0, 0)),
                  pl.BlockSpec((1, n), lambda i: (0, 0))],
        out_specs=[pl.BlockSpec((tm, n), lambda i: (i, 0)),
                   pl.BlockSpec((rs, n), lambda i: (0, 0)),
                   pl.BlockSpec((tm, n), lambda i: (i, 0))],
        out_shape=[jax.ShapeDtypeStruct((m, n), F32), jax.ShapeDtypeStruct((rs, n), F32),
                   jax.ShapeDtypeStruct((m, n), norm_dtype)],
        scratch_shapes=[pltpu.VMEM((k, n), BF16), pltpu.VMEM((2, WEIGHT_CHUNK_ROWS, n), F32),
                        pltpu.SemaphoreType.DMA((2,))],
        compiler_params=_cparams("arbitrary"),
        name=name,
    )(a, a_s, w_stack, res, res_s, next_gain.reshape(1, n))


def _hgrn_seq_body(q_ref, z_ref, v_ref, g_ref, lb_ref, gn_ref, tri_ref, y_ref, st_out_ref,
                   st_ref, qs_ref, ks_ref, cum_ref, o_ref, *, hb):
    @pl.when(pl.program_id(2) == 0)
    def _():
        st_ref[...] = jnp.zeros_like(st_ref)

    tt = q_ref.shape[0]
    lb = lb_ref[...]
    z = z_ref[...]
    forget = lb + (1.0 - lb) * _sigmoid(z)
    ks_ref[...] = (1.0 - lb) * _sigmoid(-z)
    q = q_ref[...]
    qs_ref[...] = q * _sigmoid(q)
    cum_ref[...] = jnp.dot(tri_ref[...], jnp.log(forget), preferred_element_type=F32,
                           precision=lax.Precision.HIGHEST)

    t_idx = lax.broadcasted_iota(jnp.int32, (GLA_BLOCK, 1), 0)
    half = GLA_BLOCK // 2

    def block(j, carry):
        rows = pl.ds(pl.multiple_of(j * GLA_BLOCK, GLA_BLOCK), GLA_BLOCK)
        for h in range(hb):
            cols = slice(h * HG_DK, (h + 1) * HG_DK)
            cb = cum_ref[rows, cols]
            qb = qs_ref[rows, cols]
            kb = ks_ref[rows, cols]
            vb = v_ref[rows, cols]
            last = cb[GLA_BLOCK - 1:GLA_BLOCK, :]
            st = st_ref[h]
            o = lax.dot_general((qb * jnp.exp(cb)).astype(BF16), st.astype(BF16), NT_DIMS,
                                preferred_element_type=F32)
            o_lo = jnp.zeros((half, LANES), F32)
            o_hi = jnp.zeros((half, LANES), F32)
            for s in range(GLA_BLOCK):
                cs = cb[s:s + 1, :]
                qk_hi = qb[half:, :] * kb[s:s + 1, :]
                e_hi = jnp.exp(cb[half:, :] - cs)
                if s >= half:
                    e_hi = jnp.where(t_idx[half:] >= s, e_hi, 0.0)
                g_hi = e_hi * qk_hi
                o_hi = o_hi + jnp.sum(g_hi, axis=-1, keepdims=True) * vb[s:s + 1, :]
                if s < half:
                    qk_lo = qb[:half, :] * kb[s:s + 1, :]
                    g_lo = jnp.where(t_idx[:half] >= s, jnp.exp(cb[:half, :] - cs), 0.0) * qk_lo
                    o_lo = o_lo + jnp.sum(g_lo, axis=-1, keepdims=True) * vb[s:s + 1, :]
            o_ref[rows, cols] = o + jnp.concatenate([o_lo, o_hi], axis=0)
            kd = kb * jnp.exp(last - cb)
            upd = lax.dot_general(vb.astype(BF16), kd.astype(BF16), TN_DIMS, preferred_element_type=F32)
            st_ref[h] = st * jnp.exp(last) + upd
        return carry

    lax.fori_loop(0, tt // GLA_BLOCK, block, 0)

    for h in range(hb):
        cols = slice(h * HG_DK, (h + 1) * HG_DK)
        o = o_ref[:, cols]
        ms = jnp.mean(o * o, axis=-1, keepdims=True)
        g = g_ref[:, cols]
        y_ref[:, cols] = (o * lax.rsqrt(ms + EPS) * gn_ref[0, h:h + 1, :] * (g * _sigmoid(g))).astype(BF16)

    @pl.when(pl.program_id(2) == pl.num_programs(2) - 1)
    def _():
        st_out_ref[0] = st_ref[...]


def _hgrn_seq(proj, lower_bound, norm_gain, *, batch, seq_len, tt=256, hb=8):
    m, n4 = proj.shape
    d = n4 // 4
    heads = d // HG_DK
    hg = heads // hb
    nt = seq_len // tt
    wcols = hb * HG_DK
    tri = (jnp.arange(tt)[:, None] >= jnp.arange(tt)[None, :]) & (
        jnp.arange(tt)[:, None] // GLA_BLOCK == jnp.arange(tt)[None, :] // GLA_BLOCK)
    tri = tri.astype(F32)

    def col(sec):
        return pl.BlockSpec((tt, wcols), lambda b, h, t, sec=sec: (b * nt + t, sec * hg + h))

    y, st = pl.pallas_call(
        functools.partial(_hgrn_seq_body, hb=hb),
        grid=(batch, hg, nt),
        in_specs=[col(0), col(1), col(2), col(3),
                  pl.BlockSpec((1, wcols), lambda b, h, t: (0, h)),
                  pl.BlockSpec((1, hb, HG_DK), lambda b, h, t: (h, 0, 0)),
                  pl.BlockSpec((tt, tt), lambda b, h, t: (0, 0))],
        out_specs=[pl.BlockSpec((tt, wcols), lambda b, h, t: (b * nt + t, h)),
                   pl.BlockSpec((1, hb, HG_DK, HG_DK), lambda b, h, t: (b, h, 0, 0))],
        out_shape=[jax.ShapeDtypeStruct((m, d), BF16),
                   jax.ShapeDtypeStruct((batch, heads, HG_DK, HG_DK), F32)],
        scratch_shapes=[pltpu.VMEM((hb, HG_DK, HG_DK), F32), pltpu.VMEM((tt, wcols), F32),
                        pltpu.VMEM((tt, wcols), F32), pltpu.VMEM((tt, wcols), F32),
                        pltpu.VMEM((tt, wcols), F32)],
        compiler_params=_cparams("parallel", "parallel", "arbitrary"),
        name="hgrn_seq",
    )(proj, proj, proj, proj, lower_bound.reshape(1, d), norm_gain.reshape(hg, hb, HG_DK), tri)
    return y, st


def _hgrn_step_body(q_ref, z_ref, v_ref, g_ref, lb_ref, gn_ref, s_ref, y_ref, s_out_ref):
    heads = s_ref.shape[1]
    for h in range(heads):
        lb = lb_ref[h]
        z = z_ref[0, h]
        forget = lb + (1.0 - lb) * _sigmoid(z)
        k = (1.0 - lb) * _sigmoid(-z)
        q = q_ref[0, h]
        q = q * _sigmoid(q)
        s_new = forget * s_ref[0, h] + k * v_ref[0, h]
        s_out_ref[0, h] = s_new
        o = jnp.sum(q * s_new, axis=0, keepdims=True)
        ms = jnp.mean(o * o, axis=-1, keepdims=True)
        g = g_ref[0, h]
        y_ref[0, h] = o * lax.rsqrt(ms + EPS) * gn_ref[h] * (g * _sigmoid(g))


def _hgrn_step(proj, lower_bound, norm_gain, state):
    bsz, heads, dk, dv = state.shape
    d = heads * dk
    q, z, v, g = (proj[:, s * d:(s + 1) * d] for s in range(4))
    col = lambda a: a.reshape(bsz, heads, dk, 1)
    row = lambda a: a.reshape(bsz, heads, 1, dv)
    col_spec = pl.BlockSpec((1, heads, dk, 1), lambda b: (b, 0, 0, 0))
    row_spec = pl.BlockSpec((1, heads, 1, dv), lambda b: (b, 0, 0, 0))
    st_spec = pl.BlockSpec((1, heads, dk, dv), lambda b: (b, 0, 0, 0))
    y, s_new = pl.pallas_call(
        _hgrn_step_body,
        grid=(bsz,),
        in_specs=[col_spec, col_spec, row_spec, row_spec,
                  pl.BlockSpec((heads, dk, 1), lambda b: (0, 0, 0)),
                  pl.BlockSpec((heads, 1, dv), lambda b: (0, 0, 0)),
                  st_spec],
        out_specs=[row_spec, st_spec],
        out_shape=[jax.ShapeDtypeStruct((bsz, heads, 1, dv), F32),
                   jax.ShapeDtypeStruct(state.shape, F32)],
        compiler_params=_cparams("parallel"),
        name="hgrn_step",
    )(col(q), col(z), row(v), row(g), lower_bound.reshape(heads, dk, 1),
      norm_gain.reshape(heads, 1, dv), state)
    return y.reshape(bsz, d), s_new


def _band_block(q, k, v, scale, has_prev):
    n, nkeys = q.shape[0], k.shape[0]
    qi = lax.broadcasted_iota(jnp.int32, (n, nkeys), 0)
    ki = lax.broadcasted_iota(jnp.int32, (n, nkeys), 1)
    s = lax.dot_general(q.astype(BF16), k.astype(BF16), NT_DIMS, preferred_element_type=F32) * scale
    visible = ((ki >= qi) & (ki <= qi + n)) if has_prev else (ki <= qi)
    s = jnp.where(visible, s, NEG_BIG)
    m = jnp.max(s, axis=-1, keepdims=True)
    p = jnp.exp(s - m)
    l = jnp.sum(p, axis=-1, keepdims=True)
    acc = jnp.dot(p.astype(BF16), v.astype(BF16), preferred_element_type=F32)
    return acc / l, m + jnp.log(l)


def _dil_attn_seq_body(*refs, seq_len, hb):
    n_in = 3 * N_GROUPS
    y_ref, og_ref, lse_ref = refs[hb * n_in:]
    scale = ATT_DH ** -0.5
    for g, (win, dil) in enumerate(DILATED_GROUPS):
        nk = win // dil
        for r in range(dil):
            for n in range(seq_len // dil // nk):
                def rows(first_blk, n_blk):
                    start, size = r + dil * nk * first_blk, nk * n_blk
                    return pl.ds(start, size, stride=dil) if dil > 1 else pl.ds(start, size)
                cur = rows(n, 1)
                keys = rows(n - 1, 2) if n else cur
                for h in range(hb):
                    q_ref, k_ref, v_ref = refs[h * n_in + 3 * g:h * n_in + 3 * g + 3]
                    o, lse = _band_block(q_ref[0, cur, :], k_ref[0, keys, :], v_ref[0, keys, :], scale, n > 0)
                    og_ref[h * N_GROUPS + g, cur, :] = o
                    lse_ref[h * N_GROUPS + g, cur, :] = jnp.broadcast_to(lse, (nk, LANES))
    for h in range(hb):
        lse = [lse_ref[h * N_GROUPS + g] for g in range(N_GROUPS)]
        og = [og_ref[h * N_GROUPS + g] for g in range(N_GROUPS)]
        m = jnp.maximum(jnp.maximum(lse[0], lse[1]), lse[2])
        w = [jnp.exp(x - m) for x in lse]
        y = (w[0] * og[0] + w[1] * og[1] + w[2] * og[2]) / (w[0] + w[1] + w[2])
        y_ref[0, :, h * ATT_DH:(h + 1) * ATT_DH] = y.astype(BF16)


def _dil_attn_seq(qkv, *, batch, seq_len, heads, hb=1):
    in_specs = []
    for h in range(hb):
        for s in range(3 * N_GROUPS):
            in_specs.append(pl.BlockSpec((1, seq_len, ATT_DH),
                                         lambda b, hg, s=s, h=h: (b, 0, s * heads + hg * hb + h)))
    return pl.pallas_call(
        functools.partial(_dil_attn_seq_body, seq_len=seq_len, hb=hb),
        grid=(batch, heads // hb),
        in_specs=in_specs,
        out_specs=pl.BlockSpec((1, seq_len, hb * ATT_DH), lambda b, hg: (b, 0, hg)),
        out_shape=jax.ShapeDtypeStruct((batch, seq_len, heads * ATT_DH), BF16),
        scratch_shapes=[pltpu.VMEM((hb * N_GROUPS, seq_len, ATT_DH), F32),
                        pltpu.VMEM((hb * N_GROUPS, seq_len, LANES), F32)],
        compiler_params=_cparams("parallel", "parallel"),
        name="dil_attn_seq",
    )(*([qkv] * (hb * 3 * N_GROUPS)))


def _bf16_round(x):
    return x.astype(BF16).astype(F32)


def _step_attention(q, k_new, v_new, k_c, v_c, scale):
    qb = _bf16_round(q)
    s_c = jnp.sum(_bf16_round(k_c) * qb[None], axis=-1, keepdims=True) * scale
    m = jnp.max(s_c, axis=0)
    if k_new is not None:
        s_n = jnp.sum(qb * _bf16_round(k_new), axis=-1, keepdims=True) * scale
        m = jnp.maximum(m, s_n)
    p_c = jnp.exp(s_c - m[None])
    l = jnp.sum(p_c, axis=0)
    acc = jnp.sum(_bf16_round(p_c) * _bf16_round(v_c), axis=0)
    if k_new is not None:
        p_n = jnp.exp(s_n - m)
        l = l + p_n
        acc = acc + _bf16_round(p_n) * _bf16_round(v_new)
    return acc / l, m + jnp.log(l)


def _dil_attn_step_body(*refs):
    new = refs[:3 * N_GROUPS]
    caches = refs[3 * N_GROUPS:5 * N_GROUPS]
    y_ref = refs[5 * N_GROUPS]
    scale = ATT_DH ** -0.5
    outs, lses = [], []
    for g in range(N_GROUPS):
        o, lse = _step_attention(new[3 * g][0, 0], new[3 * g + 1][0, 0], new[3 * g + 2][0, 0],
                                 caches[2 * g][...], caches[2 * g + 1][...], scale)
        outs.append(o)
        lses.append(lse)
    m = jnp.maximum(jnp.maximum(lses[0], lses[1]), lses[2])
    w = [jnp.exp(x - m) for x in lses]
    y_ref[0] = (w[0] * outs[0] + w[1] * outs[1] + w[2] * outs[2]) / (w[0] + w[1] + w[2])


def _dil_attn_step(qkv, caches, layer, *, heads):
    bsz = qkv.shape[0]
    new = qkv.reshape(bsz, 3 * N_GROUPS, heads, ATT_DH)
    args, in_specs = [], []
    for s in range(3 * N_GROUPS):
        args.append(new)
        in_specs.append(pl.BlockSpec((1, 1, heads, ATT_DH), lambda b, s=s: (b, s, 0, 0)))
    for g, (win, dil) in enumerate(DILATED_GROUPS):
        nk = win // dil
        for buf in caches[g]:
            assert buf.shape[1:] == (bsz, win, heads, ATT_DH)
            args.append(buf.reshape(buf.shape[0], bsz, nk, dil, heads, ATT_DH))
            in_specs.append(pl.BlockSpec((None, None, nk, None, heads, ATT_DH),
                                         lambda b: (layer, b, 0, 0, 0, 0)))
    y = pl.pallas_call(
        _dil_attn_step_body,
        grid=(bsz,),
        in_specs=in_specs,
        out_specs=pl.BlockSpec((1, heads, ATT_DH), lambda b: (b, 0, 0)),
        out_shape=jax.ShapeDtypeStruct((bsz, heads, ATT_DH), F32),
        compiler_params=_cparams("parallel"),
        name="dil_attn_step",
    )(*args)
    return y.reshape(bsz, heads * ATT_DH)


def _softmax_pv(s, v):
    m = jnp.max(s, axis=-1, keepdims=True)
    p = jnp.exp(s - m)
    l = jnp.sum(p, axis=-1, keepdims=True)
    return jnp.dot(p.astype(BF16), v.astype(BF16), preferred_element_type=F32) / l


def _memory_kv_body(x_ref, g_ref, w_ref, o_ref, wb_ref, xn_ref):
    _cast_weight(w_ref, wb_ref, 0)

    def body(c, carry):
        rows = pl.ds(pl.multiple_of(c * CAST_ROWS, CAST_ROWS), CAST_ROWS)
        x = x_ref[rows, :]
        ms = jnp.mean(x * x, axis=-1, keepdims=True)
        xn_ref[rows, :] = (x * lax.rsqrt(ms + EPS) * g_ref[0]).astype(BF16)
        return carry
    lax.fori_loop(0, x_ref.shape[0] // CAST_ROWS, body, 0)
    o_ref[0] = jnp.dot(xn_ref[...], wb_ref[0], preferred_element_type=F32)


def _memory_kv(mem, gains, w_kv, *, tn=512):
    rows, d = mem.shape
    layers, _, n = w_kv.shape
    assert rows % CAST_ROWS == 0 and d % CAST_ROWS == 0 and n % tn == 0
    return pl.pallas_call(
        _memory_kv_body,
        grid=(layers, n // tn),
        in_specs=[pl.BlockSpec((rows, d), lambda l, j: (0, 0)),
                  pl.BlockSpec((1, 1, d), lambda l, j: (l, 0, 0)),
                  pl.BlockSpec((1, d, tn), lambda l, j: (l, 0, j))],
        out_specs=pl.BlockSpec((1, rows, tn), lambda l, j: (l, 0, j)),
        out_shape=jax.ShapeDtypeStruct((layers, rows, n), F32),
        scratch_shapes=[pltpu.VMEM((1, d, tn), BF16), pltpu.VMEM((rows, d), BF16)],
        compiler_params=_cparams("parallel", "parallel"),
        name="memory_kv",
    )(mem, gains.reshape(layers, 1, d), w_kv)


def _stream_weight(w_hbm, wb_ref, stage_ref, sem):
    chunk = stage_ref.shape[1]
    k, n = wb_ref.shape
    n_chunks = k // chunk

    def copy(c):
        return pltpu.make_async_copy(w_hbm.at[pl.ds(c * chunk, chunk), :], stage_ref.at[c % 2, :, pl.ds(0, n)],
                                     sem.at[c % 2])
    copy(0).start()
    for c in range(n_chunks):
        if c + 1 < n_chunks:
            copy(c + 1).start()
        copy(c).wait()
        wb_ref[pl.ds(c * chunk, chunk), :] = stage_ref[c % 2, :, 0:n].astype(BF16)


def _mix_xattn_block_body(y_ref, ys_ref, r_ref, rs_ref, wm_hbm, gq_ref, wq_hbm, wo_hbm, kv_ref, ks_ref, vs_ref,
                          gn_ref, o_ref, os_ref, xnn_ref, wmb_ref, wqb_ref, wob_ref, att_ref, stage_ref, sem,
                          *, heads, nbat_s, mix_layer, layer):
    scale = XA_DH ** -0.5
    hd = heads * XA_DH

    def rms(x, g_ref):
        ms = jnp.mean(x * x, axis=-1, keepdims=True)
        return (x * lax.rsqrt(ms + EPS) * g_ref[...]).astype(BF16)

    @pl.when(pl.program_id(0) == 0)
    def _():
        _stream_weight(wm_hbm.at[mix_layer], wmb_ref, stage_ref, sem)
        _stream_weight(wq_hbm.at[layer], wqb_ref, stage_ref, sem)
        _stream_weight(wo_hbm.at[layer], wob_ref, stage_ref, sem)
        xs = rs_ref[...] + jnp.dot(ys_ref[...], wmb_ref[...], preferred_element_type=F32)
        q_s = jnp.dot(rms(xs, gq_ref), wqb_ref[...], preferred_element_type=F32)
        att_ref[...] = jnp.zeros_like(att_ref)
        for b in range(nbat_s):
            for h in range(heads):
                cols = slice(h * XA_DH, (h + 1) * XA_DH)
                qb = _bf16_round(q_s[b:b + 1, cols])
                s = jnp.sum(_bf16_round(ks_ref[b, :, h, :]) * qb, axis=-1, keepdims=True) * scale
                p = jnp.exp(s - jnp.max(s, axis=0, keepdims=True))
                acc = jnp.sum(_bf16_round(p) * _bf16_round(vs_ref[b, :, h, :]), axis=0, keepdims=True)
                att_ref[b:b + 1, cols] = acc / jnp.sum(p, axis=0, keepdims=True)
        os_ref[...] = xs + jnp.dot(att_ref[0:xs.shape[0], :].astype(BF16), wob_ref[...], preferred_element_type=F32)

    x = r_ref[...] + jnp.dot(y_ref[...], wmb_ref[...], preferred_element_type=F32)
    q = jnp.dot(rms(x, gq_ref), wqb_ref[...], preferred_element_type=F32)
    for h in range(heads):
        cols = slice(h * XA_DH, (h + 1) * XA_DH)
        k = kv_ref[0][:, cols]
        v = kv_ref[0][:, hd + h * XA_DH:hd + (h + 1) * XA_DH]
        s = lax.dot_general(q[:, cols].astype(BF16), k.astype(BF16), NT_DIMS, preferred_element_type=F32) * scale
        att_ref[:, cols] = _softmax_pv(s, v)
    x = x + jnp.dot(att_ref[...].astype(BF16), wob_ref[...], preferred_element_type=F32)
    o_ref[...] = x
    xnn_ref[...] = rms(x, gn_ref)


def _mix_xattn_block(y, y_s, xp, xs, w_mix, mix_layer, gain, w_q, w_o, layer, mem_kv, mem_k_s, mem_v_s, next_gain,
                     *, nbat, t, nbat_s, tm=256):
    m, d = xp.shape
    rs = xs.shape[0]
    hd = w_q.shape[2]
    heads = hd // XA_DH
    mem_len = mem_kv.shape[2]
    tiles = t // tm
    assert t % tm == 0 and tm >= rs and mem_k_s.shape[1:] == (nbat_s, mem_len, heads, XA_DH)
    assert d % WEIGHT_CHUNK_ROWS == 0 and hd % WEIGHT_CHUNK_ROWS == 0 and hd <= d
    once = dict(pipeline_mode=pl.Buffered(1))
    mem_s_spec = pl.BlockSpec((None, nbat_s, mem_len, heads, XA_DH), lambda i: (layer, 0, 0, 0, 0), **once)
    row = pl.BlockSpec((tm, d), lambda i: (i, 0))
    srow = pl.BlockSpec((rs, d), lambda i: (0, 0))
    vec = pl.BlockSpec((1, d), lambda i: (0, 0))
    hbm = pl.BlockSpec(memory_space=pl.ANY)
    return pl.pallas_call(
        functools.partial(_mix_xattn_block_body, heads=heads, nbat_s=nbat_s, mix_layer=mix_layer, layer=layer),
        grid=(m // tm,),
        in_specs=[row, srow, row, srow, hbm, vec, hbm, hbm,
                  pl.BlockSpec((None, 1, mem_len, 2 * hd), lambda i: (layer, i // tiles, 0, 0)),
                  mem_s_spec, mem_s_spec, vec],
        out_specs=[row, srow, row],
        out_shape=[jax.ShapeDtypeStruct((m, d), F32), jax.ShapeDtypeStruct((rs, d), F32),
                   jax.ShapeDtypeStruct((m, d), BF16)],
        scratch_shapes=[pltpu.VMEM((d, d), BF16), pltpu.VMEM((d, hd), BF16), pltpu.VMEM((hd, d), BF16),
                        pltpu.VMEM((tm, hd), F32), pltpu.VMEM((2, WEIGHT_CHUNK_ROWS, d), F32),
                        pltpu.SemaphoreType.DMA((2,))],
        compiler_params=_cparams("arbitrary"),
        name="mix_xattn_block",
    )(y, y_s, xp, xs, w_mix, gain.reshape(1, d), w_q, w_o, mem_kv, mem_k_s, mem_v_s, next_gain.reshape(1, d))


TN_COLS = 512


def _rope_tables(pos):
    rows = pos.shape[0]
    half = ROPE_DIM // 2
    inv_freq = ROPE_THETA ** (-jnp.arange(half, dtype=F32) * 2.0 / ROPE_DIM)
    ang = pos.astype(F32)[:, None] * inv_freq[None, :]
    cos, sin = jnp.cos(ang), jnp.sin(ang)
    rest = jnp.zeros((rows, LANES - ROPE_DIM), F32)
    zh = jnp.zeros((rows, half), F32)
    return (jnp.concatenate([cos, cos, rest + 1.0], axis=1), jnp.concatenate([-sin, zh, rest], axis=1),
            jnp.concatenate([zh, sin, rest], axis=1))


def _pad_rows(a, rows):
    return jnp.pad(a, ((0, rows - a.shape[0]), (0, 0)))


ROWS_S = 16
TM_ROWS = 512
TN_WIDE = 1024
TM_GATE_UP = 2048


def _ffn(xp, xs, xn, gain, w_gu, w_down, layer, tag, next_gain, norm_dtype=BF16):
    nf = w_down.shape[1] // TN_COLS
    (hid,), (hid_s,) = _proj(xn, xs, gain, w_gu, layer, w_col_blocks=(0, nf), n_col_blocks=nf,
                             epilogue=_ep_swiglu, out_dtypes=(BF16,), name=tag + "_gate_up", tm=TM_GATE_UP)
    return _res_proj_norm_big(hid, hid_s, w_down, layer, xp, xs, 0.5, next_gain, name=tag + "_down",
                              norm_dtype=norm_dtype)


def _conv_mixer(xp, xs, xn, gain, w_in, w_conv, layer, state, *, nbat, t, nbat_s):
    d = xp.shape[1]
    nb = d // TN_COLS
    tm = min(1024, t)
    tiles = t // tm
    w_spec = pl.BlockSpec((CONV_W, TN_COLS), lambda j, i: (0, j))
    st_spec = pl.BlockSpec((ROWS_S, TN_COLS), lambda j, i: (0, j))
    tail_out = (jax.ShapeDtypeStruct((nbat * SUBLANES, d), F32),
                pl.BlockSpec((SUBLANES, TN_COLS), lambda j, i: (i // tiles, j)))
    (gated, tail), (gated_s, u_s) = _proj(
        xn, xs, gain, w_in, layer, w_col_blocks=(0, nb, 2 * nb), n_col_blocks=nb, name="conv_in", tm=tm,
        epilogue=functools.partial(_ep_conv_seq, tiles_per_seq=tiles), out_dtypes=(BF16,), more_outs=(tail_out,),
        epilogue_s=_ep_conv_step, out_s_dtypes=(BF16, F32),
        extra=(w_conv,), extra_specs=(w_spec,),
        extra_s=(_pad_rows(state[:, 0], ROWS_S), _pad_rows(state[:, 1], ROWS_S), w_conv),
        extra_s_specs=(st_spec, st_spec, w_spec),
        col_chunk=MXU_COLS,
        scratch=(pltpu.VMEM((SUBLANES, TN_COLS), F32),
                 pltpu.VMEM((TN_COLS // MXU_COLS, tm + SUBLANES, MXU_COLS), F32)))
    state_p = tail.reshape(nbat, SUBLANES, d)[:, SUBLANES - (CONV_W - 1):]
    state_s = jnp.stack([state[:, 1], u_s[:nbat_s]], axis=1)
    return gated, gated_s, state_p, state_s


def _hgrn_mixer(xp, xs, xn, gain, w_in, lower_bound, norm_gain, layer, state, *, nbat, t, nbat_s):
    (proj,), (proj_s,) = _proj(xn, xs, gain, w_in, layer, w_col_blocks=(0,), n_col_blocks=w_in.shape[2] // TN_WIDE,
                               epilogue=_ep_plain, out_dtypes=(F32,), name="hgrn_in", tn=TN_WIDE)
    y, st = _hgrn_seq(proj, lower_bound, norm_gain, batch=nbat, seq_len=t)
    y_s, state_s = _hgrn_step(proj_s[:nbat_s], lower_bound, norm_gain, state)
    return y, _pad_rows(y_s, ROWS_S).astype(BF16), jnp.swapaxes(st, -1, -2), state_s


def _attn_mixer(xp, xs, xn, gain, w_qkv, layer, caches, *, nbat, t, nbat_s):
    d = xp.shape[1]
    heads = d // ATT_DH
    n_sec = 3 * N_GROUPS
    sec_blocks = d // TN_WIDE
    tm = min(1024, t)
    tiles = t // tm
    tab_spec = pl.BlockSpec((tm, LANES), lambda j, i: (i % tiles, 0))
    tab_s_spec = pl.BlockSpec((ROWS_S, LANES), lambda j, i: (0, 0))
    (qkv,), (qkv_s,) = _proj(
        xn, xs, gain, w_qkv, layer, w_col_blocks=(0,), n_col_blocks=n_sec * sec_blocks,
        epilogue=_ep_rotary, out_dtypes=(F32,), name="attn_qkv", tm=tm, tn=TN_WIDE,
        alt_epilogue=_ep_plain, alt_pred=lambda j: (j // sec_blocks) % 3 == 2,
        extra=_rope_tables(jnp.arange(t)), extra_specs=(tab_spec,) * 3,
        extra_s=_rope_tables(jnp.full((ROWS_S,), PAST_LEN, jnp.int32)), extra_s_specs=(tab_s_spec,) * 3)
    y = _dil_attn_seq(qkv.reshape(nbat, t, n_sec * d), batch=nbat, seq_len=t, heads=heads).reshape(nbat * t, d)
    y_s = _dil_attn_step(qkv_s[:nbat_s], caches, layer, heads=heads)
    q5 = qkv.reshape(nbat, t, N_GROUPS, 3, heads, ATT_DH)
    q5_s = qkv_s[:nbat_s].reshape(nbat_s, 1, N_GROUPS, 3, heads, ATT_DH)
    rows_p, rows_s = [], []
    for g, (win, _) in enumerate(DILATED_GROUPS):
        keep = min(win, t)
        rows_p += [q5[:, t - keep:, g, 1], q5[:, t - keep:, g, 2]]
        rows_s += [q5_s[:, :, g, 1], q5_s[:, :, g, 2]]
    return y, _pad_rows(y_s, ROWS_S).astype(BF16), rows_p, rows_s


def kernel(x_prompt, x_sample, state_conv, state_hgrn,
           cache_win_k0, cache_win_v0, cache_win_k1, cache_win_v1, cache_win_k2, cache_win_v2,
           cache_mem_k, cache_mem_v, mem_prompt,
           norm_ffn1, ffn1_w_gu, ffn1_w_down, norm_mix,
           conv_w_in, conv_w, conv_w_out,
           hgrn_w_in, hgrn_lb_logits, hgrn_norm, hgrn_w_out,
           attn_w_qkv, attn_w_out,
           norm_mem, xattn_w_kv, norm_xattn, xattn_w_q, xattn_w_o,
           norm_ffn2, ffn2_w_gu, ffn2_w_down, norm_final):
    batch, seq, d = x_prompt.shape
    dec_batch, dec_seq, _ = x_sample.shape
    assert dec_seq == 1 and dec_batch <= ROWS_S
    depth = norm_ffn1.shape[0]
    mem_len = mem_prompt.shape[1]
    xa_hd = xattn_w_q.shape[2]
    xa_heads = xa_hd // XA_DH
    win_cache = [(cache_win_k0, cache_win_v0), (cache_win_k1, cache_win_v1), (cache_win_k2, cache_win_v2)]
    sizes = dict(nbat=batch, t=seq, nbat_s=dec_batch)

    lb_p = jax.nn.softmax(hgrn_lb_logits.astype(F32), axis=0)
    lower_bounds = jnp.cumsum(lb_p, axis=0) - lb_p[0]

    xp = x_prompt.reshape(batch * seq, d)
    xs = _pad_rows(x_sample.reshape(dec_batch, d), ROWS_S)
    mem_kv = _memory_kv(mem_prompt.reshape(batch * mem_len, d), norm_mem, xattn_w_kv)
    mem_kv = mem_kv.reshape(depth, batch, mem_len, 2 * xa_hd)
    p_mem_k = mem_kv[..., :xa_hd].reshape(depth, batch, mem_len, xa_heads, XA_DH)
    p_mem_v = mem_kv[..., xa_hd:].reshape(depth, batch, mem_len, xa_heads, XA_DH)
    p_conv, p_hgrn, p_win, s_conv, s_hgrn, s_win = ([] for _ in range(6))
    xn = _rms_norm(xp, norm_ffn1[0], tm=TM_ROWS, out_dtype=BF16, name="first_norm")
    for i in range(depth):
        j, kind = divmod(i, N_MIXERS)
        xp, xs, xn = _ffn(xp, xs, xn, norm_ffn1[i], ffn1_w_gu, ffn1_w_down, i, "ffn1", norm_mix[i])
        if kind == 0:
            y, y_s, st_p, st_s = _conv_mixer(xp, xs, xn, norm_mix[i], conv_w_in, conv_w[j], j, state_conv[j], **sizes)
            w_mix = conv_w_out
            p_conv.append(st_p)
            s_conv.append(st_s)
        elif kind == 1:
            y, y_s, st_p, st_s = _hgrn_mixer(xp, xs, xn, norm_mix[i], hgrn_w_in, lower_bounds[i], hgrn_norm[j],
                                             j, state_hgrn[j], **sizes)
            w_mix = hgrn_w_out
            p_hgrn.append(st_p)
            s_hgrn.append(st_s)
        else:
            y, y_s, rows_p, rows_s = _attn_mixer(xp, xs, xn, norm_mix[i], attn_w_qkv, j, win_cache, **sizes)
            w_mix = attn_w_out
            p_win.append(rows_p)
            s_win.append(rows_s)
        xp, xs, xn = _mix_xattn_block(y, y_s, xp, xs, w_mix, j, norm_xattn[i], xattn_w_q, xattn_w_o, i, mem_kv,
                                      cache_mem_k, cache_mem_v, norm_ffn2[i], **sizes)
        last = i == depth - 1
        xp, xs, xn = _ffn(xp, xs, xn, norm_ffn2[i], ffn2_w_gu, ffn2_w_down, i, "ffn2",
                          norm_final if last else norm_ffn1[i + 1], F32 if last else BF16)
    y_prompt = xn.reshape(batch, seq, d)
    y_sample = _rms_norm(xs, norm_final, tm=ROWS_S, out_dtype=F32, name="final_norm_s")[:dec_batch]
    y_sample = y_sample.reshape(dec_batch, 1, d)

    stack_win = lambda rows: [jnp.stack([r[g] for r in rows]) for g in range(2 * N_GROUPS)]
    return (y_prompt, y_sample,
            jnp.stack(p_conv), jnp.stack(p_hgrn), *stack_win(p_win), p_mem_k, p_mem_v,
            jnp.stack(s_conv), jnp.stack(s_hgrn), *stack_win(s_win))
```

```python
import functools

import jax
import jax.numpy as jnp
from jax import lax
from jax.experimental import pallas as pl
from jax.experimental.pallas import tpu as pltpu

F32 = jnp.float32
BF16 = jnp.bfloat16
EPS = 1e-6
LANES = 128
SUBLANES = 8
MXU_COLS = 256
VMEM_LIMIT_BYTES = 56 << 20
NEG_BIG = -1e30

N_MIXERS = 3
CONV_W = 3
HG_DK = 128
ATT_DH = 128
DILATED_GROUPS = ((128, 1), (512, 4), (2048, 16))
N_GROUPS = 3
ROPE_DIM = ATT_DH // 4
ROPE_THETA = 500000.0
XA_DH = 128
GLA_BLOCK = 16
PAST_LEN = 16384

NT_DIMS = (((1,), (1,)), ((), ()))
TN_DIMS = (((0,), (0,)), ((), ()))


def _cparams(*semantics):
    return pltpu.CompilerParams(dimension_semantics=semantics, vmem_limit_bytes=VMEM_LIMIT_BYTES)


def _sigmoid(x):
    return 1.0 / (1.0 + jnp.exp(-x))


def _rms_body(x_ref, g_ref, o_ref):
    x = x_ref[...]
    ms = jnp.mean(x * x, axis=-1, keepdims=True)
    o_ref[...] = (x * lax.rsqrt(ms + EPS) * g_ref[...]).astype(o_ref.dtype)


def _rms_norm(x, gain, *, tm, out_dtype, name):
    m, k = x.shape
    return pl.pallas_call(
        _rms_body,
        grid=(m // tm,),
        in_specs=[pl.BlockSpec((tm, k), lambda i: (i, 0)), pl.BlockSpec((1, k), lambda i: (0, 0))],
        out_specs=pl.BlockSpec((tm, k), lambda i: (i, 0)),
        out_shape=jax.ShapeDtypeStruct((m, k), out_dtype),
        compiler_params=_cparams("parallel"),
        name=name,
    )(x, gain.reshape(1, k))


CAST_ROWS = 256


def _cast_weight(w_ref, wb_ref, slot):
    def body(c, carry):
        rows = pl.ds(pl.multiple_of(c * CAST_ROWS, CAST_ROWS), CAST_ROWS)
        wb_ref[slot, rows, :] = w_ref[0, rows, :].astype(BF16)
        return carry
    lax.fori_loop(0, w_ref.shape[1] // CAST_ROWS, body, 0)


def _ep_plain(accs, extra, outs, col0):
    acc = accs[0]
    outs[0][:, col0:col0 + acc.shape[1]] = acc.astype(outs[0].dtype)


def _ep_swiglu(accs, extra, outs, col0):
    gate, up = accs
    outs[0][:, col0:col0 + gate.shape[1]] = (gate * _sigmoid(gate) * up).astype(BF16)


def _ep_conv_seq(accs, extra, outs, col0, scratch, *, tiles_per_seq):
    b_gate, c_gate, z = accs
    (w_ref,), (gated_ref, tail_ref), (carry_ref, ue_ref) = extra, outs, scratch
    tm, width = z.shape
    cols = slice(col0, col0 + width)
    ue = ue_ref.at[col0 // width]
    u = c_gate * z
    first = (pl.program_id(1) % tiles_per_seq) == 0
    ue[0:SUBLANES, :] = jnp.where(first, 0.0, carry_ref[:, cols])
    ue[SUBLANES:SUBLANES + tm, :] = u
    w = w_ref[:, cols]
    conv = (w[0:1] * ue[pl.ds(SUBLANES - 2, tm), :] + w[1:2] * ue[pl.ds(SUBLANES - 1, tm), :]
            + w[2:3] * u)
    gated_ref[:, cols] = (b_gate * conv).astype(BF16)
    tail = u[tm - SUBLANES:, :]
    carry_ref[:, cols] = tail
    tail_ref[:, cols] = tail


def _ep_conv_step(accs, extra, outs, col0):
    b_gate, c_gate, z = accs
    s0_ref, s1_ref, w_ref = extra
    u = c_gate * z
    w = w_ref[...]
    conv = w[0:1] * s0_ref[...] + w[1:2] * s1_ref[...] + w[2:3] * u
    outs[0][...] = (b_gate * conv).astype(BF16)
    outs[1][...] = u


def _ep_rotary(accs, extra, outs, col0):
    cos, sin_lo, sin_hi = (r[...] for r in extra)
    acc = accs[0]
    half = ROPE_DIM // 2
    for c in range(acc.shape[1] // LANES):
        xc = acc[:, c * LANES:(c + 1) * LANES]
        outs[0][:, col0 + c * LANES:col0 + (c + 1) * LANES] = (
            xc * cos + pltpu.roll(xc, LANES - half, 1) * sin_lo + pltpu.roll(xc, half, 1) * sin_hi)


def _proj_body(*refs, n_w, n_extra, n_extra_s, n_out, n_out_s, has_sample, epilogue, epilogue_s, col_chunk,
               alt_epilogue, alt_pred):
    refs = list(refs)
    take = lambda n: [refs.pop(0) for _ in range(n)]
    (x_ref,) = take(1)
    xs_ref, gs_ref = take(2) if has_sample else (None, None)
    w_refs = take(n_w)
    extra = take(n_extra)
    extra_s = take(n_extra_s)
    outs = take(n_out)
    outs_s = take(n_out_s)
    wb_ref, *scratch = refs

    @pl.when(pl.program_id(1) == 0)
    def _():
        for k, w_ref in enumerate(w_refs):
            _cast_weight(w_ref, wb_ref, k)
        if has_sample:
            xs = xs_ref[...]
            ms = jnp.mean(xs * xs, axis=-1, keepdims=True)
            xsn = (xs * lax.rsqrt(ms + EPS) * gs_ref[...]).astype(BF16)
            accs_s = [jnp.dot(xsn, wb_ref[k], preferred_element_type=F32) for k in range(n_w)]
            if alt_epilogue is None:
                epilogue_s(accs_s, extra_s, outs_s, 0)
            else:
                use_alt = alt_pred(pl.program_id(0))
                pl.when(use_alt)(lambda: alt_epilogue(accs_s, extra_s, outs_s, 0))
                pl.when(jnp.logical_not(use_alt))(lambda: epilogue_s(accs_s, extra_s, outs_s, 0))

    def tile(ep):
        x = x_ref[...]
        for c0 in range(0, wb_ref.shape[2], col_chunk):
            accs = [jnp.dot(x, wb_ref[k, :, c0:c0 + col_chunk], preferred_element_type=F32) for k in range(n_w)]
            ep(accs, extra, outs, c0, *([scratch] if scratch else []))

    if alt_epilogue is None:
        tile(epilogue)
    else:
        use_alt = alt_pred(pl.program_id(0))
        pl.when(use_alt)(lambda: tile(alt_epilogue))
        pl.when(jnp.logical_not(use_alt))(lambda: tile(epilogue))


def _proj(xn, xs, gain, w_stack, layer, *, w_col_blocks, n_col_blocks, epilogue, out_dtypes, name,
          tm=1024, tn=512, col_chunk=None, extra=(), extra_specs=(), extra_s=(), extra_s_specs=(),
          epilogue_s=None, out_s_dtypes=None, more_outs=(), scratch=(), alt_epilogue=None, alt_pred=None):
    m, k = xn.shape
    tm = min(tm, m)
    has_sample = xs is not None
    assert m % tm == 0 and w_stack.shape[1] == k and k % CAST_ROWS == 0
    out_s_dtypes = (out_dtypes if out_s_dtypes is None else out_s_dtypes) if has_sample else ()
    n_total = n_col_blocks * tn
    args = [xn]
    in_specs = [pl.BlockSpec((tm, k), lambda j, i: (i, 0))]
    if has_sample:
        rs = xs.shape[0]
        args += [xs, gain.reshape(1, k)]
        in_specs += [pl.BlockSpec((rs, k), lambda j, i: (0, 0)), pl.BlockSpec((1, k), lambda j, i: (0, 0))]
    for off in w_col_blocks:
        args.append(w_stack)
        in_specs.append(pl.BlockSpec((1, k, tn), lambda j, i, off=off: (layer, 0, j + off)))
    args += list(extra) + list(extra_s)
    in_specs += list(extra_specs) + list(extra_s_specs)
    out_shape = [jax.ShapeDtypeStruct((m, n_total), dt) for dt in out_dtypes] + [o[0] for o in more_outs]
    out_specs = [pl.BlockSpec((tm, tn), lambda j, i: (i, j)) for _ in out_dtypes] + [o[1] for o in more_outs]
    n_out = len(out_shape)
    if has_sample:
        out_shape += [jax.ShapeDtypeStruct((rs, n_total), dt) for dt in out_s_dtypes]
        out_specs += [pl.BlockSpec((rs, tn), lambda j, i: (0, j)) for _ in out_s_dtypes]
    res = pl.pallas_call(
        functools.partial(_proj_body, n_w=len(w_col_blocks), n_extra=len(extra), n_extra_s=len(extra_s),
                          n_out=n_out, n_out_s=len(out_s_dtypes), has_sample=has_sample,
                          epilogue=epilogue, epilogue_s=epilogue_s or epilogue, col_chunk=col_chunk or tn,
                          alt_epilogue=alt_epilogue, alt_pred=alt_pred),
        grid=(n_col_blocks, m // tm),
        in_specs=in_specs,
        out_specs=out_specs,
        out_shape=out_shape,
        scratch_shapes=[pltpu.VMEM((len(w_col_blocks), k, tn), BF16)] + list(scratch),
        compiler_params=_cparams("arbitrary", "arbitrary"),
        name=name,
    )(*args)
    return (res[:n_out], res[n_out:]) if has_sample else (res, None)


WEIGHT_CHUNK_ROWS = 512


def _res_proj_norm_big_body(a_ref, as_ref, w_hbm, r_ref, rs_ref, g_ref, o_ref, os_ref, xn_ref,
                            wb_ref, stage_ref, sem, *, scale, layer):
    @pl.when(pl.program_id(0) == 0)
    def _():
        chunk = stage_ref.shape[1]
        n_chunks = wb_ref.shape[0] // chunk

        def copy(c):
            return pltpu.make_async_copy(w_hbm.at[layer, pl.ds(c * chunk, chunk), :], stage_ref.at[c % 2],
                                         sem.at[c % 2])
        copy(0).start()
        for c in range(n_chunks):
            if c + 1 < n_chunks:
                copy(c + 1).start()
            copy(c).wait()
            wb_ref[pl.ds(c * chunk, chunk), :] = stage_ref[c % 2].astype(BF16)
        os_ref[...] = rs_ref[...] + scale * jnp.dot(as_ref[...], wb_ref[...], preferred_element_type=F32)

    x = r_ref[...] + scale * jnp.dot(a_ref[...], wb_ref[...], preferred_element_type=F32)
    o_ref[...] = x
    ms = jnp.mean(x * x, axis=-1, keepdims=True)
    xn_ref[...] = (x * lax.rsqrt(ms + EPS) * g_ref[...]).astype(xn_ref.dtype)


def _res_proj_norm_big(a, a_s, w_stack, layer, res, res_s, scale, next_gain, *, name, norm_dtype=BF16, tm=256):
    m, k = a.shape
    rs = a_s.shape[0]
    n = w_stack.shape[2]
    assert m % tm == 0 and k % WEIGHT_CHUNK_ROWS == 0 and res.shape == (m, n)
    return pl.pallas_call(
        functools.partial(_res_proj_norm_big_body, scale=scale, layer=layer),
        grid=(m // tm,),
        in_specs=[pl.BlockSpec((tm, k), lambda i: (i, 0)),
                  pl.BlockSpec((rs, k), lambda i: (0, 0)),
                  pl.BlockSpec(memory_space=pl.ANY),
                  pl.BlockSpec((tm, n), lambda i: (i, 0)),
                  pl.BlockSpec((rs, n), lambda i: (0, 0)),
                  pl.BlockSpec((1, n), lambda i: (0, 0))],
        out_specs=[pl.BlockSpec((tm, n), lambda i: (i, 0)),
                   pl.BlockSpec((rs, n), lambda i: (0, 0)),
                   pl.BlockSpec((tm, n), lambda i: (i, 0))],
        out_shape=[jax.ShapeDtypeStruct((m, n), F32), jax.ShapeDtypeStruct((rs, n), F32),
                   jax.ShapeDtypeStruct((m, n), norm_dtype)],
        scratch_shapes=[pltpu.VMEM((k, n), BF16), pltpu.VMEM((2, WEIGHT_CHUNK_ROWS, n), F32),
                        pltpu.SemaphoreType.DMA((2,))],
        compiler_params=_cparams("arbitrary"),
        name=name,
    )(a, a_s, w_stack, res, res_s, next_gain.reshape(1, n))


def _hgrn_seq_body(q_ref, z_ref, v_ref, g_ref, lb_ref, gn_ref, tri_ref, y_ref, st_out_ref,
                   st_ref, qs_ref, ks_ref, cum_ref, o_ref, *, hb):
    @pl.when(pl.program_id(2) == 0)
    def _():
        st_ref[...] = jnp.zeros_like(st_ref)

    tt = q_ref.shape[0]
    lb = lb_ref[...]
    z = z_ref[...]
    log_sig = jnp.minimum(z, 0.0) - jnp.log(1.0 + jnp.exp(-jnp.abs(z)))
    a, b = jnp.log(lb), jnp.log1p(-lb) + log_sig
    log_f = jnp.maximum(a, b) + jnp.log(1.0 + jnp.exp(-jnp.abs(a - b)))
    ks_ref[...] = (1.0 - lb) * _sigmoid(-z)
    q = q_ref[...]
    qs_ref[...] = q * _sigmoid(q)
    cum_ref[...] = jnp.dot(tri_ref[...], log_f, preferred_element_type=F32, precision=lax.Precision.HIGHEST)

    t_idx = lax.broadcasted_iota(jnp.int32, (GLA_BLOCK, 1), 0)
    half = GLA_BLOCK // 2

    def block(j, carry):
        rows = pl.ds(pl.multiple_of(j * GLA_BLOCK, GLA_BLOCK), GLA_BLOCK)
        for h in range(hb):
            cols = slice(h * HG_DK, (h + 1) * HG_DK)
            cb = cum_ref[rows, cols]
            qb = qs_ref[rows, cols]
            kb = ks_ref[rows, cols]
            vb = v_ref[rows, cols]
            last = cb[GLA_BLOCK - 1:GLA_BLOCK, :]
            st = st_ref[h]
            o = lax.dot_general((qb * jnp.exp(cb)).astype(BF16), st.astype(BF16), NT_DIMS,
                                preferred_element_type=F32)
            o_lo = jnp.zeros((half, LANES), F32)
            o_hi = jnp.zeros((half, LANES), F32)
            for s in range(GLA_BLOCK):
                cs = cb[s:s + 1, :]
                qk_hi = qb[half:, :] * kb[s:s + 1, :]
                e_hi = jnp.exp(cb[half:, :] - cs)
                if s >= half:
                    e_hi = jnp.where(t_idx[half:] >= s, e_hi, 0.0)
                g_hi = e_hi * qk_hi
                o_hi = o_hi + jnp.sum(g_hi, axis=-1, keepdims=True) * vb[s:s + 1, :]
                if s < half:
                    qk_lo = qb[:half, :] * kb[s:s + 1, :]
                    g_lo = jnp.where(t_idx[:half] >= s, jnp.exp(cb[:half, :] - cs), 0.0) * qk_lo
                    o_lo = o_lo + jnp.sum(g_lo, axis=-1, keepdims=True) * vb[s:s + 1, :]
            o_ref[rows, cols] = o + jnp.concatenate([o_lo, o_hi], axis=0)
            kd = kb * jnp.exp(last - cb)
            upd = lax.dot_general(vb.astype(BF16), kd.astype(BF16), TN_DIMS, preferred_element_type=F32)
            st_ref[h] = st * jnp.exp(last) + upd
        return carry

    lax.fori_loop(0, tt // GLA_BLOCK, block, 0)

    for h in range(hb):
        cols = slice(h * HG_DK, (h + 1) * HG_DK)
        o = o_ref[:, cols]
        ms = jnp.mean(o * o, axis=-1, keepdims=True)
        g = g_ref[:, cols]
        y_ref[:, cols] = (o * lax.rsqrt(ms + EPS) * gn_ref[0, h:h + 1, :] * (g * _sigmoid(g))).astype(BF16)

    @pl.when(pl.program_id(2) == pl.num_programs(2) - 1)
    def _():
        st_out_ref[0] = st_ref[...]


def _hgrn_seq(proj, lower_bound, norm_gain, *, batch, seq_len, tt=256, hb=8):
    m, n4 = proj.shape
    d = n4 // 4
    heads = d // HG_DK
    hg = heads // hb
    nt = seq_len // tt
    wcols = hb * HG_DK
    tri = (jnp.arange(tt)[:, None] >= jnp.arange(tt)[None, :]) & (
        jnp.arange(tt)[:, None] // GLA_BLOCK == jnp.arange(tt)[None, :] // GLA_BLOCK)
    tri = tri.astype(F32)

    def col(sec):
        return pl.BlockSpec((tt, wcols), lambda b, h, t, sec=sec: (b * nt + t, sec * hg + h))

    y, st = pl.pallas_call(
        functools.partial(_hgrn_seq_body, hb=hb),
        grid=(batch, hg, nt),
        in_specs=[col(0), col(1), col(2), col(3),
                  pl.BlockSpec((1, wcols), lambda b, h, t: (0, h)),
                  pl.BlockSpec((1, hb, HG_DK), lambda b, h, t: (h, 0, 0)),
                  pl.BlockSpec((tt, tt), lambda b, h, t: (0, 0))],
        out_specs=[pl.BlockSpec((tt, wcols), lambda b, h, t: (b * nt + t, h)),
                   pl.BlockSpec((1, hb, HG_DK, HG_DK), lambda b, h, t: (b, h, 0, 0))],
        out_shape=[jax.ShapeDtypeStruct((m, d), BF16),
                   jax.ShapeDtypeStruct((batch, heads, HG_DK, HG_DK), F32)],
        scratch_shapes=[pltpu.VMEM((hb, HG_DK, HG_DK), F32), pltpu.VMEM((tt, wcols), F32),
                        pltpu.VMEM((tt, wcols), F32), pltpu.VMEM((tt, wcols), F32),
                        pltpu.VMEM((tt, wcols), F32)],
        compiler_params=_cparams("parallel", "parallel", "arbitrary"),
        name="hgrn_seq",
    )(proj, proj, proj, proj, lower_bound.reshape(1, d), norm_gain.reshape(hg, hb, HG_DK), tri)
    return y, st


def _hgrn_step_body(q_ref, z_ref, v_ref, g_ref, lb_ref, gn_ref, s_ref, y_ref, s_out_ref):
    heads = s_ref.shape[1]
    for h in range(heads):
        lb = lb_ref[h]
        z = z_ref[0, h]
        forget = lb + (1.0 - lb) * _sigmoid(z)
        k = (1.0 - lb) * _sigmoid(-z)
        q = q_ref[0, h]
        q = q * _sigmoid(q)
        s_new = forget * s_ref[0, h] + k * v_ref[0, h]
        s_out_ref[0, h] = s_new
        o = jnp.sum(q * s_new, axis=0, keepdims=True)
        ms = jnp.mean(o * o, axis=-1, keepdims=True)
        g = g_ref[0, h]
        y_ref[0, h] = o * lax.rsqrt(ms + EPS) * gn_ref[h] * (g * _sigmoid(g))


def _hgrn_step(proj, lower_bound, norm_gain, state):
    bsz, heads, dk, dv = state.shape
    d = heads * dk
    q, z, v, g = (proj[:, s * d:(s + 1) * d] for s in range(4))
    col = lambda a: a.reshape(bsz, heads, dk, 1)
    row = lambda a: a.reshape(bsz, heads, 1, dv)
    col_spec = pl.BlockSpec((1, heads, dk, 1), lambda b: (b, 0, 0, 0))
    row_spec = pl.BlockSpec((1, heads, 1, dv), lambda b: (b, 0, 0, 0))
    st_spec = pl.BlockSpec((1, heads, dk, dv), lambda b: (b, 0, 0, 0))
    y, s_new = pl.pallas_call(
        _hgrn_step_body,
        grid=(bsz,),
        in_specs=[col_spec, col_spec, row_spec, row_spec,
                  pl.BlockSpec((heads, dk, 1), lambda b: (0, 0, 0)),
                  pl.BlockSpec((heads, 1, dv), lambda b: (0, 0, 0)),
                  st_spec],
        out_specs=[row_spec, st_spec],
        out_shape=[jax.ShapeDtypeStruct((bsz, heads, 1, dv), F32),
                   jax.ShapeDtypeStruct(state.shape, F32)],
        compiler_params=_cparams("parallel"),
        name="hgrn_step",
    )(col(q), col(z), row(v), row(g), lower_bound.reshape(heads, dk, 1),
      norm_gain.reshape(heads, 1, dv), state)
    return y.reshape(bsz, d), s_new


def _band_block(q, k, v, scale, has_prev):
    n, nkeys = q.shape[0], k.shape[0]
    qi = lax.broadcasted_iota(jnp.int32, (n, nkeys), 0)
    ki = lax.broadcasted_iota(jnp.int32, (n, nkeys), 1)
    s = lax.dot_general(q.astype(BF16), k.astype(BF16), NT_DIMS, preferred_element_type=F32) * scale
    visible = ((ki >= qi) & (ki <= qi + n)) if has_prev else (ki <= qi)
    s = jnp.where(visible, s, NEG_BIG)
    m = jnp.max(s, axis=-1, keepdims=True)
    p = jnp.exp(s - m)
    l = jnp.sum(p, axis=-1, keepdims=True)
    acc = jnp.dot(p.astype(BF16), v.astype(BF16), preferred_element_type=F32)
    return acc / l, m + jnp.log(l)


def _dil_attn_seq_body(*refs, seq_len, hb):
    n_in = 3 * N_GROUPS
    y_ref, og_ref, lse_ref = refs[hb * n_in:]
    scale = ATT_DH ** -0.5
    for g, (win, dil) in enumerate(DILATED_GROUPS):
        nk = win // dil
        for r in range(dil):
            for n in range(seq_len // dil // nk):
                def rows(first_blk, n_blk):
                    start, size = r + dil * nk * first_blk, nk * n_blk
                    return pl.ds(start, size, stride=dil) if dil > 1 else pl.ds(start, size)
                cur = rows(n, 1)
                keys = rows(n - 1, 2) if n else cur
                for h in range(hb):
                    q_ref, k_ref, v_ref = refs[h * n_in + 3 * g:h * n_in + 3 * g + 3]
                    o, lse = _band_block(q_ref[0, cur, :], k_ref[0, keys, :], v_ref[0, keys, :], scale, n > 0)
                    og_ref[h * N_GROUPS + g, cur, :] = o
                    lse_ref[h * N_GROUPS + g, cur, :] = jnp.broadcast_to(lse, (nk, LANES))
    for h in range(hb):
        lse = [lse_ref[h * N_GROUPS + g] for g in range(N_GROUPS)]
        og = [og_ref[h * N_GROUPS + g] for g in range(N_GROUPS)]
        m = jnp.maximum(jnp.maximum(lse[0], lse[1]), lse[2])
        w = [jnp.exp(x - m) for x in lse]
        y = (w[0] * og[0] + w[1] * og[1] + w[2] * og[2]) / (w[0] + w[1] + w[2])
        y_ref[0, :, h * ATT_DH:(h + 1) * ATT_DH] = y.astype(BF16)


def _dil_attn_seq(qkv, *, batch, seq_len, heads, hb=1):
    in_specs = []
    for h in range(hb):
        for s in range(3 * N_GROUPS):
            in_specs.append(pl.BlockSpec((1, seq_len, ATT_DH),
                                         lambda b, hg, s=s, h=h: (b, 0, s * heads + hg * hb + h)))
    return pl.pallas_call(
        functools.partial(_dil_attn_seq_body, seq_len=seq_len, hb=hb),
        grid=(batch, heads // hb),
        in_specs=in_specs,
        out_specs=pl.BlockSpec((1, seq_len, hb * ATT_DH), lambda b, hg: (b, 0, hg)),
        out_shape=jax.ShapeDtypeStruct((batch, seq_len, heads * ATT_DH), BF16),
        scratch_shapes=[pltpu.VMEM((hb * N_GROUPS, seq_len, ATT_DH), F32),
                        pltpu.VMEM((hb * N_GROUPS, seq_len, LANES), F32)],
        compiler_params=_cparams("parallel", "parallel"),
        name="dil_attn_seq",
    )(*([qkv] * (hb * 3 * N_GROUPS)))


def _bf16_round(x):
    return x.astype(BF16).astype(F32)


def _step_attention(q, k_new, v_new, k_c, v_c, scale):
    qb = _bf16_round(q)
    s_c = jnp.sum(_bf16_round(k_c) * qb[None], axis=-1, keepdims=True) * scale
    m = jnp.max(s_c, axis=0)
    if k_new is not None:
        s_n = jnp.sum(qb * _bf16_round(k_new), axis=-1, keepdims=True) * scale
        m = jnp.maximum(m, s_n)
    p_c = jnp.exp(s_c - m[None])
    l = jnp.sum(p_c, axis=0)
    acc = jnp.sum(_bf16_round(p_c) * _bf16_round(v_c), axis=0)
    if k_new is not None:
        p_n = jnp.exp(s_n - m)
        l = l + p_n
        acc = acc + _bf16_round(p_n) * _bf16_round(v_new)
    return acc / l, m + jnp.log(l)


def _dil_attn_step_body(*refs):
    new = refs[:3 * N_GROUPS]
    caches = refs[3 * N_GROUPS:5 * N_GROUPS]
    y_ref = refs[5 * N_GROUPS]
    scale = ATT_DH ** -0.5
    outs, lses = [], []
    for g in range(N_GROUPS):
        o, lse = _step_attention(new[3 * g][0, 0], new[3 * g + 1][0, 0], new[3 * g + 2][0, 0],
                                 caches[2 * g][...], caches[2 * g + 1][...], scale)
        outs.append(o)
        lses.append(lse)
    m = jnp.maximum(jnp.maximum(lses[0], lses[1]), lses[2])
    w = [jnp.exp(x - m) for x in lses]
    y_ref[0] = (w[0] * outs[0] + w[1] * outs[1] + w[2] * outs[2]) / (w[0] + w[1] + w[2])


def _dil_attn_step(qkv, caches, layer, *, heads):
    bsz = qkv.shape[0]
    new = qkv.reshape(bsz, 3 * N_GROUPS, heads, ATT_DH)
    args, in_specs = [], []
    for s in range(3 * N_GROUPS):
        args.append(new)
        in_specs.append(pl.BlockSpec((1, 1, heads, ATT_DH), lambda b, s=s: (b, s, 0, 0)))
    for g, (win, dil) in enumerate(DILATED_GROUPS):
        nk = win // dil
        for buf in caches[g]:
            assert buf.shape[1:] == (bsz, win, heads, ATT_DH)
            args.append(buf.reshape(buf.shape[0], bsz, nk, dil, heads, ATT_DH))
            in_specs.append(pl.BlockSpec((None, None, nk, None, heads, ATT_DH),
                                         lambda b: (layer, b, 0, 0, 0, 0)))
    y = pl.pallas_call(
        _dil_attn_step_body,
        grid=(bsz,),
        in_specs=in_specs,
        out_specs=pl.BlockSpec((1, heads, ATT_DH), lambda b: (b, 0, 0)),
        out_shape=jax.ShapeDtypeStruct((bsz, heads, ATT_DH), F32),
        compiler_params=_cparams("parallel"),
        name="dil_attn_step",
    )(*args)
    return y.reshape(bsz, heads * ATT_DH)


def _softmax_pv(s, v):
    m = jnp.max(s, axis=-1, keepdims=True)
    p = jnp.exp(s - m)
    l = jnp.sum(p, axis=-1, keepdims=True)
    return jnp.dot(p.astype(BF16), v.astype(BF16), preferred_element_type=F32) / l


def _memory_kv_body(x_ref, g_ref, w_ref, o_ref, wb_ref, xn_ref):
    _cast_weight(w_ref, wb_ref, 0)

    def body(c, carry):
        rows = pl.ds(pl.multiple_of(c * CAST_ROWS, CAST_ROWS), CAST_ROWS)
        x = x_ref[rows, :]
        ms = jnp.mean(x * x, axis=-1, keepdims=True)
        xn_ref[rows, :] = (x * lax.rsqrt(ms + EPS) * g_ref[0]).astype(BF16)
        return carry
    lax.fori_loop(0, x_ref.shape[0] // CAST_ROWS, body, 0)
    o_ref[0] = jnp.dot(xn_ref[...], wb_ref[0], preferred_element_type=F32)


def _memory_kv(mem, gains, w_kv, *, tn=512):
    rows, d = mem.shape
    layers, _, n = w_kv.shape
    assert rows % CAST_ROWS == 0 and d % CAST_ROWS == 0 and n % tn == 0
    return pl.pallas_call(
        _memory_kv_body,
        grid=(layers, n // tn),
        in_specs=[pl.BlockSpec((rows, d), lambda l, j: (0, 0)),
                  pl.BlockSpec((1, 1, d), lambda l, j: (l, 0, 0)),
                  pl.BlockSpec((1, d, tn), lambda l, j: (l, 0, j))],
        out_specs=pl.BlockSpec((1, rows, tn), lambda l, j: (l, 0, j)),
        out_shape=jax.ShapeDtypeStruct((layers, rows, n), F32),
        scratch_shapes=[pltpu.VMEM((1, d, tn), BF16), pltpu.VMEM((rows, d), BF16)],
        compiler_params=_cparams("parallel", "parallel"),
        name="memory_kv",
    )(mem, gains.reshape(layers, 1, d), w_kv)


def _stream_weight(w_hbm, wb_ref, stage_ref, sem):
    chunk = stage_ref.shape[1]
    k, n = wb_ref.shape
    n_chunks = k // chunk

    def copy(c):
        return pltpu.make_async_copy(w_hbm.at[pl.ds(c * chunk, chunk), :], stage_ref.at[c % 2, :, pl.ds(0, n)],
                                     sem.at[c % 2])
    copy(0).start()
    for c in range(n_chunks):
        if c + 1 < n_chunks:
            copy(c + 1).start()
        copy(c).wait()
        wb_ref[pl.ds(c * chunk, chunk), :] = stage_ref[c % 2, :, 0:n].astype(BF16)


def _mix_xattn_block_body(y_ref, ys_ref, r_ref, rs_ref, wm_hbm, gq_ref, wq_hbm, wo_hbm, kv_ref, ks_ref, vs_ref,
                          gn_ref, o_ref, os_ref, xnn_ref, wmb_ref, wqb_ref, wob_ref, att_ref, stage_ref, sem,
                          *, heads, nbat_s, mix_layer, layer):
    scale = XA_DH ** -0.5
    hd = heads * XA_DH

    def rms(x, g_ref):
        ms = jnp.mean(x * x, axis=-1, keepdims=True)
        return (x * lax.rsqrt(ms + EPS) * g_ref[...]).astype(BF16)

    @pl.when(pl.program_id(0) == 0)
    def _():
        _stream_weight(wm_hbm.at[mix_layer], wmb_ref, stage_ref, sem)
        _stream_weight(wq_hbm.at[layer], wqb_ref, stage_ref, sem)
        _stream_weight(wo_hbm.at[layer], wob_ref, stage_ref, sem)
        xs = rs_ref[...] + jnp.dot(ys_ref[...], wmb_ref[...], preferred_element_type=F32)
        q_s = jnp.dot(rms(xs, gq_ref), wqb_ref[...], preferred_element_type=F32)
        att_ref[...] = jnp.zeros_like(att_ref)
        for b in range(nbat_s):
            for h in range(heads):
                cols = slice(h * XA_DH, (h + 1) * XA_DH)
                qb = _bf16_round(q_s[b:b + 1, cols])
                s = jnp.sum(_bf16_round(ks_ref[b, :, h, :]) * qb, axis=-1, keepdims=True) * scale
                p = jnp.exp(s - jnp.max(s, axis=0, keepdims=True))
                acc = jnp.sum(_bf16_round(p) * _bf16_round(vs_ref[b, :, h, :]), axis=0, keepdims=True)
                att_ref[b:b + 1, cols] = acc / jnp.sum(p, axis=0, keepdims=True)
        os_ref[...] = xs + jnp.dot(att_ref[0:xs.shape[0], :].astype(BF16), wob_ref[...], preferred_element_type=F32)

    x = r_ref[...] + jnp.dot(y_ref[...], wmb_ref[...], preferred_element_type=F32)
    q = jnp.dot(rms(x, gq_ref), wqb_ref[...], preferred_element_type=F32)
    for h in range(heads):
        cols = slice(h * XA_DH, (h + 1) * XA_DH)
        k = kv_ref[0][:, cols]
        v = kv_ref[0][:, hd + h * XA_DH:hd + (h + 1) * XA_DH]
        s = lax.dot_general(q[:, cols].astype(BF16), k.astype(BF16), NT_DIMS, preferred_element_type=F32) * scale
        att_ref[:, cols] = _softmax_pv(s, v)
    x = x + jnp.dot(att_ref[...].astype(BF16), wob_ref[...], preferred_element_type=F32)
    o_ref[...] = x
    xnn_ref[...] = rms(x, gn_ref)


def _mix_xattn_block(y, y_s, xp, xs, w_mix, mix_layer, gain, w_q, w_o, layer, mem_kv, mem_k_s, mem_v_s, next_gain,
                     *, nbat, t, nbat_s, tm=256):
    m, d = xp.shape
    rs = xs.shape[0]
    hd = w_q.shape[2]
    heads = hd // XA_DH
    mem_len = mem_kv.shape[2]
    tiles = t // tm
    assert t % tm == 0 and tm >= rs and mem_k_s.shape[1:] == (nbat_s, mem_len, heads, XA_DH)
    assert d % WEIGHT_CHUNK_ROWS == 0 and hd % WEIGHT_CHUNK_ROWS == 0 and hd <= d
    once = dict(pipeline_mode=pl.Buffered(1))
    mem_s_spec = pl.BlockSpec((None, nbat_s, mem_len, heads, XA_DH), lambda i: (layer, 0, 0, 0, 0), **once)
    row = pl.BlockSpec((tm, d), lambda i: (i, 0))
    srow = pl.BlockSpec((rs, d), lambda i: (0, 0))
    vec = pl.BlockSpec((1, d), lambda i: (0, 0))
    hbm = pl.BlockSpec(memory_space=pl.ANY)
    return pl.pallas_call(
        functools.partial(_mix_xattn_block_body, heads=heads, nbat_s=nbat_s, mix_layer=mix_layer, layer=layer),
        grid=(m // tm,),
        in_specs=[row, srow, row, srow, hbm, vec, hbm, hbm,
                  pl.BlockSpec((None, 1, mem_len, 2 * hd), lambda i: (layer, i // tiles, 0, 0)),
                  mem_s_spec, mem_s_spec, vec],
        out_specs=[row, srow, row],
        out_shape=[jax.ShapeDtypeStruct((m, d), F32), jax.ShapeDtypeStruct((rs, d), F32),
                   jax.ShapeDtypeStruct((m, d), BF16)],
        scratch_shapes=[pltpu.VMEM((d, d), BF16), pltpu.VMEM((d, hd), BF16), pltpu.VMEM((hd, d), BF16),
                        pltpu.VMEM((tm, hd), F32), pltpu.VMEM((2, WEIGHT_CHUNK_ROWS, d), F32),
                        pltpu.SemaphoreType.DMA((2,))],
        compiler_params=_cparams("arbitrary"),
        name="mix_xattn_block",
    )(y, y_s, xp, xs, w_mix, gain.reshape(1, d), w_q, w_o, mem_kv, mem_k_s, mem_v_s, next_gain.reshape(1, d))


TN_COLS = 512


def _rope_tables(pos):
    rows = pos.shape[0]
    half = ROPE_DIM // 2
    inv_freq = ROPE_THETA ** (-jnp.arange(half, dtype=F32) * 2.0 / ROPE_DIM)
    ang = pos.astype(F32)[:, None] * inv_freq[None, :]
    cos, sin = jnp.cos(ang), jnp.sin(ang)
    rest = jnp.zeros((rows, LANES - ROPE_DIM), F32)
    zh = jnp.zeros((rows, half), F32)
    return (jnp.concatenate([cos, cos, rest + 1.0], axis=1), jnp.concatenate([-sin, zh, rest], axis=1),
            jnp.concatenate([zh, sin, rest], axis=1))


def _pad_rows(a, rows):
    return jnp.pad(a, ((0, rows - a.shape[0]), (0, 0)))


ROWS_S = 16
TM_ROWS = 512
TN_WIDE = 1024
TM_GATE_UP = 2048


def _ffn(xp, xs, xn, gain, w_gu, w_down, layer, tag, next_gain, norm_dtype=BF16):
    nf = w_down.shape[1] // TN_COLS
    (hid,), (hid_s,) = _proj(xn, xs, gain, w_gu, layer, w_col_blocks=(0, nf), n_col_blocks=nf,
                             epilogue=_ep_swiglu, out_dtypes=(BF16,), name=tag + "_gate_up", tm=TM_GATE_UP)
    return _res_proj_norm_big(hid, hid_s, w_down, layer, xp, xs, 0.5, next_gain, name=tag + "_down",
                              norm_dtype=norm_dtype)


def _conv_mixer(xp, xs, xn, gain, w_in, w_conv, layer, state, *, nbat, t, nbat_s):
    d = xp.shape[1]
    nb = d // TN_COLS
    tm = min(1024, t)
    tiles = t // tm
    w_spec = pl.BlockSpec((CONV_W, TN_COLS), lambda j, i: (0, j))
    st_spec = pl.BlockSpec((ROWS_S, TN_COLS), lambda j, i: (0, j))
    tail_out = (jax.ShapeDtypeStruct((nbat * SUBLANES, d), F32),
                pl.BlockSpec((SUBLANES, TN_COLS), lambda j, i: (i // tiles, j)))
    (gated, tail), (gated_s, u_s) = _proj(
        xn, xs, gain, w_in, layer, w_col_blocks=(0, nb, 2 * nb), n_col_blocks=nb, name="conv_in", tm=tm,
        epilogue=functools.partial(_ep_conv_seq, tiles_per_seq=tiles), out_dtypes=(BF16,), more_outs=(tail_out,),
        epilogue_s=_ep_conv_step, out_s_dtypes=(BF16, F32),
        extra=(w_conv,), extra_specs=(w_spec,),
        extra_s=(_pad_rows(state[:, 0], ROWS_S), _pad_rows(state[:, 1], ROWS_S), w_conv),
        extra_s_specs=(st_spec, st_spec, w_spec),
        col_chunk=MXU_COLS,
        scratch=(pltpu.VMEM((SUBLANES, TN_COLS), F32),
                 pltpu.VMEM((TN_COLS // MXU_COLS, tm + SUBLANES, MXU_COLS), F32)))
    state_p = tail.reshape(nbat, SUBLANES, d)[:, SUBLANES - (CONV_W - 1):]
    state_s = jnp.stack([state[:, 1], u_s[:nbat_s]], axis=1)
    return gated, gated_s, state_p, state_s


def _hgrn_mixer(xp, xs, xn, gain, w_in, lower_bound, norm_gain, layer, state, *, nbat, t, nbat_s):
    (proj,), (proj_s,) = _proj(xn, xs, gain, w_in, layer, w_col_blocks=(0,), n_col_blocks=w_in.shape[2] // TN_WIDE,
                               epilogue=_ep_plain, out_dtypes=(F32,), name="hgrn_in", tn=TN_WIDE)
    y, st = _hgrn_seq(proj, lower_bound, norm_gain, batch=nbat, seq_len=t)
    y_s, state_s = _hgrn_step(proj_s[:nbat_s], lower_bound, norm_gain, state)
    return y, _pad_rows(y_s, ROWS_S).astype(BF16), jnp.swapaxes(st, -1, -2), state_s


def _attn_mixer(xp, xs, xn, gain, w_qkv, layer, caches, *, nbat, t, nbat_s):
    d = xp.shape[1]
    heads = d // ATT_DH
    n_sec = 3 * N_GROUPS
    sec_blocks = d // TN_WIDE
    tm = min(1024, t)
    tiles = t // tm
    tab_spec = pl.BlockSpec((tm, LANES), lambda j, i: (i % tiles, 0))
    tab_s_spec = pl.BlockSpec((ROWS_S, LANES), lambda j, i: (0, 0))
    (qkv,), (qkv_s,) = _proj(
        xn, xs, gain, w_qkv, layer, w_col_blocks=(0,), n_col_blocks=n_sec * sec_blocks,
        epilogue=_ep_rotary, out_dtypes=(F32,), name="attn_qkv", tm=tm, tn=TN_WIDE,
        alt_epilogue=_ep_plain, alt_pred=lambda j: (j // sec_blocks) % 3 == 2,
        extra=_rope_tables(jnp.arange(t)), extra_specs=(tab_spec,) * 3,
        extra_s=_rope_tables(jnp.full((ROWS_S,), PAST_LEN, jnp.int32)), extra_s_specs=(tab_s_spec,) * 3)
    y = _dil_attn_seq(qkv.reshape(nbat, t, n_sec * d), batch=nbat, seq_len=t, heads=heads).reshape(nbat * t, d)
    y_s = _dil_attn_step(qkv_s[:nbat_s], caches, layer, heads=heads)
    q5 = qkv.reshape(nbat, t, N_GROUPS, 3, heads, ATT_DH)
    q5_s = qkv_s[:nbat_s].reshape(nbat_s, 1, N_GROUPS, 3, heads, ATT_DH)
    rows_p, rows_s = [], []
    for g, (win, _) in enumerate(DILATED_GROUPS):
        keep = min(win, t)
        rows_p += [q5[:, t - keep:, g, 1], q5[:, t - keep:, g, 2]]
        rows_s += [q5_s[:, :, g, 1], q5_s[:, :, g, 2]]
    return y, _pad_rows(y_s, ROWS_S).astype(BF16), rows_p, rows_s


def kernel(x_prompt, x_sample, state_conv, state_hgrn,
           cache_win_k0, cache_win_v0, cache_win_k1, cache_win_v1, cache_win_k2, cache_win_v2,
           cache_mem_k, cache_mem_v, mem_prompt,
           norm_ffn1, ffn1_w_gu, ffn1_w_down, norm_mix,
           conv_w_in, conv_w, conv_w_out,
           hgrn_w_in, hgrn_lb_logits, hgrn_norm, hgrn_w_out,
           attn_w_qkv, attn_w_out,
           norm_mem, xattn_w_kv, norm_xattn, xattn_w_q, xattn_w_o,
           norm_ffn2, ffn2_w_gu, ffn2_w_down, norm_final):
    batch, seq, d = x_prompt.shape
    dec_batch, dec_seq, _ = x_sample.shape
    assert dec_seq == 1 and dec_batch <= ROWS_S
    depth = norm_ffn1.shape[0]
    mem_len = mem_prompt.shape[1]
    xa_hd = xattn_w_q.shape[2]
    xa_heads = xa_hd // XA_DH
    win_cache = [(cache_win_k0, cache_win_v0), (cache_win_k1, cache_win_v1), (cache_win_k2, cache_win_v2)]
    sizes = dict(nbat=batch, t=seq, nbat_s=dec_batch)

    lb_p = jax.nn.softmax(hgrn_lb_logits.astype(F32), axis=0)
    lower_bounds = jnp.cumsum(lb_p, axis=0) - lb_p[0]

    xp = x_prompt.reshape(batch * seq, d)
    xs = _pad_rows(x_sample.reshape(dec_batch, d), ROWS_S)
    mem_kv = _memory_kv(mem_prompt.reshape(batch * mem_len, d), norm_mem, xattn_w_kv)
    mem_kv = mem_kv.reshape(depth, batch, mem_len, 2 * xa_hd)
    p_mem_k = mem_kv[..., :xa_hd].reshape(depth, batch, mem_len, xa_heads, XA_DH)
    p_mem_v = mem_kv[..., xa_hd:].reshape(depth, batch, mem_len, xa_heads, XA_DH)
    p_conv, p_hgrn, p_win, s_conv, s_hgrn, s_win = ([] for _ in range(6))
    xn = _rms_norm(xp, norm_ffn1[0], tm=TM_ROWS, out_dtype=BF16, name="first_norm")
    for i in range(depth):
        j, kind = divmod(i, N_MIXERS)
        xp, xs, xn = _ffn(xp, xs, xn, norm_ffn1[i], ffn1_w_gu, ffn1_w_down, i, "ffn1", norm_mix[i])
        if kind == 0:
            y, y_s, st_p, st_s = _conv_mixer(xp, xs, xn, norm_mix[i], conv_w_in, conv_w[j], j, state_conv[j], **sizes)
            w_mix = conv_w_out
            p_conv.append(st_p)
            s_conv.append(st_s)
        elif kind == 1:
            y, y_s, st_p, st_s = _hgrn_mixer(xp, xs, xn, norm_mix[i], hgrn_w_in, lower_bounds[i], hgrn_norm[j],
                                             j, state_hgrn[j], **sizes)
            w_mix = hgrn_w_out
            p_hgrn.append(st_p)
            s_hgrn.append(st_s)
        else:
            y, y_s, rows_p, rows_s = _attn_mixer(xp, xs, xn, norm_mix[i], attn_w_qkv, j, win_cache, **sizes)
            w_mix = attn_w_out
            p_win.append(rows_p)
            s_win.append(rows_s)
        xp, xs, xn = _mix_xattn_block(y, y_s, xp, xs, w_mix, j, norm_xattn[i], xattn_w_q, xattn_w_o, i, mem_kv,
                                      cache_mem_k, cache_mem_v, norm_ffn2[i], **sizes)
        last = i == depth - 1
        xp, xs, xn = _ffn(xp, xs, xn, norm_ffn2[i], ffn2_w_gu, ffn2_w_down, i, "ffn2",
                          norm_final if last else norm_ffn1[i + 1], F32 if last else BF16)
    y_prompt = xn.reshape(batch, seq, d)
    y_sample = _rms_norm(xs, norm_final, tm=ROWS_S, out_dtype=F32, name="final_norm_s")[:dec_batch]
    y_sample = y_sample.reshape(dec_batch, 1, d)

    stack_win = lambda rows: [jnp.stack([r[g] for r in rows]) for g in range(2 * N_GROUPS)]
    return (y_prompt, y_sample,
            jnp.stack(p_conv), jnp.stack(p_hgrn), *stack_win(p_win), p_mem_k, p_mem_v,
            jnp.stack(s_conv), jnp.stack(s_hgrn), *stack_win(s_win))
```

```python
import functools

import jax
import jax.numpy as jnp
from jax import lax
from jax.experimental import pallas as pl
from jax.experimental.pallas import tpu as pltpu

F32 = jnp.float32
BF16 = jnp.bfloat16
EPS = 1e-6
LANES = 128
SUBLANES = 8
MXU_COLS = 256
VMEM_LIMIT_BYTES = 56 << 20
NEG_BIG = -1e30

N_MIXERS = 3
CONV_W = 3
HG_DK = 128
ATT_DH = 128
DILATED_GROUPS = ((128, 1), (512, 4), (2048, 16))
N_GROUPS = 3
ROPE_DIM = ATT_DH // 4
ROPE_THETA = 500000.0
XA_DH = 128
GLA_BLOCK = 16
PAST_LEN = 16384

NT_DIMS = (((1,), (1,)), ((), ()))
TN_DIMS = (((0,), (0,)), ((), ()))


def _cparams(*semantics):
    return pltpu.CompilerParams(dimension_semantics=semantics, vmem_limit_bytes=VMEM_LIMIT_BYTES)


def _sigmoid(x):
    return 1.0 / (1.0 + jnp.exp(-x))


def _rms_body(x_ref, g_ref, o_ref):
    x = x_ref[...]
    ms = jnp.mean(x * x, axis=-1, keepdims=True)
    o_ref[...] = (x * lax.rsqrt(ms + EPS) * g_ref[...]).astype(o_ref.dtype)


def _rms_norm(x, gain, *, tm, out_dtype, name):
    m, k = x.shape
    return pl.pallas_call(
        _rms_body,
        grid=(m // tm,),
        in_specs=[pl.BlockSpec((tm, k), lambda i: (i, 0)), pl.BlockSpec((1, k), lambda i: (0, 0))],
        out_specs=pl.BlockSpec((tm, k), lambda i: (i, 0)),
        out_shape=jax.ShapeDtypeStruct((m, k), out_dtype),
        compiler_params=_cparams("parallel"),
        name=name,
    )(x, gain.reshape(1, k))


CAST_ROWS = 256


def _cast_weight(w_ref, wb_ref, slot):
    def body(c, carry):
        rows = pl.ds(pl.multiple_of(c * CAST_ROWS, CAST_ROWS), CAST_ROWS)
        wb_ref[slot, rows, :] = w_ref[0, rows, :].astype(BF16)
        return carry
    lax.fori_loop(0, w_ref.shape[1] // CAST_ROWS, body, 0)


def _ep_plain(accs, extra, outs, col0):
    acc = accs[0]
    outs[0][:, col0:col0 + acc.shape[1]] = acc.astype(outs[0].dtype)


def _ep_swiglu(accs, extra, outs, col0):
    gate, up = accs
    outs[0][:, col0:col0 + gate.shape[1]] = (gate * _sigmoid(gate) * up).astype(BF16)


def _ep_conv_seq(accs, extra, outs, col0, scratch, *, tiles_per_seq):
    b_gate, c_gate, z = accs
    (w_ref,), (gated_ref, tail_ref), (carry_ref, ue_ref) = extra, outs, scratch
    tm, width = z.shape
    cols = slice(col0, col0 + width)
    ue = ue_ref.at[col0 // width]
    u = c_gate * z
    first = (pl.program_id(1) % tiles_per_seq) == 0
    ue[0:SUBLANES, :] = jnp.where(first, 0.0, carry_ref[:, cols])
    ue[SUBLANES:SUBLANES + tm, :] = u
    w = w_ref[:, cols]
    conv = (w[0:1] * ue[pl.ds(SUBLANES - 2, tm), :] + w[1:2] * ue[pl.ds(SUBLANES - 1, tm), :]
            + w[2:3] * u)
    gated_ref[:, cols] = (b_gate * conv).astype(BF16)
    tail = u[tm - SUBLANES:, :]
    carry_ref[:, cols] = tail
    tail_ref[:, cols] = tail


def _ep_conv_step(accs, extra, outs, col0):
    b_gate, c_gate, z = accs
    s0_ref, s1_ref, w_ref = extra
    u = c_gate * z
    w = w_ref[...]
    conv = w[0:1] * s0_ref[...] + w[1:2] * s1_ref[...] + w[2:3] * u
    outs[0][...] = (b_gate * conv).astype(BF16)
    outs[1][...] = u


def _ep_rotary(accs, extra, outs, col0):
    cos, sin_lo, sin_hi = (r[...] for r in extra)
    acc = accs[0]
    half = ROPE_DIM // 2
    for c in range(acc.shape[1] // LANES):
        xc = acc[:, c * LANES:(c + 1) * LANES]
        outs[0][:, col0 + c * LANES:col0 + (c + 1) * LANES] = (
            xc * cos + pltpu.roll(xc, LANES - half, 1) * sin_lo + pltpu.roll(xc, half, 1) * sin_hi)


def _proj_body(*refs, n_w, n_extra, n_extra_s, n_out, n_out_s, has_sample, epilogue, epilogue_s, col_chunk,
               alt_epilogue, alt_pred):
    refs = list(refs)
    take = lambda n: [refs.pop(0) for _ in range(n)]
    (x_ref,) = take(1)
    xs_ref, gs_ref = take(2) if has_sample else (None, None)
    w_refs = take(n_w)
    extra = take(n_extra)
    extra_s = take(n_extra_s)
    outs = take(n_out)
    outs_s = take(n_out_s)
    wb_ref, *scratch = refs

    @pl.when(pl.program_id(1) == 0)
    def _():
        for k, w_ref in enumerate(w_refs):
            _cast_weight(w_ref, wb_ref, k)
        if has_sample:
            xs = xs_ref[...]
            ms = jnp.mean(xs * xs, axis=-1, keepdims=True)
            xsn = (xs * lax.rsqrt(ms + EPS) * gs_ref[...]).astype(BF16)
            accs_s = [jnp.dot(xsn, wb_ref[k], preferred_element_type=F32) for k in range(n_w)]
            if alt_epilogue is None:
                epilogue_s(accs_s, extra_s, outs_s, 0)
            else:
                use_alt = alt_pred(pl.program_id(0))
                pl.when(use_alt)(lambda: alt_epilogue(accs_s, extra_s, outs_s, 0))
                pl.when(jnp.logical_not(use_alt))(lambda: epilogue_s(accs_s, extra_s, outs_s, 0))

    def tile(ep):
        x = x_ref[...]
        for c0 in range(0, wb_ref.shape[2], col_chunk):
            accs = [jnp.dot(x, wb_ref[k, :, c0:c0 + col_chunk], preferred_element_type=F32) for k in range(n_w)]
            ep(accs, extra, outs, c0, *([scratch] if scratch else []))

    if alt_epilogue is None:
        tile(epilogue)
    else:
        use_alt = alt_pred(pl.program_id(0))
        pl.when(use_alt)(lambda: tile(alt_epilogue))
        pl.when(jnp.logical_not(use_alt))(lambda: tile(epilogue))


def _proj(xn, xs, gain, w_stack, layer, *, w_col_blocks, n_col_blocks, epilogue, out_dtypes, name,
          tm=1024, tn=512, col_chunk=None, extra=(), extra_specs=(), extra_s=(), extra_s_specs=(),
          epilogue_s=None, out_s_dtypes=None, more_outs=(), scratch=(), alt_epilogue=None, alt_pred=None):
    m, k = xn.shape
    tm = min(tm, m)
    has_sample = xs is not None
    assert m % tm == 0 and w_stack.shape[1] == k and k % CAST_ROWS == 0
    out_s_dtypes = (out_dtypes if out_s_dtypes is None else out_s_dtypes) if has_sample else ()
    n_total = n_col_blocks * tn
    args = [xn]
    in_specs = [pl.BlockSpec((tm, k), lambda j, i: (i, 0))]
    if has_sample:
        rs = xs.shape[0]
        args += [xs, gain.reshape(1, k)]
        in_specs += [pl.BlockSpec((rs, k), lambda j, i: (0, 0)), pl.BlockSpec((1, k), lambda j, i: (0, 0))]
    for off in w_col_blocks:
        args.append(w_stack)
        in_specs.append(pl.BlockSpec((1, k, tn), lambda j, i, off=off: (layer, 0, j + off)))
    args += list(extra) + list(extra_s)
    in_specs += list(extra_specs) + list(extra_s_specs)
    out_shape = [jax.ShapeDtypeStruct((m, n_total), dt) for dt in out_dtypes] + [o[0] for o in more_outs]
    out_specs = [pl.BlockSpec((tm, tn), lambda j, i: (i, j)) for _ in out_dtypes] + [o[1] for o in more_outs]
    n_out = len(out_shape)
    if has_sample:
        out_shape += [jax.ShapeDtypeStruct((rs, n_total), dt) for dt in out_s_dtypes]
        out_specs += [pl.BlockSpec((rs, tn), lambda j, i: (0, j)) for _ in out_s_dtypes]
    res = pl.pallas_call(
        functools.partial(_proj_body, n_w=len(w_col_blocks), n_extra=len(extra), n_extra_s=len(extra_s),
                          n_out=n_out, n_out_s=len(out_s_dtypes), has_sample=has_sample,
                          epilogue=epilogue, epilogue_s=epilogue_s or epilogue, col_chunk=col_chunk or tn,
                          alt_epilogue=alt_epilogue, alt_pred=alt_pred),
        grid=(n_col_blocks, m // tm),
        in_specs=in_specs,
        out_specs=out_specs,
        out_shape=out_shape,
        scratch_shapes=[pltpu.VMEM((len(w_col_blocks), k, tn), BF16)] + list(scratch),
        compiler_params=_cparams("arbitrary", "arbitrary"),
        name=name,
    )(*args)
    return (res[:n_out], res[n_out:]) if has_sample else (res, None)


WEIGHT_CHUNK_ROWS = 512


def _res_proj_norm_big_body(a_ref, as_ref, w_hbm, r_ref, rs_ref, g_ref, o_ref, os_ref, xn_ref,
                            wb_ref, stage_ref, sem, *, scale, layer):
    def norm(x):
        ms = jnp.mean(x * x, axis=-1, keepdims=True)
        xn_ref[...] = (x * lax.rsqrt(ms + EPS) * g_ref[...]).astype(xn_ref.dtype)

    @pl.when(pl.program_id(0) == 0)
    def _():
        chunk = stage_ref.shape[1]
        n_chunks = wb_ref.shape[0] // chunk

        def copy(c):
            return pltpu.make_async_copy(w_hbm.at[layer, pl.ds(c * chunk, chunk), :], stage_ref.at[c % 2],
                                         sem.at[c % 2])
        copy(0).start()
        o_ref[...] = r_ref[...]
        for c in range(n_chunks):
            if c + 1 < n_chunks:
                copy(c + 1).start()
            copy(c).wait()
            rows = pl.ds(c * chunk, chunk)
            wb_ref[rows, :] = stage_ref[c % 2].astype(BF16)
            o_ref[...] += scale * jnp.dot(a_ref[:, c * chunk:(c + 1) * chunk], wb_ref[rows, :],
                                          preferred_element_type=F32)
        os_ref[...] = rs_ref[...] + scale * jnp.dot(as_ref[...], wb_ref[...], preferred_element_type=F32)
        norm(o_ref[...])

    @pl.when(pl.program_id(0) != 0)
    def _():
        x = r_ref[...] + scale * jnp.dot(a_ref[...], wb_ref[...], preferred_element_type=F32)
        o_ref[...] = x
        norm(x)


def _res_proj_norm_big(a, a_s, w_stack, layer, res, res_s, scale, next_gain, *, name, norm_dtype=BF16, tm=256):
    m, k = a.shape
    rs = a_s.shape[0]
    n = w_stack.shape[2]
    assert m % tm == 0 and k % WEIGHT_CHUNK_ROWS == 0 and res.shape == (m, n)
    return pl.pallas_call(
        functools.partial(_res_proj_norm_big_body, scale=scale, layer=layer),
        grid=(m // tm,),
        in_specs=[pl.BlockSpec((tm, k), lambda i: (i, 0)),
                  pl.BlockSpec((rs, k), lambda i: (0, 0)),
                  pl.BlockSpec(memory_space=pl.ANY),
                  pl.BlockSpec((tm, n), lambda i: (i, 0)),
                  pl.BlockSpec((rs, n), lambda i: (0, 0)),
                  pl.BlockSpec((1, n), lambda i: (0, 0))],
        out_specs=[pl.BlockSpec((tm, n), lambda i: (i, 0)),
                   pl.BlockSpec((rs, n), lambda i: (0, 0)),
                   pl.BlockSpec((tm, n), lambda i: (i, 0))],
        out_shape=[jax.ShapeDtypeStruct((m, n), F32), jax.ShapeDtypeStruct((rs, n), F32),
                   jax.ShapeDtypeStruct((m, n), norm_dtype)],
        scratch_shapes=[pltpu.VMEM((k, n), BF16), pltpu.VMEM((2, WEIGHT_CHUNK_ROWS, n), F32),
                        pltpu.SemaphoreType.DMA((2,))],
        compiler_params=_cparams("arbitrary"),
        name=name,
    )(a, a_s, w_stack, res, res_s, next_gain.reshape(1, n))


def _hgrn_seq_body(q_ref, z_ref, v_ref, g_ref, lb_ref, gn_ref, tri_ref, y_ref, st_out_ref,
                   st_ref, qs_ref, ks_ref, cum_ref, o_ref, *, hb):
    @pl.when(pl.program_id(2) == 0)
    def _():
        st_ref[...] = jnp.zeros_like(st_ref)

    tt = q_ref.shape[0]
    lb = lb_ref[...]
    z = z_ref[...]
    log_sig = jnp.minimum(z, 0.0) - jnp.log(1.0 + jnp.exp(-jnp.abs(z)))
    a, b = jnp.log(lb), jnp.log1p(-lb) + log_sig
    log_f = jnp.maximum(a, b) + jnp.log(1.0 + jnp.exp(-jnp.abs(a - b)))
    ks_ref[...] = (1.0 - lb) * _sigmoid(-z)
    q = q_ref[...]
    qs_ref[...] = q * _sigmoid(q)
    cum_ref[...] = jnp.dot(tri_ref[...], log_f, preferred_element_type=F32, precision=lax.Precision.HIGHEST)

    t_idx = lax.broadcasted_iota(jnp.int32, (GLA_BLOCK, 1), 0)
    half = GLA_BLOCK // 2

    def block(j, carry):
        rows = pl.ds(pl.multiple_of(j * GLA_BLOCK, GLA_BLOCK), GLA_BLOCK)
        for h in range(hb):
            cols = slice(h * HG_DK, (h + 1) * HG_DK)
            cb = cum_ref[rows, cols]
            qb = qs_ref[rows, cols]
            kb = ks_ref[rows, cols]
            vb = v_ref[rows, cols]
            last = cb[GLA_BLOCK - 1:GLA_BLOCK, :]
            st = st_ref[h]
            o = lax.dot_general((qb * jnp.exp(cb)).astype(BF16), st.astype(BF16), NT_DIMS,
                                preferred_element_type=F32)
            o_lo = jnp.zeros((half, LANES), F32)
            o_hi = jnp.zeros((half, LANES), F32)
            for s in range(GLA_BLOCK):
                cs = cb[s:s + 1, :]
                qk_hi = qb[half:, :] * kb[s:s + 1, :]
                e_hi = jnp.exp(cb[half:, :] - cs)
                if s >= half:
                    e_hi = jnp.where(t_idx[half:] >= s, e_hi, 0.0)
                g_hi = e_hi * qk_hi
                o_hi = o_hi + jnp.sum(g_hi, axis=-1, keepdims=True) * vb[s:s + 1, :]
                if s < half:
                    qk_lo = qb[:half, :] * kb[s:s + 1, :]
                    g_lo = jnp.where(t_idx[:half] >= s, jnp.exp(cb[:half, :] - cs), 0.0) * qk_lo
                    o_lo = o_lo + jnp.sum(g_lo, axis=-1, keepdims=True) * vb[s:s + 1, :]
            o_ref[rows, cols] = o + jnp.concatenate([o_lo, o_hi], axis=0)
            kd = kb * jnp.exp(last - cb)
            upd = lax.dot_general(vb.astype(BF16), kd.astype(BF16), TN_DIMS, preferred_element_type=F32)
            st_ref[h] = st * jnp.exp(last) + upd
        return carry

    lax.fori_loop(0, tt // GLA_BLOCK, block, 0)

    for h in range(hb):
        cols = slice(h * HG_DK, (h + 1) * HG_DK)
        o = o_ref[:, cols]
        ms = jnp.mean(o * o, axis=-1, keepdims=True)
        g = g_ref[:, cols]
        y_ref[:, cols] = (o * lax.rsqrt(ms + EPS) * gn_ref[0, h:h + 1, :] * (g * _sigmoid(g))).astype(BF16)

    @pl.when(pl.program_id(2) == pl.num_programs(2) - 1)
    def _():
        st_out_ref[0] = st_ref[...]


def _hgrn_seq(proj, lower_bound, norm_gain, *, batch, seq_len, tt=256, hb=8):
    m, n4 = proj.shape
    d = n4 // 4
    heads = d // HG_DK
    hg = heads // hb
    nt = seq_len // tt
    wcols = hb * HG_DK
    tri = (jnp.arange(tt)[:, None] >= jnp.arange(tt)[None, :]) & (
        jnp.arange(tt)[:, None] // GLA_BLOCK == jnp.arange(tt)[None, :] // GLA_BLOCK)
    tri = tri.astype(F32)

    def col(sec):
        return pl.BlockSpec((tt, wcols), lambda b, h, t, sec=sec: (b * nt + t, sec * hg + h))

    y, st = pl.pallas_call(
        functools.partial(_hgrn_seq_body, hb=hb),
        grid=(batch, hg, nt),
        in_specs=[col(0), col(1), col(2), col(3),
                  pl.BlockSpec((1, wcols), lambda b, h, t: (0, h)),
                  pl.BlockSpec((1, hb, HG_DK), lambda b, h, t: (h, 0, 0)),
                  pl.BlockSpec((tt, tt), lambda b, h, t: (0, 0))],
        out_specs=[pl.BlockSpec((tt, wcols), lambda b, h, t: (b * nt + t, h)),
                   pl.BlockSpec((1, hb, HG_DK, HG_DK), lambda b, h, t: (b, h, 0, 0))],
        out_shape=[jax.ShapeDtypeStruct((m, d), BF16),
                   jax.ShapeDtypeStruct((batch, heads, HG_DK, HG_DK), F32)],
        scratch_shapes=[pltpu.VMEM((hb, HG_DK, HG_DK), F32), pltpu.VMEM((tt, wcols), F32),
                        pltpu.VMEM((tt, wcols), F32), pltpu.VMEM((tt, wcols), F32),
                        pltpu.VMEM((tt, wcols), F32)],
        compiler_params=_cparams("parallel", "parallel", "arbitrary"),
        name="hgrn_seq",
    )(proj, proj, proj, proj, lower_bound.reshape(1, d), norm_gain.reshape(hg, hb, HG_DK), tri)
    return y, st


def _hgrn_step_body(q_ref, z_ref, v_ref, g_ref, lb_ref, gn_ref, s_ref, y_ref, s_out_ref):
    heads = s_ref.shape[1]
    for h in range(heads):
        lb = lb_ref[h]
        z = z_ref[0, h]
        forget = lb + (1.0 - lb) * _sigmoid(z)
        k = (1.0 - lb) * _sigmoid(-z)
        q = q_ref[0, h]
        q = q * _sigmoid(q)
        s_new = forget * s_ref[0, h] + k * v_ref[0, h]
        s_out_ref[0, h] = s_new
        o = jnp.sum(q * s_new, axis=0, keepdims=True)
        ms = jnp.mean(o * o, axis=-1, keepdims=True)
        g = g_ref[0, h]
        y_ref[0, h] = o * lax.rsqrt(ms + EPS) * gn_ref[h] * (g * _sigmoid(g))


def _hgrn_step(proj, lower_bound, norm_gain, state):
    bsz, heads, dk, dv = state.shape
    d = heads * dk
    q, z, v, g = (proj[:, s * d:(s + 1) * d] for s in range(4))
    col = lambda a: a.reshape(bsz, heads, dk, 1)
    row = lambda a: a.reshape(bsz, heads, 1, dv)
    col_spec = pl.BlockSpec((1, heads, dk, 1), lambda b: (b, 0, 0, 0))
    row_spec = pl.BlockSpec((1, heads, 1, dv), lambda b: (b, 0, 0, 0))
    st_spec = pl.BlockSpec((1, heads, dk, dv), lambda b: (b, 0, 0, 0))
    y, s_new = pl.pallas_call(
        _hgrn_step_body,
        grid=(bsz,),
        in_specs=[col_spec, col_spec, row_spec, row_spec,
                  pl.BlockSpec((heads, dk, 1), lambda b: (0, 0, 0)),
                  pl.BlockSpec((heads, 1, dv), lambda b: (0, 0, 0)),
                  st_spec],
        out_specs=[row_spec, st_spec],
        out_shape=[jax.ShapeDtypeStruct((bsz, heads, 1, dv), F32),
                   jax.ShapeDtypeStruct(state.shape, F32)],
        compiler_params=_cparams("parallel"),
        name="hgrn_step",
    )(col(q), col(z), row(v), row(g), lower_bound.reshape(heads, dk, 1),
      norm_gain.reshape(heads, 1, dv), state)
    return y.reshape(bsz, d), s_new


def _band_block(q, k, v, scale, has_prev):
    n, nkeys = q.shape[0], k.shape[0]
    qi = lax.broadcasted_iota(jnp.int32, (n, nkeys), 0)
    ki = lax.broadcasted_iota(jnp.int32, (n, nkeys), 1)
    s = lax.dot_general(q.astype(BF16), k.astype(BF16), NT_DIMS, preferred_element_type=F32) * scale
    visible = ((ki >= qi) & (ki <= qi + n)) if has_prev else (ki <= qi)
    s = jnp.where(visible, s, NEG_BIG)
    m = jnp.max(s, axis=-1, keepdims=True)
    p = jnp.exp(s - m)
    l = jnp.sum(p, axis=-1, keepdims=True)
    acc = jnp.dot(p.astype(BF16), v.astype(BF16), preferred_element_type=F32)
    return acc / l, m + jnp.log(l)


def _dil_attn_seq_body(*refs, seq_len, hb):
    n_in = 3 * N_GROUPS
    y_ref, og_ref, lse_ref = refs[hb * n_in:]
    scale = ATT_DH ** -0.5
    for g, (win, dil) in enumerate(DILATED_GROUPS):
        nk = win // dil
        for r in range(dil):
            for n in range(seq_len // dil // nk):
                def rows(first_blk, n_blk):
                    start, size = r + dil * nk * first_blk, nk * n_blk
                    return pl.ds(start, size, stride=dil) if dil > 1 else pl.ds(start, size)
                cur = rows(n, 1)
                keys = rows(n - 1, 2) if n else cur
                for h in range(hb):
                    q_ref, k_ref, v_ref = refs[h * n_in + 3 * g:h * n_in + 3 * g + 3]
                    o, lse = _band_block(q_ref[0, cur, :], k_ref[0, keys, :], v_ref[0, keys, :], scale, n > 0)
                    og_ref[h * N_GROUPS + g, cur, :] = o
                    lse_ref[h * N_GROUPS + g, cur, :] = jnp.broadcast_to(lse, (nk, LANES))
    for h in range(hb):
        lse = [lse_ref[h * N_GROUPS + g] for g in range(N_GROUPS)]
        og = [og_ref[h * N_GROUPS + g] for g in range(N_GROUPS)]
        m = jnp.maximum(jnp.maximum(lse[0], lse[1]), lse[2])
        w = [jnp.exp(x - m) for x in lse]
        y = (w[0] * og[0] + w[1] * og[1] + w[2] * og[2]) / (w[0] + w[1] + w[2])
        y_ref[0, :, h * ATT_DH:(h + 1) * ATT_DH] = y.astype(BF16)


def _dil_attn_seq(qkv, *, batch, seq_len, heads, hb=1):
    in_specs = []
    for h in range(hb):
        for s in range(3 * N_GROUPS):
            in_specs.append(pl.BlockSpec((1, seq_len, ATT_DH),
                                         lambda b, hg, s=s, h=h: (b, 0, s * heads + hg * hb + h)))
    return pl.pallas_call(
        functools.partial(_dil_attn_seq_body, seq_len=seq_len, hb=hb),
        grid=(batch, heads // hb),
        in_specs=in_specs,
        out_specs=pl.BlockSpec((1, seq_len, hb * ATT_DH), lambda b, hg: (b, 0, hg)),
        out_shape=jax.ShapeDtypeStruct((batch, seq_len, heads * ATT_DH), BF16),
        scratch_shapes=[pltpu.VMEM((hb * N_GROUPS, seq_len, ATT_DH), F32),
                        pltpu.VMEM((hb * N_GROUPS, seq_len, LANES), F32)],
        compiler_params=_cparams("parallel", "parallel"),
        name="dil_attn_seq",
    )(*([qkv] * (hb * 3 * N_GROUPS)))


def _bf16_round(x):
    return x.astype(BF16).astype(F32)


def _step_attention(q, k_new, v_new, k_c, v_c, scale):
    qb = _bf16_round(q)
    s_c = jnp.sum(_bf16_round(k_c) * qb[None], axis=-1, keepdims=True) * scale
    m = jnp.max(s_c, axis=0)
    if k_new is not None:
        s_n = jnp.sum(qb * _bf16_round(k_new), axis=-1, keepdims=True) * scale
        m = jnp.maximum(m, s_n)
    p_c = jnp.exp(s_c - m[None])
    l = jnp.sum(p_c, axis=0)
    acc = jnp.sum(_bf16_round(p_c) * _bf16_round(v_c), axis=0)
    if k_new is not None:
        p_n = jnp.exp(s_n - m)
        l = l + p_n
        acc = acc + _bf16_round(p_n) * _bf16_round(v_new)
    return acc / l, m + jnp.log(l)


def _dil_attn_step_body(*refs):
    new = refs[:3 * N_GROUPS]
    caches = refs[3 * N_GROUPS:5 * N_GROUPS]
    y_ref = refs[5 * N_GROUPS]
    scale = ATT_DH ** -0.5
    outs, lses = [], []
    for g in range(N_GROUPS):
        o, lse = _step_attention(new[3 * g][0, 0], new[3 * g + 1][0, 0], new[3 * g + 2][0, 0],
                                 caches[2 * g][...], caches[2 * g + 1][...], scale)
        outs.append(o)
        lses.append(lse)
    m = jnp.maximum(jnp.maximum(lses[0], lses[1]), lses[2])
    w = [jnp.exp(x - m) for x in lses]
    y_ref[0] = (w[0] * outs[0] + w[1] * outs[1] + w[2] * outs[2]) / (w[0] + w[1] + w[2])


def _dil_attn_step(qkv, caches, layer, *, heads):
    bsz = qkv.shape[0]
    new = qkv.reshape(bsz, 3 * N_GROUPS, heads, ATT_DH)
    args, in_specs = [], []
    for s in range(3 * N_GROUPS):
        args.append(new)
        in_specs.append(pl.BlockSpec((1, 1, heads, ATT_DH), lambda b, s=s: (b, s, 0, 0)))
    for g, (win, dil) in enumerate(DILATED_GROUPS):
        nk = win // dil
        for buf in caches[g]:
            assert buf.shape[1:] == (bsz, win, heads, ATT_DH)
            args.append(buf.reshape(buf.shape[0], bsz, nk, dil, heads, ATT_DH))
            in_specs.append(pl.BlockSpec((None, None, nk, None, heads, ATT_DH),
                                         lambda b: (layer, b, 0, 0, 0, 0)))
    y = pl.pallas_call(
        _dil_attn_step_body,
        grid=(bsz,),
        in_specs=in_specs,
        out_specs=pl.BlockSpec((1, heads, ATT_DH), lambda b: (b, 0, 0)),
        out_shape=jax.ShapeDtypeStruct((bsz, heads, ATT_DH), F32),
        compiler_params=_cparams("parallel"),
        name="dil_attn_step",
    )(*args)
    return y.reshape(bsz, heads * ATT_DH)


def _softmax_pv(s, v):
    m = jnp.max(s, axis=-1, keepdims=True)
    p = jnp.exp(s - m)
    l = jnp.sum(p, axis=-1, keepdims=True)
    return jnp.dot(p.astype(BF16), v.astype(BF16), preferred_element_type=F32) / l


def _memory_kv_body(x_ref, g_ref, w_ref, o_ref, wb_ref, xn_ref):
    _cast_weight(w_ref, wb_ref, 0)

    def body(c, carry):
        rows = pl.ds(pl.multiple_of(c * CAST_ROWS, CAST_ROWS), CAST_ROWS)
        x = x_ref[rows, :]
        ms = jnp.mean(x * x, axis=-1, keepdims=True)
        xn_ref[rows, :] = (x * lax.rsqrt(ms + EPS) * g_ref[0]).astype(BF16)
        return carry
    lax.fori_loop(0, x_ref.shape[0] // CAST_ROWS, body, 0)
    o_ref[0] = jnp.dot(xn_ref[...], wb_ref[0], preferred_element_type=F32)


def _memory_kv(mem, gains, w_kv, *, tn=512):
    rows, d = mem.shape
    layers, _, n = w_kv.shape
    assert rows % CAST_ROWS == 0 and d % CAST_ROWS == 0 and n % tn == 0
    return pl.pallas_call(
        _memory_kv_body,
        grid=(layers, n // tn),
        in_specs=[pl.BlockSpec((rows, d), lambda l, j: (0, 0)),
                  pl.BlockSpec((1, 1, d), lambda l, j: (l, 0, 0)),
                  pl.BlockSpec((1, d, tn), lambda l, j: (l, 0, j))],
        out_specs=pl.BlockSpec((1, rows, tn), lambda l, j: (l, 0, j)),
        out_shape=jax.ShapeDtypeStruct((layers, rows, n), F32),
        scratch_shapes=[pltpu.VMEM((1, d, tn), BF16), pltpu.VMEM((rows, d), BF16)],
        compiler_params=_cparams("parallel", "parallel"),
        name="memory_kv",
    )(mem, gains.reshape(layers, 1, d), w_kv)


def _stream_weight(w_hbm, wb_ref, stage_ref, sem):
    chunk = stage_ref.shape[1]
    k, n = wb_ref.shape
    n_chunks = k // chunk

    def copy(c):
        return pltpu.make_async_copy(w_hbm.at[pl.ds(c * chunk, chunk), :], stage_ref.at[c % 2, :, pl.ds(0, n)],
                                     sem.at[c % 2])
    copy(0).start()
    for c in range(n_chunks):
        if c + 1 < n_chunks:
            copy(c + 1).start()
        copy(c).wait()
        wb_ref[pl.ds(c * chunk, chunk), :] = stage_ref[c % 2, :, 0:n].astype(BF16)


def _mix_xattn_block_body(y_ref, ys_ref, r_ref, rs_ref, wm_hbm, gq_ref, wq_hbm, wo_hbm, kv_ref, ks_ref, vs_ref,
                          gn_ref, o_ref, os_ref, xnn_ref, wmb_ref, wqb_ref, wob_ref, att_ref, stage_ref, sem,
                          *, heads, nbat_s, mix_layer, layer):
    scale = XA_DH ** -0.5
    hd = heads * XA_DH

    def rms(x, g_ref):
        ms = jnp.mean(x * x, axis=-1, keepdims=True)
        return (x * lax.rsqrt(ms + EPS) * g_ref[...]).astype(BF16)

    @pl.when(pl.program_id(0) == 0)
    def _():
        _stream_weight(wm_hbm.at[mix_layer], wmb_ref, stage_ref, sem)
        _stream_weight(wq_hbm.at[layer], wqb_ref, stage_ref, sem)
        _stream_weight(wo_hbm.at[layer], wob_ref, stage_ref, sem)
        xs = rs_ref[...] + jnp.dot(ys_ref[...], wmb_ref[...], preferred_element_type=F32)
        q_s = jnp.dot(rms(xs, gq_ref), wqb_ref[...], preferred_element_type=F32)
        att_ref[...] = jnp.zeros_like(att_ref)
        for b in range(nbat_s):
            for h in range(heads):
                cols = slice(h * XA_DH, (h + 1) * XA_DH)
                qb = _bf16_round(q_s[b:b + 1, cols])
                s = jnp.sum(_bf16_round(ks_ref[b, :, h, :]) * qb, axis=-1, keepdims=True) * scale
                p = jnp.exp(s - jnp.max(s, axis=0, keepdims=True))
                acc = jnp.sum(_bf16_round(p) * _bf16_round(vs_ref[b, :, h, :]), axis=0, keepdims=True)
                att_ref[b:b + 1, cols] = acc / jnp.sum(p, axis=0, keepdims=True)
        os_ref[...] = xs + jnp.dot(att_ref[0:xs.shape[0], :].astype(BF16), wob_ref[...], preferred_element_type=F32)

    x = r_ref[...] + jnp.dot(y_ref[...], wmb_ref[...], preferred_element_type=F32)
    q = jnp.dot(rms(x, gq_ref), wqb_ref[...], preferred_element_type=F32)
    for h in range(heads):
        cols = slice(h * XA_DH, (h + 1) * XA_DH)
        k = kv_ref[0][:, cols]
        v = kv_ref[0][:, hd + h * XA_DH:hd + (h + 1) * XA_DH]
        s = lax.dot_general(q[:, cols].astype(BF16), k.astype(BF16), NT_DIMS, preferred_element_type=F32) * scale
        att_ref[:, cols] = _softmax_pv(s, v)
    x = x + jnp.dot(att_ref[...].astype(BF16), wob_ref[...], preferred_element_type=F32)
    o_ref[...] = x
    xnn_ref[...] = rms(x, gn_ref)


def _mix_xattn_block(y, y_s, xp, xs, w_mix, mix_layer, gain, w_q, w_o, layer, mem_kv, mem_k_s, mem_v_s, next_gain,
                     *, nbat, t, nbat_s, tm=256):
    m, d = xp.shape
    rs = xs.shape[0]
    hd = w_q.shape[2]
    heads = hd // XA_DH
    mem_len = mem_kv.shape[2]
    tiles = t // tm
    assert t % tm == 0 and tm >= rs and mem_k_s.shape[1:] == (nbat_s, mem_len, heads, XA_DH)
    assert d % WEIGHT_CHUNK_ROWS == 0 and hd % WEIGHT_CHUNK_ROWS == 0 and hd <= d
    once = dict(pipeline_mode=pl.Buffered(1))
    mem_s_spec = pl.BlockSpec((None, nbat_s, mem_len, heads, XA_DH), lambda i: (layer, 0, 0, 0, 0), **once)
    row = pl.BlockSpec((tm, d), lambda i: (i, 0))
    srow = pl.BlockSpec((rs, d), lambda i: (0, 0))
    vec = pl.BlockSpec((1, d), lambda i: (0, 0))
    hbm = pl.BlockSpec(memory_space=pl.ANY)
    return pl.pallas_call(
        functools.partial(_mix_xattn_block_body, heads=heads, nbat_s=nbat_s, mix_layer=mix_layer, layer=layer),
        grid=(m // tm,),
        in_specs=[row, srow, row, srow, hbm, vec, hbm, hbm,
                  pl.BlockSpec((None, 1, mem_len, 2 * hd), lambda i: (layer, i // tiles, 0, 0)),
                  mem_s_spec, mem_s_spec, vec],
        out_specs=[row, srow, row],
        out_shape=[jax.ShapeDtypeStruct((m, d), F32), jax.ShapeDtypeStruct((rs, d), F32),
                   jax.ShapeDtypeStruct((m, d), BF16)],
        scratch_shapes=[pltpu.VMEM((d, d), BF16), pltpu.VMEM((d, hd), BF16), pltpu.VMEM((hd, d), BF16),
                        pltpu.VMEM((tm, hd), F32), pltpu.VMEM((2, WEIGHT_CHUNK_ROWS, d), F32),
                        pltpu.SemaphoreType.DMA((2,))],
        compiler_params=_cparams("arbitrary"),
        name="mix_xattn_block",
    )(y, y_s, xp, xs, w_mix, gain.reshape(1, d), w_q, w_o, mem_kv, mem_k_s, mem_v_s, next_gain.reshape(1, d))


TN_COLS = 512


def _rope_tables(pos):
    rows = pos.shape[0]
    half = ROPE_DIM // 2
    inv_freq = ROPE_THETA ** (-jnp.arange(half, dtype=F32) * 2.0 / ROPE_DIM)
    ang = pos.astype(F32)[:, None] * inv_freq[None, :]
    cos, sin = jnp.cos(ang), jnp.sin(ang)
    rest = jnp.zeros((rows, LANES - ROPE_DIM), F32)
    zh = jnp.zeros((rows, half), F32)
    return (jnp.concatenate([cos, cos, rest + 1.0], axis=1), jnp.concatenate([-sin, zh, rest], axis=1),
            jnp.concatenate([zh, sin, rest], axis=1))


def _pad_rows(a, rows):
    return jnp.pad(a, ((0, rows - a.shape[0]), (0, 0)))


ROWS_S = 16
TM_ROWS = 512
TN_WIDE = 1024
TM_GATE_UP = 2048


def _ffn(xp, xs, xn, gain, w_gu, w_down, layer, tag, next_gain, norm_dtype=BF16):
    nf = w_down.shape[1] // TN_COLS
    (hid,), (hid_s,) = _proj(xn, xs, gain, w_gu, layer, w_col_blocks=(0, nf), n_col_blocks=nf,
                             epilogue=_ep_swiglu, out_dtypes=(BF16,), name=tag + "_gate_up", tm=TM_GATE_UP,
                             col_chunk=MXU_COLS)
    return _res_proj_norm_big(hid, hid_s, w_down, layer, xp, xs, 0.5, next_gain, name=tag + "_down",
                              norm_dtype=norm_dtype)


def _conv_mixer(xp, xs, xn, gain, w_in, w_conv, layer, state, *, nbat, t, nbat_s):
    d = xp.shape[1]
    nb = d // TN_COLS
    tm = min(1024, t)
    tiles = t // tm
    w_spec = pl.BlockSpec((CONV_W, TN_COLS), lambda j, i: (0, j))
    st_spec = pl.BlockSpec((ROWS_S, TN_COLS), lambda j, i: (0, j))
    tail_out = (jax.ShapeDtypeStruct((nbat * SUBLANES, d), F32),
                pl.BlockSpec((SUBLANES, TN_COLS), lambda j, i: (i // tiles, j)))
    (gated, tail), (gated_s, u_s) = _proj(
        xn, xs, gain, w_in, layer, w_col_blocks=(0, nb, 2 * nb), n_col_blocks=nb, name="conv_in", tm=tm,
        epilogue=functools.partial(_ep_conv_seq, tiles_per_seq=tiles), out_dtypes=(BF16,), more_outs=(tail_out,),
        epilogue_s=_ep_conv_step, out_s_dtypes=(BF16, F32),
        extra=(w_conv,), extra_specs=(w_spec,),
        extra_s=(_pad_rows(state[:, 0], ROWS_S), _pad_rows(state[:, 1], ROWS_S), w_conv),
        extra_s_specs=(st_spec, st_spec, w_spec),
        col_chunk=MXU_COLS,
        scratch=(pltpu.VMEM((SUBLANES, TN_COLS), F32),
                 pltpu.VMEM((TN_COLS // MXU_COLS, tm + SUBLANES, MXU_COLS), F32)))
    state_p = tail.reshape(nbat, SUBLANES, d)[:, SUBLANES - (CONV_W - 1):]
    state_s = jnp.stack([state[:, 1], u_s[:nbat_s]], axis=1)
    return gated, gated_s, state_p, state_s


def _hgrn_mixer(xp, xs, xn, gain, w_in, lower_bound, norm_gain, layer, state, *, nbat, t, nbat_s):
    (proj,), (proj_s,) = _proj(xn, xs, gain, w_in, layer, w_col_blocks=(0,), n_col_blocks=w_in.shape[2] // TN_WIDE,
                               epilogue=_ep_plain, out_dtypes=(F32,), name="hgrn_in", tn=TN_WIDE)
    y, st = _hgrn_seq(proj, lower_bound, norm_gain, batch=nbat, seq_len=t)
    y_s, state_s = _hgrn_step(proj_s[:nbat_s], lower_bound, norm_gain, state)
    return y, _pad_rows(y_s, ROWS_S).astype(BF16), jnp.swapaxes(st, -1, -2), state_s


def _attn_mixer(xp, xs, xn, gain, w_qkv, layer, caches, *, nbat, t, nbat_s):
    d = xp.shape[1]
    heads = d // ATT_DH
    n_sec = 3 * N_GROUPS
    sec_blocks = d // TN_WIDE
    tm = min(1024, t)
    tiles = t // tm
    tab_spec = pl.BlockSpec((tm, LANES), lambda j, i: (i % tiles, 0))
    tab_s_spec = pl.BlockSpec((ROWS_S, LANES), lambda j, i: (0, 0))
    (qkv,), (qkv_s,) = _proj(
        xn, xs, gain, w_qkv, layer, w_col_blocks=(0,), n_col_blocks=n_sec * sec_blocks,
        epilogue=_ep_rotary, out_dtypes=(F32,), name="attn_qkv", tm=tm, tn=TN_WIDE,
        alt_epilogue=_ep_plain, alt_pred=lambda j: (j // sec_blocks) % 3 == 2,
        extra=_rope_tables(jnp.arange(t)), extra_specs=(tab_spec,) * 3,
        extra_s=_rope_tables(jnp.full((ROWS_S,), PAST_LEN, jnp.int32)), extra_s_specs=(tab_s_spec,) * 3)
    y = _dil_attn_seq(qkv.reshape(nbat, t, n_sec * d), batch=nbat, seq_len=t, heads=heads).reshape(nbat * t, d)
    y_s = _dil_attn_step(qkv_s[:nbat_s], caches, layer, heads=heads)
    q5 = qkv.reshape(nbat, t, N_GROUPS, 3, heads, ATT_DH)
    q5_s = qkv_s[:nbat_s].reshape(nbat_s, 1, N_GROUPS, 3, heads, ATT_DH)
    rows_p, rows_s = [], []
    for g, (win, _) in enumerate(DILATED_GROUPS):
        keep = min(win, t)
        rows_p += [q5[:, t - keep:, g, 1], q5[:, t - keep:, g, 2]]
        rows_s += [q5_s[:, :, g, 1], q5_s[:, :, g, 2]]
    return y, _pad_rows(y_s, ROWS_S).astype(BF16), rows_p, rows_s


def kernel(x_prompt, x_sample, state_conv, state_hgrn,
           cache_win_k0, cache_win_v0, cache_win_k1, cache_win_v1, cache_win_k2, cache_win_v2,
           cache_mem_k, cache_mem_v, mem_prompt,
           norm_ffn1, ffn1_w_gu, ffn1_w_down, norm_mix,
           conv_w_in, conv_w, conv_w_out,
           hgrn_w_in, hgrn_lb_logits, hgrn_norm, hgrn_w_out,
           attn_w_qkv, attn_w_out,
           norm_mem, xattn_w_kv, norm_xattn, xattn_w_q, xattn_w_o,
           norm_ffn2, ffn2_w_gu, ffn2_w_down, norm_final):
    batch, seq, d = x_prompt.shape
    dec_batch, dec_seq, _ = x_sample.shape
    assert dec_seq == 1 and dec_batch <= ROWS_S
    depth = norm_ffn1.shape[0]
    mem_len = mem_prompt.shape[1]
    xa_hd = xattn_w_q.shape[2]
    xa_heads = xa_hd // XA_DH
    win_cache = [(cache_win_k0, cache_win_v0), (cache_win_k1, cache_win_v1), (cache_win_k2, cache_win_v2)]
    sizes = dict(nbat=batch, t=seq, nbat_s=dec_batch)

    lb_p = jax.nn.softmax(hgrn_lb_logits.astype(F32), axis=0)
    lower_bounds = jnp.cumsum(lb_p, axis=0) - lb_p[0]

    xp = x_prompt.reshape(batch * seq, d)
    xs = _pad_rows(x_sample.reshape(dec_batch, d), ROWS_S)
    mem_kv = _memory_kv(mem_prompt.reshape(batch * mem_len, d), norm_mem, xattn_w_kv)
    mem_kv = mem_kv.reshape(depth, batch, mem_len, 2 * xa_hd)
    p_mem_k = mem_kv[..., :xa_hd].reshape(depth, batch, mem_len, xa_heads, XA_DH)
    p_mem_v = mem_kv[..., xa_hd:].reshape(depth, batch, mem_len, xa_heads, XA_DH)
    p_conv, p_hgrn, p_win, s_conv, s_hgrn, s_win = ([] for _ in range(6))
    xn = _rms_norm(xp, norm_ffn1[0], tm=TM_ROWS, out_dtype=BF16, name="first_norm")
    for i in range(depth):
        j, kind = divmod(i, N_MIXERS)
        xp, xs, xn = _ffn(xp, xs, xn, norm_ffn1[i], ffn1_w_gu, ffn1_w_down, i, "ffn1", norm_mix[i])
        if kind == 0:
            y, y_s, st_p, st_s = _conv_mixer(xp, xs, xn, norm_mix[i], conv_w_in, conv_w[j], j, state_conv[j], **sizes)
            w_mix = conv_w_out
            p_conv.append(st_p)
            s_conv.append(st_s)
        elif kind == 1:
            y, y_s, st_p, st_s = _hgrn_mixer(xp, xs, xn, norm_mix[i], hgrn_w_in, lower_bounds[i], hgrn_norm[j],
                                             j, state_hgrn[j], **sizes)
            w_mix = hgrn_w_out
            p_hgrn.append(st_p)
            s_hgrn.append(st_s)
        else:
            y, y_s, rows_p, rows_s = _attn_mixer(xp, xs, xn, norm_mix[i], attn_w_qkv, j, win_cache, **sizes)
            w_mix = attn_w_out
            p_win.append(rows_p)
            s_win.append(rows_s)
        xp, xs, xn = _mix_xattn_block(y, y_s, xp, xs, w_mix, j, norm_xattn[i], xattn_w_q, xattn_w_o, i, mem_kv,
                                      cache_mem_k, cache_mem_v, norm_ffn2[i], **sizes)
        last = i == depth - 1
        xp, xs, xn = _ffn(xp, xs, xn, norm_ffn2[i], ffn2_w_gu, ffn2_w_down, i, "ffn2",
                          norm_final if last else norm_ffn1[i + 1], F32 if last else BF16)
    y_prompt = xn.reshape(batch, seq, d)
    y_sample = _rms_norm(xs, norm_final, tm=ROWS_S, out_dtype=F32, name="final_norm_s")[:dec_batch]
    y_sample = y_sample.reshape(dec_batch, 1, d)

    stack_win = lambda rows: [jnp.stack([r[g] for r in rows]) for g in range(2 * N_GROUPS)]
    return (y_prompt, y_sample,
            jnp.stack(p_conv), jnp.stack(p_hgrn), *stack_win(p_win), p_mem_k, p_mem_v,
            jnp.stack(s_conv), jnp.stack(s_hgrn), *stack_win(s_win))
```
